```python
import jax, jax.numpy as jnp
from jax import lax
import numpy as np

D_MODEL = 1024
BATCH = 32
SEQ = 256
DEPTH = 2
DEC_BATCH = 4
DEC_SEQ = 2048
PAST_LEN = 256

GRID_W = 64
HEAD_DIM = 64
N_HEADS = D_MODEL // 128
N_KV_HEADS = N_HEADS // 4
KV_GROUP = N_HEADS // N_KV_HEADS
ATTN_CH = N_HEADS * HEAD_DIM
KV_CH = N_KV_HEADS * HEAD_DIM
CONV_CH = D_MODEL // 2
CONV_WIDTH = 31
POOL_WINDOWS = (2, 4, 8, 16)
POOL_GROUPS = 4
POOL_CH = D_MODEL // 2
POOL_GROUP_CH = POOL_CH // POOL_GROUPS
FOURIER_HEADS = 4
FOURIER_CH = D_MODEL // 2
FOURIER_HEAD_CH = FOURIER_CH // FOURIER_HEADS
MIX_WIDTH = ATTN_CH + CONV_CH + POOL_CH + FOURIER_CH
IN_COLS = ATTN_CH + 2 * KV_CH + 2 * CONV_CH + POOL_CH + FOURIER_CH
N_EXPERT_GROUPS = 4
EXPERTS_PER_GROUP = 4
N_EXPERTS = N_EXPERT_GROUPS * EXPERTS_PER_GROUP
TOP_K_IN_GROUP = 2
EXPERT_FF = D_MODEL // 4
ROPE_THETA = 10000.0
ROPE_PAIRS_PER_AXIS = HEAD_DIM // 4
Q_BLOCK = 128
EPS = 1e-6
N_MOD = 6

kernel_name = 'hybrid_flow_prefix_trunk_step'


def rms_norm(x, g):
    xf = x.astype(jnp.float32)
    y = xf * lax.rsqrt(jnp.mean(xf * xf, axis=-1, keepdims=True) + EPS)
    return (y * g.astype(jnp.float32)).astype(x.dtype)


def grid_rope(n_tokens):
    rows = n_tokens // GRID_W
    row = jnp.repeat(jnp.arange(rows), GRID_W).astype(jnp.float32)
    col = jnp.tile(jnp.arange(GRID_W), rows).astype(jnp.float32)
    inv = ROPE_THETA ** (-jnp.arange(ROPE_PAIRS_PER_AXIS, dtype=jnp.float32) / ROPE_PAIRS_PER_AXIS)
    ang = jnp.concatenate([row[:, None] * inv, col[:, None] * inv], axis=-1)
    return jnp.cos(ang)[:, None, :], jnp.sin(ang)[:, None, :]


def apply_rope(x, cos, sin):
    xf = x.astype(jnp.float32).reshape(*x.shape[:-1], HEAD_DIM // 2, 2)
    x0, x1 = xf[..., 0], xf[..., 1]
    out = jnp.stack([x0 * cos - x1 * sin, x0 * sin + x1 * cos], axis=-1)
    return out.reshape(x.shape).astype(x.dtype)


def gqa_attention(q, k, v):
    B, Lq = q.shape[0], q.shape[1]
    nb = Lq // Q_BLOCK
    qb = q.reshape(B, nb, Q_BLOCK, N_KV_HEADS, KV_GROUP, HEAD_DIM).transpose(1, 0, 2, 3, 4, 5)
    scale = HEAD_DIM ** -0.5

    def block(qblk):
        s = jnp.einsum('bqkgd,bskd->bkgqs', qblk, k).astype(jnp.float32) * scale
        p = jax.nn.softmax(s, axis=-1).astype(v.dtype)
        return jnp.einsum('bkgqs,bskd->bqkgd', p, v)

    o = lax.map(block, qb)
    return o.transpose(1, 0, 2, 3, 4, 5).reshape(B, Lq, ATTN_CH)


def conformer_conv(u2, w, b, ln_g, ln_b):
    a, gate = jnp.split(u2, 2, axis=-1)
    u = a * jax.nn.sigmoid(gate)
    y = lax.conv_general_dilated(u, w[:, None, :].astype(u.dtype), window_strides=(1,),
                                 padding=[(CONV_WIDTH // 2, CONV_WIDTH // 2)],
                                 dimension_numbers=('NWC', 'WIO', 'NWC'),
                                 feature_group_count=CONV_CH)
    yf = y.astype(jnp.float32) + b.astype(jnp.float32)
    mu = jnp.mean(yf, axis=-1, keepdims=True)
    var = jnp.mean(jnp.square(yf - mu), axis=-1, keepdims=True)
    yn = (yf - mu) * lax.rsqrt(var + EPS) * ln_g.astype(jnp.float32) + ln_b.astype(jnp.float32)
    return jax.nn.silu(yn).astype(u.dtype)


def multi_scale_pool(u, pool_w, pool_scale):
    B, L, _ = u.shape
    ug = u.reshape(B, L, POOL_GROUPS, POOL_GROUP_CH).astype(jnp.float32)
    cs = jnp.cumsum(ug, axis=1)
    cs = jnp.concatenate([jnp.zeros_like(cs[:, :1]), cs], axis=1)
    t = jnp.arange(L)
    outs = []
    for g, w in enumerate(POOL_WINDOWS):
        lo = jnp.clip(t - w // 2, 0, L)
        hi = jnp.clip(t + w - w // 2, 0, L)
        win_sum = cs[:, hi, g] - cs[:, lo, g]
        mean = win_sum / (hi - lo).astype(jnp.float32)[None, :, None]
        outs.append(mean - ug[:, :, g])
    pooled = jnp.stack(outs, axis=2).astype(u.dtype)
    mixed = jnp.einsum('blgc,gcd->blgd', pooled, pool_w)
    return mixed.reshape(B, L, POOL_CH) * pool_scale


def fourier_mix(u):
    B, L, _ = u.shape
    uh = u.reshape(B, L, FOURIER_HEADS, FOURIER_HEAD_CH).astype(jnp.float32)
    f = jnp.fft.fftn(uh, axes=(1, 3), norm='ortho').real
    return f.reshape(B, L, FOURIER_CH).astype(u.dtype)


def parallel_mixer(h, l, p, ctx_k, ctx_v, rope):
    B, L, _ = h.shape
    proj = h @ p['w_in'][l]
    o1 = ATTN_CH
    o2 = o1 + KV_CH
    o3 = o2 + KV_CH
    o4 = o3 + 2 * CONV_CH
    o5 = o4 + POOL_CH
    q, k, v, conv_in, pool_in, four_in = jnp.split(proj, [o1, o2, o3, o4, o5], axis=-1)
    q = rms_norm(q.reshape(B, L, N_HEADS, HEAD_DIM), p['q_norm'][l])
    k = rms_norm(k.reshape(B, L, N_KV_HEADS, HEAD_DIM), p['k_norm'][l])
    v = v.reshape(B, L, N_KV_HEADS, HEAD_DIM)
    if rope is None:
        keys, vals = k, v
    else:
        cos, sin = rope
        q = apply_rope(q, cos, sin)
        k_lat = apply_rope(k, cos, sin)
        keys = jnp.concatenate([ctx_k.astype(k.dtype), k_lat], axis=1)
        vals = jnp.concatenate([ctx_v.astype(v.dtype), v], axis=1)
    attn = gqa_attention(q, keys, vals)
    conv = conformer_conv(conv_in, p['conv_w'][l], p['conv_b'][l], p['conv_ln_g'][l], p['conv_ln_b'][l])
    pool = multi_scale_pool(pool_in, p['pool_w'][l], p['pool_scale'][l])
    four = fourier_mix(four_in)
    out = jnp.concatenate([attn, conv, pool, four], axis=-1) @ p['w_out'][l]
    return out, k, v


def hier_moe(h, l, p):
    B, L, D = h.shape
    T = B * L
    x = h.reshape(T, D)
    g_logits = (x @ p['router_group_w'][l] + p['router_group_b'][l]).astype(jnp.float32)
    g_prob = jax.nn.softmax(g_logits, axis=-1)
    g_idx = jnp.argmax(g_logits, axis=-1)
    g_w = jnp.take_along_axis(g_prob, g_idx[:, None], axis=-1)
    e_logits = (x @ p['router_expert_w'][l] + p['router_expert_b'][l]).astype(jnp.float32)
    e_logits = e_logits.reshape(T, N_EXPERT_GROUPS, EXPERTS_PER_GROUP)
    e_in = jnp.take_along_axis(e_logits, g_idx[:, None, None], axis=1)[:, 0]
    top_v, top_i = lax.top_k(e_in, TOP_K_IN_GROUP)
    top_p = jax.nn.softmax(top_v, axis=-1)
    local = jnp.sum(jax.nn.one_hot(top_i, EXPERTS_PER_GROUP, dtype=jnp.float32) * top_p[..., None], axis=1)
    gates = (jax.nn.one_hot(g_idx, N_EXPERT_GROUPS, dtype=jnp.float32)[:, :, None] * local[:, None, :])
    gates = gates.reshape(T, N_EXPERTS) * g_w
    hid = jax.nn.silu(jnp.einsum('td,edf->tef', x, p['moe_w_gate'][l])) * jnp.einsum('td,edf->tef', x, p['moe_w_up'][l])
    out = jnp.einsum('tef,efd->td', hid * gates[:, :, None].astype(hid.dtype), p['moe_w_down'][l])
    return out.reshape(B, L, D)


def trunk(x, cond, p, ctx_k_all, ctx_v_all, rope):
    ks, vs = [], []
    for l in range(DEPTH):
        mod = (jax.nn.silu(cond) @ p['w_ada'][l] + p['b_ada'][l])[:, None, :]
        sh1, sc1, g1, sh2, sc2, g2 = jnp.split(mod, N_MOD, axis=-1)
        hmix = rms_norm(x, p['norm_mix'][l]) * (1 + sc1) + sh1
        ctx_k = None if ctx_k_all is None else ctx_k_all[:, l]
        ctx_v = None if ctx_v_all is None else ctx_v_all[:, l]
        mix, k, v = parallel_mixer(hmix, l, p, ctx_k, ctx_v, rope)
        x = x + g1 * mix
        hffn = rms_norm(x, p['norm_ffn'][l]) * (1 + sc2) + sh2
        x = x + g2 * hier_moe(hffn, l, p)
        ks.append(k)
        vs.append(v)
    return rms_norm(x, p['final_norm']), ks, vs


def setup_inputs(seed: int = 0) -> dict:
    key = jax.random.key(seed)
    ks = jax.random.split(key, 32)
    f32 = jnp.float32
    D = D_MODEL

    def nrm(k, shape, scale):
        return jax.random.normal(k, shape, f32) * scale

    kv_shape = (DEC_BATCH, DEPTH, PAST_LEN, N_KV_HEADS, HEAD_DIM)
    return {
        'x_prompt': nrm(ks[0], (BATCH, SEQ, D), 1.0),
        'x_sample': nrm(ks[1], (DEC_BATCH, DEC_SEQ, D), 1.0),
        'cache_k': nrm(ks[2], kv_shape, 1.0),
        'cache_v': nrm(ks[3], kv_shape, 1.0),
        'c': nrm(ks[4], (DEC_BATCH, D), 1.0),
        'c_ctx': nrm(ks[5], (D,), 1.0),
        'w_ada': nrm(ks[6], (DEPTH, D, N_MOD * D), 0.5 * D ** -0.5),
        'b_ada': nrm(ks[7], (DEPTH, N_MOD * D), 0.02),
        'norm_mix': 1.0 + nrm(ks[8], (DEPTH, D), 0.02),
        'norm_ffn': 1.0 + nrm(ks[9], (DEPTH, D), 0.02),
        'w_in': nrm(ks[10], (DEPTH, D, IN_COLS), D ** -0.5),
        'w_out': nrm(ks[11], (DEPTH, MIX_WIDTH, D), MIX_WIDTH ** -0.5),
        'q_norm': 1.0 + nrm(ks[12], (DEPTH, HEAD_DIM), 0.02),
        'k_norm': 1.0 + nrm(ks[13], (DEPTH, HEAD_DIM), 0.02),
        'conv_w': nrm(ks[14], (DEPTH, CONV_WIDTH, CONV_CH), CONV_WIDTH ** -0.5),
        'conv_b': nrm(ks[15], (DEPTH, CONV_CH), 0.02),
        'conv_ln_g': 1.0 + nrm(ks[16], (DEPTH, CONV_CH), 0.02),
        'conv_ln_b': nrm(ks[17], (DEPTH, CONV_CH), 0.02),
        'pool_w': nrm(ks[18], (DEPTH, POOL_GROUPS, POOL_GROUP_CH, POOL_GROUP_CH), POOL_GROUP_CH ** -0.5),
        'pool_scale': 1.0 + nrm(ks[19], (DEPTH, POOL_CH), 0.1),
        'router_group_w': nrm(ks[20], (DEPTH, D, N_EXPERT_GROUPS), D ** -0.5),
        'router_group_b': nrm(ks[21], (DEPTH, N_EXPERT_GROUPS), 0.01),
        'router_expert_w': nrm(ks[22], (DEPTH, D, N_EXPERTS), D ** -0.5),
        'router_expert_b': nrm(ks[23], (DEPTH, N_EXPERTS), 0.01),
        'moe_w_gate': nrm(ks[24], (DEPTH, N_EXPERTS, D, EXPERT_FF), D ** -0.5),
        'moe_w_up': nrm(ks[25], (DEPTH, N_EXPERTS, D, EXPERT_FF), D ** -0.5),
        'moe_w_down': nrm(ks[26], (DEPTH, N_EXPERTS, EXPERT_FF, D), EXPERT_FF ** -0.5),
        'final_norm': 1.0 + nrm(ks[27], (D,), 0.02),
    }


def reference(x_prompt, x_sample, cache_k, cache_v, c, c_ctx, w_ada, b_ada, norm_mix, norm_ffn,
              w_in, w_out, q_norm, k_norm, conv_w, conv_b, conv_ln_g, conv_ln_b, pool_w, pool_scale,
              router_group_w, router_group_b, router_expert_w, router_expert_b,
              moe_w_gate, moe_w_up, moe_w_down, final_norm):
    p = {
        'w_ada': w_ada, 'b_ada': b_ada, 'norm_mix': norm_mix, 'norm_ffn': norm_ffn,
        'w_in': w_in, 'w_out': w_out, 'q_norm': q_norm, 'k_norm': k_norm,
        'conv_w': conv_w, 'conv_b': conv_b, 'conv_ln_g': conv_ln_g, 'conv_ln_b': conv_ln_b,
        'pool_w': pool_w, 'pool_scale': pool_scale,
        'router_group_w': router_group_w, 'router_group_b': router_group_b,
        'router_expert_w': router_expert_w, 'router_expert_b': router_expert_b,
        'moe_w_gate': moe_w_gate, 'moe_w_up': moe_w_up, 'moe_w_down': moe_w_down,
        'final_norm': final_norm,
    }
    y_prompt, ctx_ks, ctx_vs = trunk(x_prompt, c_ctx[None, :], p, None, None, None)
    new_cache_k = jnp.stack(ctx_ks, axis=1)
    new_cache_v = jnp.stack(ctx_vs, axis=1)
    rope = grid_rope(x_sample.shape[1])
    y_sample, _, _ = trunk(x_sample, c, p, cache_k, cache_v, rope)
    return (y_prompt, y_sample, new_cache_k, new_cache_v)
```

```python
import functools

import numpy as np
import jax
import jax.numpy as jnp
from jax import lax
from jax.experimental import pallas as pl
from jax.experimental.pallas import tpu as pltpu

F32 = jnp.float32
BF16 = jnp.bfloat16

D = 1024
BATCH = 32
SEQ = 256
DEPTH = 2
DEC_BATCH = 4
DEC_SEQ = 2048
PAST_LEN = 256
GRID_W = 64
HEAD_DIM = 64
ATTN_CH = 512
KV_CH = 128
CONV_CH = 512
CONV_WIDTH = 31
POOL_WINDOWS = (2, 4, 8, 16)
POOL_GROUP_CH = 128
FOURIER_HEAD_CH = 128
FOURIER_HEADS = 4
N_EXPERT_GROUPS = 4
EXPERTS_PER_GROUP = 4
N_EXPERTS = 16
EXPERT_FF = 256
ROPE_THETA = 10000.0
EPS = 1e-6
N_MOD = 6

T_CTX = BATCH * SEQ
T_LAT = DEC_BATCH * DEC_SEQ
T_ALL = T_CTX + T_LAT
LANES = 128
PAD = 16
IN_COLS = 2816
COL_Q, COL_CA, COL_CG, COL_POOL, COL_FOUR, COL_K, COL_V = 0, 512, 1024, 1536, 2048, 2560, 2688

TM = 512
TQ = 256
LK = PAST_LEN + DEC_SEQ


def _cparams(sem, vmem_mb):
    return pltpu.CompilerParams(dimension_semantics=sem, vmem_limit_bytes=vmem_mb * 1024 * 1024)


def _mod_row(i, tm):
    nctx = T_CTX // tm
    per = DEC_SEQ // tm
    return jnp.where(i < nctx, 0, 1 + (i - nctx) // per)


def _mod_spec(k, tm):
    return pl.BlockSpec((None, None, 1, D), lambda i: (_mod_row(i, tm), k, 0, 0))


def _silu(x):
    return x * jax.nn.sigmoid(x)


def _ada_body(c_ref, w_ref, b_ref, o_ref):
    s = _silu(c_ref[...])
    o_ref[...] = jnp.dot(s.astype(BF16), w_ref[...].astype(BF16), preferred_element_type=F32) + b_ref[...]


def _ada(cond8, w_ada, b_ada):
    tn = 1536
    n = N_MOD * D
    return pl.pallas_call(
        _ada_body,
        grid=(DEPTH, n // tn),
        in_specs=[pl.BlockSpec((8, D), lambda l, j: (0, 0)),
                  pl.BlockSpec((None, D, tn), lambda l, j: (l, 0, j)),
                  pl.BlockSpec((None, 1, tn), lambda l, j: (l, 0, j))],
        out_specs=pl.BlockSpec((None, 8, tn), lambda l, j: (l, 0, j)),
        out_shape=jax.ShapeDtypeStruct((DEPTH, 8, n), F32),
        compiler_params=_cparams(("arbitrary", "arbitrary"), 40),
        name="ada_mod",
    )(cond8, w_ada, b_ada.reshape(DEPTH, 1, n))


def _inproj_body(x_ref, g_ref, sc_ref, sh_ref, w_ref, o_ref):
    x = x_ref[...]
    ms = jnp.mean(x * x, axis=-1, keepdims=True)
    h = x * lax.rsqrt(ms + EPS) * g_ref[...]
    h = h * (1.0 + sc_ref[...]) + sh_ref[...]
    o_ref[...] = jnp.dot(h.astype(BF16), w_ref[...], preferred_element_type=F32)


def _inproj(x, gain, mod4, w_in_bf):
    return pl.pallas_call(
        _inproj_body,
        grid=(T_ALL // TM,),
        in_specs=[pl.BlockSpec((TM, D), lambda i: (i, 0)),
                  pl.BlockSpec((1, D), lambda i: (0, 0)),
                  _mod_spec(1, TM), _mod_spec(0, TM),
                  pl.BlockSpec((D, IN_COLS), lambda i: (0, 0))],
        out_specs=pl.BlockSpec((TM, IN_COLS), lambda i: (i, 0)),
        out_shape=jax.ShapeDtypeStruct((T_ALL, IN_COLS), F32),
        compiler_params=_cparams(("parallel",), 56),
        name="in_proj",
    )(x, gain, mod4, mod4, w_in_bf)


def _lane_lo():
    return lax.broadcasted_iota(jnp.int32, (1, LANES), 1) < HEAD_DIM


def _head_norm(x, gain):
    lo = _lane_lo()
    x2 = x * x
    s_lo = jnp.sum(jnp.where(lo, x2, 0.0), axis=-1, keepdims=True)
    s_hi = jnp.sum(jnp.where(lo, 0.0, x2), axis=-1, keepdims=True)
    r = jnp.where(lo, lax.rsqrt(s_lo * (1.0 / HEAD_DIM) + EPS), lax.rsqrt(s_hi * (1.0 / HEAD_DIM) + EPS))
    return x * r * gain


def _rope(x, cos, sin_signed):
    even = (lax.broadcasted_iota(jnp.int32, (1, LANES), 1) % 2) == 0
    swapped = jnp.where(even, pltpu.roll(x, LANES - 1, axis=1), pltpu.roll(x, 1, axis=1))
    return x * cos + swapped * sin_signed


def _split_heads(x, g):
    lo = _lane_lo()
    own = jnp.where(lo if g == 0 else jnp.logical_not(lo), x, 0.0)
    other = pltpu.roll(own, HEAD_DIM, axis=1)
    return (own, other) if g == 0 else (other, own)


def _attend_tile(q_bf, k_cat, v_cat, lk):
    s = lax.dot_general(q_bf, k_cat, (((1,), (1,)), ((), ())), preferred_element_type=F32)
    s_lo, s_hi = s[:, :lk], s[:, lk:]
    p_lo = jnp.exp(s_lo - jnp.max(s_lo, axis=-1, keepdims=True))
    p_hi = jnp.exp(s_hi - jnp.max(s_hi, axis=-1, keepdims=True))
    l_lo = jnp.sum(p_lo, axis=-1, keepdims=True)
    l_hi = jnp.sum(p_hi, axis=-1, keepdims=True)
    p = jnp.concatenate([p_lo, p_hi], axis=1).astype(BF16)
    o = jnp.dot(p, v_cat, preferred_element_type=F32)
    return o * jnp.where(_lane_lo(), 1.0 / l_lo, 1.0 / l_hi)


def _ctx_attn_body(q_ref, k_ref, v_ref, qg_ref, kg_ref, kc_in, vc_in, o_ref, kc_ref, vc_ref):
    del kc_in, vc_in
    kn = _head_norm(k_ref[...], kg_ref[...])
    v = v_ref[...]
    kc_ref[...] = kn
    vc_ref[...] = v
    scale = HEAD_DIM ** -0.5
    for g in range(2):
        k_lo, k_hi = _split_heads(kn, g)
        v_lo, v_hi = _split_heads(v, g)
        k_cat = jnp.concatenate([k_lo, k_hi], axis=0).astype(BF16)
        v_cat = jnp.concatenate([v_lo, v_hi], axis=0).astype(BF16)
        for tt in range(2):
            t = 2 * g + tt
            q = _head_norm(q_ref[:, t * LANES:(t + 1) * LANES], qg_ref[...]) * scale
            o = _attend_tile(q.astype(BF16), k_cat, v_cat, SEQ)
            o_ref[:, t * LANES:(t + 1) * LANES] = o.astype(BF16)


def _ctx_attn(proj, qg, kg, kcache, vcache, layer):
    nq = ATTN_CH // LANES
    return pl.pallas_call(
        _ctx_attn_body,
        grid=(BATCH,),
        in_specs=[pl.BlockSpec((SEQ, ATTN_CH), lambda b: (b, COL_Q // ATTN_CH)),
                  pl.BlockSpec((SEQ, KV_CH), lambda b: (b, COL_K // KV_CH)),
                  pl.BlockSpec((SEQ, KV_CH), lambda b: (b, COL_V // KV_CH)),
                  pl.BlockSpec((1, LANES), lambda b: (0, 0)),
                  pl.BlockSpec((1, LANES), lambda b: (0, 0)),
                  pl.BlockSpec(memory_space=pl.ANY),
                  pl.BlockSpec(memory_space=pl.ANY)],
        out_specs=[pl.BlockSpec((SEQ, ATTN_CH), lambda b: (b, 0)),
                   pl.BlockSpec((None, None, SEQ, KV_CH), lambda b: (b, layer, 0, 0)),
                   pl.BlockSpec((None, None, SEQ, KV_CH), lambda b: (b, layer, 0, 0))],
        out_shape=[jax.ShapeDtypeStruct((T_ALL, ATTN_CH), BF16),
                   jax.ShapeDtypeStruct((BATCH, DEPTH, SEQ, KV_CH), F32),
                   jax.ShapeDtypeStruct((BATCH, DEPTH, SEQ, KV_CH), F32)],
        input_output_aliases={5: 1, 6: 2},
        compiler_params=_cparams(("parallel",), 32),
        name="ctx_attn",
    )(proj, proj, proj, qg, kg, kcache, vcache)


def _lat_attn_body(q_ref, k_ref, v_ref, ck_ref, cv_ref, qg_ref, kg_ref, cosk_ref, sink_ref, cosq_ref, sinq_ref,
                   prev_ref, o_ref, kcat_ref, vcat_ref):
    del prev_ref

    @pl.when(pl.program_id(1) == 0)
    def _():
        rows = 256
        for c in range(LK // rows):
            if c == 0:
                kn = ck_ref[...]
                v = cv_ref[...]
            else:
                r0 = (c - 1) * rows
                kn = _head_norm(k_ref[r0:r0 + rows, :], kg_ref[...])
                kn = _rope(kn, cosk_ref[r0:r0 + rows, :], sink_ref[r0:r0 + rows, :])
                v = v_ref[r0:r0 + rows, :]
            for g in range(2):
                k_lo, k_hi = _split_heads(kn, g)
                v_lo, v_hi = _split_heads(v, g)
                kcat_ref[g, c * rows:(c + 1) * rows, :] = k_lo.astype(BF16)
                kcat_ref[g, LK + c * rows:LK + (c + 1) * rows, :] = k_hi.astype(BF16)
                vcat_ref[g, c * rows:(c + 1) * rows, :] = v_lo.astype(BF16)
                vcat_ref[g, LK + c * rows:LK + (c + 1) * rows, :] = v_hi.astype(BF16)

    scale = HEAD_DIM ** -0.5
    for t in range(ATTN_CH // LANES):
        g = t // 2
        q = _head_norm(q_ref[:, t * LANES:(t + 1) * LANES], qg_ref[...])
        q = _rope(q, cosq_ref[...], sinq_ref[...]) * scale
        o = _attend_tile(q.astype(BF16), kcat_ref[g], vcat_ref[g], LK)
        o_ref[:, t * LANES:(t + 1) * LANES] = o.astype(BF16)


def _lat_attn(proj, cache_k4, cache_v4, qg, kg, cos, sin, prev, layer):
    nqb = DEC_SEQ // TQ
    row0 = T_CTX // DEC_SEQ
    return pl.pallas_call(
        _lat_attn_body,
        grid=(DEC_BATCH, nqb),
        in_specs=[pl.BlockSpec((TQ, ATTN_CH), lambda b, i: (T_CTX // TQ + b * nqb + i, COL_Q // ATTN_CH)),
                  pl.BlockSpec((DEC_SEQ, KV_CH), lambda b, i: (row0 + b, COL_K // KV_CH)),
                  pl.BlockSpec((DEC_SEQ, KV_CH), lambda b, i: (row0 + b, COL_V // KV_CH)),
                  pl.BlockSpec((None, None, PAST_LEN, KV_CH), lambda b, i: (b, layer, 0, 0)),
                  pl.BlockSpec((None, None, PAST_LEN, KV_CH), lambda b, i: (b, layer, 0, 0)),
                  pl.BlockSpec((1, LANES), lambda b, i: (0, 0)),
                  pl.BlockSpec((1, LANES), lambda b, i: (0, 0)),
                  pl.BlockSpec((DEC_SEQ, LANES), lambda b, i: (0, 0)),
                  pl.BlockSpec((DEC_SEQ, LANES), lambda b, i: (0, 0)),
                  pl.BlockSpec((TQ, LANES), lambda b, i: (i, 0)),
                  pl.BlockSpec((TQ, LANES), lambda b, i: (i, 0)),
                  pl.BlockSpec(memory_space=pl.ANY)],
        out_specs=pl.BlockSpec((TQ, ATTN_CH), lambda b, i: (T_CTX // TQ + b * nqb + i, 0)),
        out_shape=jax.ShapeDtypeStruct((T_ALL, ATTN_CH), BF16),
        input_output_aliases={11: 0},
        scratch_shapes=[pltpu.VMEM((2, 2 * LK, LANES), BF16), pltpu.VMEM((2, 2 * LK, LANES), BF16)],
        compiler_params=_cparams(("parallel", "arbitrary"), 56),
        name="lat_attn",
    )(proj, proj, proj, cache_k4, cache_v4, qg, kg, cos, sin, cos, sin, prev)


def _conv_body(L, a_ref, g_ref, w_ref, b_ref, lg_ref, lb_ref, o_ref, pad_ref):
    rows = 64
    zeros = jnp.zeros((PAD, CONV_CH), F32)
    pad_ref[0:PAD, :] = zeros
    pad_ref[PAD + L:2 * PAD + L, :] = zeros
    pad_ref[PAD:PAD + L, :] = a_ref[...] * jax.nn.sigmoid(g_ref[...])

    for c in range(L // rows):
        base = c * rows
        acc = jnp.zeros((rows, CONV_CH), F32)
        for j in range(CONV_WIDTH):
            off = base + j + PAD - CONV_WIDTH // 2
            acc = acc + w_ref[j:j + 1, :] * pad_ref[off:off + rows, :]
        y = acc + b_ref[...]
        mu = jnp.mean(y, axis=-1, keepdims=True)
        yc = y - mu
        var = jnp.mean(yc * yc, axis=-1, keepdims=True)
        yn = yc * lax.rsqrt(var + EPS) * lg_ref[...] + lb_ref[...]
        o_ref[base:base + rows, :] = _silu(yn).astype(BF16)


def _with_prev(body, n_in, prev):
    if prev is None:
        return body, [], [], {}

    def wrapped(*refs):
        return body(*refs[:n_in], *refs[n_in + 1:])

    return wrapped, [pl.BlockSpec(memory_space=pl.ANY)], [prev], {n_in: 0}


def _conv(proj, w32, b, lg, lb, L, nseq, row0, prev=None):
    vec = pl.BlockSpec((1, CONV_CH), lambda s: (0, 0))
    body, pspec, parg, alias = _with_prev(functools.partial(_conv_body, L), 6, prev)
    return pl.pallas_call(
        body,
        grid=(nseq,),
        in_specs=[pl.BlockSpec((L, CONV_CH), lambda s: (row0 + s, COL_CA // CONV_CH)),
                  pl.BlockSpec((L, CONV_CH), lambda s: (row0 + s, COL_CG // CONV_CH)),
                  pl.BlockSpec((32, CONV_CH), lambda s: (0, 0)), vec, vec, vec] + pspec,
        out_specs=pl.BlockSpec((L, CONV_CH), lambda s: (row0 + s, 0)),
        out_shape=jax.ShapeDtypeStruct((T_ALL, CONV_CH), BF16),
        input_output_aliases=alias,
        scratch_shapes=[pltpu.VMEM((L + 2 * PAD, CONV_CH), F32)],
        compiler_params=_cparams(("parallel",), 48),
        name="conv_L%d" % L,
    )(proj, proj, w32, b, lg, lb, *parg)


def _pool_body(L, u_ref, w_ref, sc_ref, o_ref, pad_ref):
    rows = 256
    zeros = jnp.zeros((PAD, 512), F32)
    pad_ref[0:PAD, :] = zeros
    pad_ref[PAD + L:2 * PAD + L, :] = zeros
    pad_ref[PAD:PAD + L, :] = u_ref[...]
    for c in range(L // rows):
        r0 = c * rows
        t = lax.broadcasted_iota(jnp.int32, (rows, 1), 0) + r0
        for g, w in enumerate(POOL_WINDOWS):
            cols = slice(g * POOL_GROUP_CH, (g + 1) * POOL_GROUP_CH)
            acc = jnp.zeros((rows, POOL_GROUP_CH), F32)
            for i in range(-(w // 2), w - w // 2):
                acc = acc + pad_ref[PAD + r0 + i:PAD + r0 + i + rows, cols]
            lo = jnp.maximum(t - w // 2, 0)
            hi = jnp.minimum(t + (w - w // 2), L)
            cnt = (hi - lo).astype(F32)
            pooled = acc / cnt - pad_ref[PAD + r0:PAD + r0 + rows, cols]
            mixed = jnp.dot(pooled.astype(BF16), w_ref[g], preferred_element_type=F32)
            o_ref[r0:r0 + rows, cols] = (mixed * sc_ref[:, cols]).astype(BF16)


def _pool(proj, pw_bf, pscale, L, nseq, row0, prev=None):
    body, pspec, parg, alias = _with_prev(functools.partial(_pool_body, L), 3, prev)
    return pl.pallas_call(
        body,
        grid=(nseq,),
        in_specs=[pl.BlockSpec((L, 512), lambda s: (row0 + s, COL_POOL // 512)),
                  pl.BlockSpec((4, POOL_GROUP_CH, POOL_GROUP_CH), lambda s: (0, 0, 0)),
                  pl.BlockSpec((1, 512), lambda s: (0, 0))] + pspec,
        out_specs=pl.BlockSpec((L, 512), lambda s: (row0 + s, 0)),
        out_shape=jax.ShapeDtypeStruct((T_ALL, 512), BF16),
        input_output_aliases=alias,
        scratch_shapes=[pltpu.VMEM((L + 2 * PAD, 512), F32)],
        compiler_params=_cparams(("parallel",), 48),
        name="pool_L%d" % L,
    )(proj, pw_bf, pscale, *parg)


def _channel_dft(u, wc_ref, z_ref, r0, L):
    rows = u.shape[0]
    for h in range(FOURIER_HEADS):
        cols = slice(h * FOURIER_HEAD_CH, (h + 1) * FOURIER_HEAD_CH)
        a = jnp.dot(u[:, cols].astype(BF16), wc_ref[...], preferred_element_type=F32)
        z_ref[r0:r0 + rows, cols] = a[:, :FOURIER_HEAD_CH].astype(BF16)
        z_ref[L + r0:L + r0 + rows, cols] = a[:, FOURIER_HEAD_CH:].astype(BF16)


def _four_body(L, tr, u_ref, wc_ref, wl_ref, o_ref, z_ref):
    @pl.when(pl.program_id(1) == 0)
    def _():
        rows = min(L, 512)
        for c in range(L // rows):
            _channel_dft(u_ref[c * rows:(c + 1) * rows, :], wc_ref, z_ref, c * rows, L)

    scale = (L * FOURIER_HEAD_CH) ** -0.5
    o_ref[...] = (jnp.dot(wl_ref[...], z_ref[...], preferred_element_type=F32) * scale).astype(BF16)


def _four(proj, wc_bf, wl_bf, L, nseq, row0, prev=None):
    tr = min(L, 256)
    body, pspec, parg, alias = _with_prev(functools.partial(_four_body, L, tr), 3, prev)
    return pl.pallas_call(
        body,
        grid=(nseq, L // tr),
        in_specs=[pl.BlockSpec((L, 512), lambda s, r: (row0 + s, COL_FOUR // 512)),
                  pl.BlockSpec((FOURIER_HEAD_CH, 2 * FOURIER_HEAD_CH), lambda s, r: (0, 0)),
                  pl.BlockSpec((tr, 2 * L), lambda s, r: (r, 0))] + pspec,
        out_specs=pl.BlockSpec((tr, 512), lambda s, r: ((row0 + s) * (L // tr) + r, 0)),
        out_shape=jax.ShapeDtypeStruct((T_ALL, 512), BF16),
        input_output_aliases=alias,
        scratch_shapes=[pltpu.VMEM((2 * L, 512), BF16)],
        compiler_params=_cparams(("parallel", "arbitrary"), 48),
        name="four_L%d" % L,
    )(proj, wc_bf, wl_bf, *parg)


def _route(logits):
    lane = lax.broadcasted_iota(jnp.int32, logits.shape, 1).astype(F32)
    neg = jnp.float32(-jnp.inf)
    big = jnp.float32(1 << 20)
    is_g = (lane >= N_EXPERTS) & (lane < N_EXPERTS + N_EXPERT_GROUPS)
    gl = jnp.where(is_g, logits, neg)
    gmax = jnp.max(gl, axis=-1, keepdims=True)
    g_idx = jnp.min(jnp.where(gl == gmax, lane, big), axis=-1, keepdims=True) - N_EXPERTS
    denom = jnp.sum(jnp.where(is_g, jnp.exp(gl - gmax), 0.0), axis=-1, keepdims=True)
    g_w = 1.0 / denom
    in_grp = (lane >= g_idx * EXPERTS_PER_GROUP) & (lane < (g_idx + 1) * EXPERTS_PER_GROUP)
    el = jnp.where(in_grp, logits, neg)
    v1 = jnp.max(el, axis=-1, keepdims=True)
    i1 = jnp.min(jnp.where(el == v1, lane, big), axis=-1, keepdims=True)
    el2 = jnp.where(lane == i1, neg, el)
    v2 = jnp.max(el2, axis=-1, keepdims=True)
    i2 = jnp.min(jnp.where(el2 == v2, lane, big), axis=-1, keepdims=True)
    e2 = jnp.exp(v2 - v1)
    p1 = 1.0 / (1.0 + e2)
    p2 = e2 / (1.0 + e2)
    return g_w * jnp.where(lane == i1, p1, jnp.where(lane == i2, p2, 0.0))


def _outproj_body(at_ref, cv_ref, po_ref, fo_ref, x_ref, w_ref, g1_ref, gain_ref, sc_ref, sh_ref, wr_ref, br_ref,
                  xo_ref, h_ref, gates_ref):
    mix = jnp.dot(at_ref[...], w_ref[0:512, :], preferred_element_type=F32)
    mix = mix + jnp.dot(cv_ref[...], w_ref[512:1024, :], preferred_element_type=F32)
    mix = mix + jnp.dot(po_ref[...], w_ref[1024:1536, :], preferred_element_type=F32)
    mix = mix + jnp.dot(fo_ref[...], w_ref[1536:2048, :], preferred_element_type=F32)
    x = x_ref[...] + g1_ref[...] * mix
    xo_ref[...] = x
    ms = jnp.mean(x * x, axis=-1, keepdims=True)
    h = x * lax.rsqrt(ms + EPS) * gain_ref[...]
    h = h * (1.0 + sc_ref[...]) + sh_ref[...]
    h_ref[...] = h.astype(BF16)
    logits = jnp.dot(h, wr_ref[...], preferred_element_type=F32, precision=lax.Precision.HIGHEST) + br_ref[...]
    gates_ref[...] = _route(logits)


def _outproj(attn, conv, pool, four, x, w_out_bf, mod4, gain, wr, br):
    tile = lambda w: pl.BlockSpec((TM, w), lambda i: (i, 0))
    const = lambda r, c: pl.BlockSpec((r, c), lambda i: (0, 0))
    return pl.pallas_call(
        _outproj_body,
        grid=(T_ALL // TM,),
        in_specs=[tile(512), tile(512), tile(512), tile(512), tile(D), const(2048, D),
                  _mod_spec(2, TM), const(1, D), _mod_spec(4, TM), _mod_spec(3, TM),
                  const(D, LANES), const(1, LANES)],
        out_specs=[tile(D), tile(D), tile(LANES)],
        out_shape=[jax.ShapeDtypeStruct((T_ALL, D), F32),
                   jax.ShapeDtypeStruct((T_ALL, D), BF16),
                   jax.ShapeDtypeStruct((T_ALL, LANES), F32)],
        compiler_params=_cparams(("parallel",), 48),
        name="out_proj",
    )(attn, conv, pool, four, x, w_out_bf, mod4, gain, mod4, mod4, wr, br)


def _moe_body(final, h_ref, gates_ref, wg_ref, wu_ref, wd_ref, x_ref, g2_ref, fg_ref, o_ref, acc_ref):
    e = pl.program_id(1)

    @pl.when(e == 0)
    def _():
        acc_ref[...] = jnp.zeros_like(acc_ref)

    h = h_ref[...]
    a = jnp.dot(h, wg_ref[...], preferred_element_type=F32)
    b = jnp.dot(h, wu_ref[...], preferred_element_type=F32)
    gates = gates_ref[...]
    lane = lax.broadcasted_iota(jnp.int32, gates.shape, 1)
    gate = jnp.sum(jnp.where(lane == e, gates, 0.0), axis=-1, keepdims=True)
    hid = _silu(a) * b * gate
    acc_ref[...] += jnp.dot(hid.astype(BF16), wd_ref[...], preferred_element_type=F32)

    @pl.when(e == N_EXPERTS - 1)
    def _():
        x = x_ref[...] + g2_ref[...] * acc_ref[...]
        if final:
            ms = jnp.mean(x * x, axis=-1, keepdims=True)
            x = x * lax.rsqrt(ms + EPS) * fg_ref[...]
        o_ref[...] = x


def _moe(h, gates, wg_bf, wu_bf, wd_bf, x, mod4, final_gain, final):
    tile = lambda w: pl.BlockSpec((TM, w), lambda i, e: (i, 0))
    return pl.pallas_call(
        functools.partial(_moe_body, final),
        grid=(T_ALL // TM, N_EXPERTS),
        in_specs=[tile(D), tile(LANES),
                  pl.BlockSpec((None, D, EXPERT_FF), lambda i, e: (e, 0, 0)),
                  pl.BlockSpec((None, D, EXPERT_FF), lambda i, e: (e, 0, 0)),
                  pl.BlockSpec((None, EXPERT_FF, D), lambda i, e: (e, 0, 0)),
                  tile(D),
                  pl.BlockSpec((None, None, 1, D), lambda i, e: (_mod_row(i, TM), 5, 0, 0)),
                  pl.BlockSpec((1, D), lambda i, e: (0, 0))],
        out_specs=tile(D),
        out_shape=jax.ShapeDtypeStruct((T_ALL, D), F32),
        scratch_shapes=[pltpu.VMEM((TM, D), F32)],
        compiler_params=_cparams(("parallel", "arbitrary"), 40),
        name="moe",
    )(h, gates, wg_bf, wu_bf, wd_bf, x, mod4, final_gain)


def _rope_tables():
    t = np.arange(DEC_SEQ)
    pos = np.stack([t // GRID_W, t % GRID_W], axis=1).astype(np.float64)
    inv = ROPE_THETA ** (-np.arange(16, dtype=np.float64) / 16.0)
    d = np.arange(LANES) % HEAD_DIM
    pair = d // 2
    ang = pos[:, (pair >= 16).astype(np.int64)] * inv[pair % 16][None, :]
    sign = np.where(d % 2 == 0, -1.0, 1.0)
    return np.cos(ang).astype(np.float32), (np.sin(ang) * sign[None, :]).astype(np.float32)


def _dft_pos(L):
    j = np.arange(L)
    ang = 2.0 * np.pi * ((j[:, None] * j[None, :]) % L) / L
    return np.concatenate([np.cos(ang), -np.sin(ang)], axis=1).astype(np.float32)


def _dft_ch():
    c = np.arange(FOURIER_HEAD_CH)
    ang = 2.0 * np.pi * ((c[:, None] * c[None, :]) % FOURIER_HEAD_CH) / FOURIER_HEAD_CH
    return np.concatenate([np.cos(ang), np.sin(ang)], axis=1).astype(np.float32)


def kernel(x_prompt, x_sample, cache_k, cache_v, c, c_ctx, w_ada, b_ada, norm_mix, norm_ffn, w_in, w_out, q_norm, k_norm, conv_w, conv_b, conv_ln_g, conv_ln_b, pool_w, pool_scale, router_group_w, router_group_b, router_expert_w, router_expert_b, moe_w_gate, moe_w_up, moe_w_down, final_norm):
    x = jnp.concatenate([x_prompt.reshape(T_CTX, D), x_sample.reshape(T_LAT, D)], axis=0)
    cond8 = jnp.concatenate([c_ctx[None, :], c, jnp.zeros((3, D), F32)], axis=0)
    mod = _ada(cond8, w_ada, b_ada)

    o1, o2, o3 = ATTN_CH, ATTN_CH + KV_CH, ATTN_CH + 2 * KV_CH
    w_in_p = jnp.concatenate([w_in[:, :, :o1], w_in[:, :, o3:], w_in[:, :, o1:o3]], axis=-1).astype(BF16)
    w_out_bf = w_out.astype(BF16)
    wg_bf, wu_bf, wd_bf = moe_w_gate.astype(BF16), moe_w_up.astype(BF16), moe_w_down.astype(BF16)
    pw_bf = pool_w.astype(BF16)
    w_router = jnp.concatenate([router_expert_w, router_group_w,
                                jnp.zeros((DEPTH, D, LANES - N_EXPERTS - N_EXPERT_GROUPS), F32)], axis=-1)
    b_router = jnp.concatenate([router_expert_b, router_group_b,
                                jnp.zeros((DEPTH, LANES - N_EXPERTS - N_EXPERT_GROUPS), F32)], axis=-1)
    conv_w32 = jnp.concatenate([conv_w, jnp.zeros((DEPTH, 1, CONV_CH), F32)], axis=1)

    cos_np, sin_np = _rope_tables()
    cos, sin = jnp.asarray(cos_np), jnp.asarray(sin_np)
    wc_bf = jnp.asarray(_dft_ch()).astype(BF16)
    wl_ctx = jnp.asarray(_dft_pos(SEQ)).astype(BF16)
    wl_lat = jnp.asarray(_dft_pos(DEC_SEQ)).astype(BF16)

    cache_k4 = cache_k.reshape(DEC_BATCH, DEPTH, PAST_LEN, KV_CH)
    cache_v4 = cache_v.reshape(DEC_BATCH, DEPTH, PAST_LEN, KV_CH)
    new_k = jnp.zeros((BATCH, DEPTH, SEQ, KV_CH), F32)
    new_v = jnp.zeros((BATCH, DEPTH, SEQ, KV_CH), F32)
    fgain = final_norm.reshape(1, D)

    for l in range(DEPTH):
        mod4 = mod[l].reshape(8, N_MOD, 1, D)
        proj = _inproj(x, norm_mix[l].reshape(1, D), mod4, w_in_p[l])
        qg = jnp.tile(q_norm[l], 2).reshape(1, LANES)
        kg = jnp.tile(k_norm[l], 2).reshape(1, LANES)
        attn, new_k, new_v = _ctx_attn(proj, qg, kg, new_k, new_v, l)
        attn = _lat_attn(proj, cache_k4, cache_v4, qg, kg, cos, sin, attn, l)
        cb, lg, lb = conv_b[l].reshape(1, -1), conv_ln_g[l].reshape(1, -1), conv_ln_b[l].reshape(1, -1)
        ps = pool_scale[l].reshape(1, -1)
        lat0 = T_CTX // DEC_SEQ
        conv = _conv(proj, conv_w32[l], cb, lg, lb, SEQ, BATCH, 0)
        conv = _conv(proj, conv_w32[l], cb, lg, lb, DEC_SEQ, DEC_BATCH, lat0, conv)
        pool = _pool(proj, pw_bf[l], ps, SEQ, BATCH, 0)
        pool = _pool(proj, pw_bf[l], ps, DEC_SEQ, DEC_BATCH, lat0, pool)
        four = _four(proj, wc_bf, wl_ctx, SEQ, BATCH, 0)
        four = _four(proj, wc_bf, wl_lat, DEC_SEQ, DEC_BATCH, lat0, four)
        x, h, gates = _outproj(attn, conv, pool, four, x, w_out_bf[l], mod4, norm_ffn[l].reshape(1, D),
                               w_router[l], b_router[l].reshape(1, LANES))
        x = _moe(h, gates, wg_bf[l], wu_bf[l], wd_bf[l], x, mod4, fgain, l == DEPTH - 1)

    y_prompt = x[:T_CTX].reshape(BATCH, SEQ, D)
    y_sample = x[T_CTX:].reshape(DEC_BATCH, DEC_SEQ, D)
    new_k = new_k.reshape(BATCH, DEPTH, SEQ, 2, HEAD_DIM)
    new_v = new_v.reshape(BATCH, DEPTH, SEQ, 2, HEAD_DIM)
    return (y_prompt, y_sample, new_k, new_v)
```

```python
import functools

import numpy as np
import jax
import jax.numpy as jnp
from jax import lax
from jax.experimental import pallas as pl
from jax.experimental.pallas import tpu as pltpu

F32 = jnp.float32
BF16 = jnp.bfloat16

D = 1024
BATCH = 32
SEQ = 256
DEPTH = 2
DEC_BATCH = 4
DEC_SEQ = 2048
PAST_LEN = 256
GRID_W = 64
HEAD_DIM = 64
ATTN_CH = 512
KV_CH = 128
CONV_CH = 512
CONV_WIDTH = 31
POOL_WINDOWS = (2, 4, 8, 16)
POOL_GROUP_CH = 128
FOURIER_HEAD_CH = 128
FOURIER_HEADS = 4
N_EXPERT_GROUPS = 4
EXPERTS_PER_GROUP = 4
N_EXPERTS = 16
EXPERT_FF = 256
ROPE_THETA = 10000.0
EPS = 1e-6
N_MOD = 6

T_CTX = BATCH * SEQ
T_LAT = DEC_BATCH * DEC_SEQ
T_ALL = T_CTX + T_LAT
LANES = 128
SUBLANES = 8
PAD = 16
CONV_ROWS = 128
ROUTER_LO_LANE = 32
IN_COLS = 2816
COL_Q, COL_CA, COL_CG, COL_POOL, COL_FOUR, COL_K, COL_V = 0, 512, 1024, 1536, 2048, 2560, 2688

TM = 512
TQ = 256
LK = PAST_LEN + DEC_SEQ


def _cparams(sem, vmem_mb):
    return pltpu.CompilerParams(dimension_semantics=sem, vmem_limit_bytes=vmem_mb * 1024 * 1024)


def _mod_row(i, tm):
    nctx = T_CTX // tm
    per = DEC_SEQ // tm
    return jnp.where(i < nctx, 0, 1 + (i - nctx) // per)


def _mod_spec(k, tm):
    return pl.BlockSpec((None, None, 1, D), lambda i: (_mod_row(i, tm), k, 0, 0))


def _silu(x):
    return x * jax.nn.sigmoid(x)


def _ada_body(c_ref, w_ref, b_ref, o_ref):
    s = _silu(c_ref[...])
    o_ref[...] = jnp.dot(s.astype(BF16), w_ref[...].astype(BF16), preferred_element_type=F32) + b_ref[...]


def _ada(cond8, w_ada, b_ada):
    tn = 1536
    n = N_MOD * D
    return pl.pallas_call(
        _ada_body,
        grid=(DEPTH, n // tn),
        in_specs=[pl.BlockSpec((8, D), lambda l, j: (0, 0)),
                  pl.BlockSpec((None, D, tn), lambda l, j: (l, 0, j)),
                  pl.BlockSpec((None, 1, tn), lambda l, j: (l, 0, j))],
        out_specs=pl.BlockSpec((None, 8, tn), lambda l, j: (l, 0, j)),
        out_shape=jax.ShapeDtypeStruct((DEPTH, 8, n), F32),
        compiler_params=_cparams(("arbitrary", "arbitrary"), 40),
        name="ada_mod",
    )(cond8, w_ada, b_ada.reshape(DEPTH, 1, n))


def _x_specs(nx):
    if nx == 1:
        return [pl.BlockSpec((TM, D), lambda i: (i, 0))]
    n = T_CTX // TM
    return [pl.BlockSpec((TM, D), lambda i: (jnp.minimum(i, n - 1), 0)),
            pl.BlockSpec((TM, D), lambda i: (jnp.maximum(i - n, 0), 0))]


def _load_x(x_refs):
    if len(x_refs) == 1:
        return x_refs[0][...]
    return jnp.where(pl.program_id(0) < T_CTX // TM, x_refs[0][...], x_refs[1][...])


def _inproj_body(nx, *refs):
    g_ref, sc_ref, sh_ref, w_ref, o_ref = refs[nx:]
    x = _load_x(refs[:nx])
    ms = jnp.mean(x * x, axis=-1, keepdims=True)
    h = x * lax.rsqrt(ms + EPS) * g_ref[...]
    h = h * (1.0 + sc_ref[...]) + sh_ref[...]
    o_ref[...] = jnp.dot(h.astype(BF16), w_ref[...], preferred_element_type=F32)


def _inproj(xs, gain, mod4, w_in_bf):
    return pl.pallas_call(
        functools.partial(_inproj_body, len(xs)),
        grid=(T_ALL // TM,),
        in_specs=_x_specs(len(xs)) + [pl.BlockSpec((1, D), lambda i: (0, 0)),
                                      _mod_spec(1, TM), _mod_spec(0, TM),
                                      pl.BlockSpec((D, IN_COLS), lambda i: (0, 0))],
        out_specs=pl.BlockSpec((TM, IN_COLS), lambda i: (i, 0)),
        out_shape=jax.ShapeDtypeStruct((T_ALL, IN_COLS), F32),
        compiler_params=_cparams(("parallel",), 56),
        name="in_proj",
    )(*xs, gain, mod4, mod4, w_in_bf)


def _lane_lo():
    return lax.broadcasted_iota(jnp.int32, (1, LANES), 1) < HEAD_DIM


def _head_norm(x, gain):
    lo = _lane_lo()
    x2 = x * x
    s_lo = jnp.sum(jnp.where(lo, x2, 0.0), axis=-1, keepdims=True)
    s_hi = jnp.sum(jnp.where(lo, 0.0, x2), axis=-1, keepdims=True)
    r = jnp.where(lo, lax.rsqrt(s_lo * (1.0 / HEAD_DIM) + EPS), lax.rsqrt(s_hi * (1.0 / HEAD_DIM) + EPS))
    return x * r * gain


def _rope(x, cos, sin_signed):
    even = (lax.broadcasted_iota(jnp.int32, (1, LANES), 1) % 2) == 0
    swapped = jnp.where(even, pltpu.roll(x, LANES - 1, axis=1), pltpu.roll(x, 1, axis=1))
    return x * cos + swapped * sin_signed


def _split_heads(x, g):
    lo = _lane_lo()
    own = jnp.where(lo if g == 0 else jnp.logical_not(lo), x, 0.0)
    other = pltpu.roll(own, HEAD_DIM, axis=1)
    return (own, other) if g == 0 else (other, own)


def _attend_tile(q_bf, k_cat, v_cat, lk):
    s = lax.dot_general(q_bf, k_cat, (((1,), (1,)), ((), ())), preferred_element_type=F32)
    s_lo, s_hi = s[:, :lk], s[:, lk:]
    p_lo = jnp.exp(s_lo - jnp.max(s_lo, axis=-1, keepdims=True))
    p_hi = jnp.exp(s_hi - jnp.max(s_hi, axis=-1, keepdims=True))
    l_lo = jnp.sum(p_lo, axis=-1, keepdims=True)
    l_hi = jnp.sum(p_hi, axis=-1, keepdims=True)
    p = jnp.concatenate([p_lo, p_hi], axis=1).astype(BF16)
    o = jnp.dot(p, v_cat, preferred_element_type=F32)
    return o * jnp.where(_lane_lo(), 1.0 / l_lo, 1.0 / l_hi)


def _ctx_attn_body(q_ref, k_ref, v_ref, qg_ref, kg_ref, kc_in, vc_in, o_ref, kc_ref, vc_ref):
    del kc_in, vc_in
    kn = _head_norm(k_ref[...], kg_ref[...])
    v = v_ref[...]
    kc_ref[...] = kn
    vc_ref[...] = v
    scale = HEAD_DIM ** -0.5
    for g in range(2):
        k_lo, k_hi = _split_heads(kn, g)
        v_lo, v_hi = _split_heads(v, g)
        k_cat = jnp.concatenate([k_lo, k_hi], axis=0).astype(BF16)
        v_cat = jnp.concatenate([v_lo, v_hi], axis=0).astype(BF16)
        for tt in range(2):
            t = 2 * g + tt
            q = _head_norm(q_ref[:, t * LANES:(t + 1) * LANES], qg_ref[...]) * scale
            o = _attend_tile(q.astype(BF16), k_cat, v_cat, SEQ)
            o_ref[:, t * LANES:(t + 1) * LANES] = o.astype(BF16)


def _ctx_attn(proj, qg, kg, kcache, vcache, layer):
    nq = ATTN_CH // LANES
    return pl.pallas_call(
        _ctx_attn_body,
        grid=(BATCH,),
        in_specs=[pl.BlockSpec((SEQ, ATTN_CH), lambda b: (b, COL_Q // ATTN_CH)),
                  pl.BlockSpec((SEQ, KV_CH), lambda b: (b, COL_K // KV_CH)),
                  pl.BlockSpec((SEQ, KV_CH), lambda b: (b, COL_V // KV_CH)),
                  pl.BlockSpec((1, LANES), lambda b: (0, 0)),
                  pl.BlockSpec((1, LANES), lambda b: (0, 0)),
                  pl.BlockSpec(memory_space=pl.ANY),
                  pl.BlockSpec(memory_space=pl.ANY)],
        out_specs=[pl.BlockSpec((SEQ, ATTN_CH), lambda b: (b, 0)),
                   pl.BlockSpec((None, None, SEQ, KV_CH), lambda b: (b, layer, 0, 0)),
                   pl.BlockSpec((None, None, SEQ, KV_CH), lambda b: (b, layer, 0, 0))],
        out_shape=[jax.ShapeDtypeStruct((T_ALL, ATTN_CH), BF16),
                   jax.ShapeDtypeStruct((BATCH, DEPTH, SEQ, KV_CH), F32),
                   jax.ShapeDtypeStruct((BATCH, DEPTH, SEQ, KV_CH), F32)],
        input_output_aliases={5: 1, 6: 2},
        compiler_params=_cparams(("parallel",), 32),
        name="ctx_attn",
    )(proj, proj, proj, qg, kg, kcache, vcache)


def _lat_attn_body(q_ref, k_ref, v_ref, ck_ref, cv_ref, qg_ref, kg_ref, cosk_ref, sink_ref, cosq_ref, sinq_ref,
                   prev_ref, o_ref, kcat_ref, vcat_ref):
    del prev_ref

    @pl.when(pl.program_id(1) == 0)
    def _():
        rows = 256
        for c in range(LK // rows):
            if c == 0:
                kn = ck_ref[...]
                v = cv_ref[...]
            else:
                r0 = (c - 1) * rows
                kn = _head_norm(k_ref[r0:r0 + rows, :], kg_ref[...])
                kn = _rope(kn, cosk_ref[r0:r0 + rows, :], sink_ref[r0:r0 + rows, :])
                v = v_ref[r0:r0 + rows, :]
            for g in range(2):
                k_lo, k_hi = _split_heads(kn, g)
                v_lo, v_hi = _split_heads(v, g)
                kcat_ref[g, c * rows:(c + 1) * rows, :] = k_lo.astype(BF16)
                kcat_ref[g, LK + c * rows:LK + (c + 1) * rows, :] = k_hi.astype(BF16)
                vcat_ref[g, c * rows:(c + 1) * rows, :] = v_lo.astype(BF16)
                vcat_ref[g, LK + c * rows:LK + (c + 1) * rows, :] = v_hi.astype(BF16)

    scale = HEAD_DIM ** -0.5
    for t in range(ATTN_CH // LANES):
        g = t // 2
        q = _head_norm(q_ref[:, t * LANES:(t + 1) * LANES], qg_ref[...])
        q = _rope(q, cosq_ref[...], sinq_ref[...]) * scale
        o = _attend_tile(q.astype(BF16), kcat_ref[g], vcat_ref[g], LK)
        o_ref[:, t * LANES:(t + 1) * LANES] = o.astype(BF16)


def _lat_attn(proj, cache_k4, cache_v4, qg, kg, cos, sin, prev, layer):
    nqb = DEC_SEQ // TQ
    row0 = T_CTX // DEC_SEQ
    return pl.pallas_call(
        _lat_attn_body,
        grid=(DEC_BATCH, nqb),
        in_specs=[pl.BlockSpec((TQ, ATTN_CH), lambda b, i: (T_CTX // TQ + b * nqb + i, COL_Q // ATTN_CH)),
                  pl.BlockSpec((DEC_SEQ, KV_CH), lambda b, i: (row0 + b, COL_K // KV_CH)),
                  pl.BlockSpec((DEC_SEQ, KV_CH), lambda b, i: (row0 + b, COL_V // KV_CH)),
                  pl.BlockSpec((None, None, PAST_LEN, KV_CH), lambda b, i: (b, layer, 0, 0)),
                  pl.BlockSpec((None, None, PAST_LEN, KV_CH), lambda b, i: (b, layer, 0, 0)),
                  pl.BlockSpec((1, LANES), lambda b, i: (0, 0)),
                  pl.BlockSpec((1, LANES), lambda b, i: (0, 0)),
                  pl.BlockSpec((DEC_SEQ, LANES), lambda b, i: (0, 0)),
                  pl.BlockSpec((DEC_SEQ, LANES), lambda b, i: (0, 0)),
                  pl.BlockSpec((TQ, LANES), lambda b, i: (i, 0)),
                  pl.BlockSpec((TQ, LANES), lambda b, i: (i, 0)),
                  pl.BlockSpec(memory_space=pl.ANY)],
        out_specs=pl.BlockSpec((TQ, ATTN_CH), lambda b, i: (T_CTX // TQ + b * nqb + i, 0)),
        out_shape=jax.ShapeDtypeStruct((T_ALL, ATTN_CH), BF16),
        input_output_aliases={11: 0},
        scratch_shapes=[pltpu.VMEM((2, 2 * LK, LANES), BF16), pltpu.VMEM((2, 2 * LK, LANES), BF16)],
        compiler_params=_cparams(("parallel", "arbitrary"), 56),
        name="lat_attn",
    )(proj, proj, proj, cache_k4, cache_v4, qg, kg, cos, sin, cos, sin, prev)


def _conv_body(L, a_ref, g_ref, w_ref, b_ref, lg_ref, lb_ref, o_ref, pad_ref, y_ref):
    rows = CONV_ROWS
    zeros = jnp.zeros((PAD, CONV_CH), F32)
    pad_ref[0:PAD, :] = zeros
    pad_ref[PAD + L:2 * PAD + L, :] = zeros
    pad_ref[PAD:PAD + L, :] = a_ref[...] * jax.nn.sigmoid(g_ref[...])
    shift0 = PAD - CONV_WIDTH // 2

    def chunk(c, carry):
        base = pl.multiple_of(c * rows, rows)
        for lg in range(CONV_CH // LANES):
            lanes = slice(lg * LANES, (lg + 1) * LANES)
            y = jnp.zeros((rows, LANES), F32)
            for r in range(SUBLANES):
                z = None
                for q in range((CONV_WIDTH + shift0) // SUBLANES + 1):
                    j = SUBLANES * q + r - shift0
                    if 0 <= j < CONV_WIDTH:
                        term = w_ref[j:j + 1, lanes] * pad_ref[pl.ds(base + SUBLANES * q, rows + SUBLANES), lanes]
                        z = term if z is None else z + term
                y = y + z[r:r + rows, :]
            y_ref[:, lanes] = y + b_ref[:, lanes]
        half = rows // 2
        for p in range(2):
            y = y_ref[p * half:(p + 1) * half, :]
            mu = jnp.mean(y, axis=-1, keepdims=True)
            yc = y - mu
            var = jnp.mean(yc * yc, axis=-1, keepdims=True)
            yn = yc * lax.rsqrt(var + EPS) * lg_ref[...] + lb_ref[...]
            o_ref[pl.ds(base + p * half, half), :] = _silu(yn).astype(BF16)
        return carry

    lax.fori_loop(0, L // rows, chunk, 0)


def _with_prev(body, n_in, prev):
    if prev is None:
        return body, [], [], {}

    def wrapped(*refs):
        return body(*refs[:n_in], *refs[n_in + 1:])

    return wrapped, [pl.BlockSpec(memory_space=pl.ANY)], [prev], {n_in: 0}


def _conv(proj, w32, b, lg, lb, L, nseq, row0, prev=None):
    vec = pl.BlockSpec((1, CONV_CH), lambda s: (0, 0))
    body, pspec, parg, alias = _with_prev(functools.partial(_conv_body, L), 6, prev)
    return pl.pallas_call(
        body,
        grid=(nseq,),
        in_specs=[pl.BlockSpec((L, CONV_CH), lambda s: (row0 + s, COL_CA // CONV_CH)),
                  pl.BlockSpec((L, CONV_CH), lambda s: (row0 + s, COL_CG // CONV_CH)),
                  pl.BlockSpec((32, CONV_CH), lambda s: (0, 0)), vec, vec, vec] + pspec,
        out_specs=pl.BlockSpec((L, CONV_CH), lambda s: (row0 + s, 0)),
        out_shape=jax.ShapeDtypeStruct((T_ALL, CONV_CH), BF16),
        input_output_aliases=alias,
        scratch_shapes=[pltpu.VMEM((L + 2 * PAD, CONV_CH), F32), pltpu.VMEM((CONV_ROWS, CONV_CH), F32)],
        compiler_params=_cparams(("parallel",), 48),
        name="conv_L%d" % L,
    )(proj, proj, w32, b, lg, lb, *parg)


def _pool_body(L, u_ref, w_ref, sc_ref, o_ref, pad_ref):
    rows = 256
    zeros = jnp.zeros((PAD, 512), F32)
    pad_ref[0:PAD, :] = zeros
    pad_ref[PAD + L:2 * PAD + L, :] = zeros
    pad_ref[PAD:PAD + L, :] = u_ref[...]
    for c in range(L // rows):
        r0 = c * rows
        t = lax.broadcasted_iota(jnp.int32, (rows, 1), 0) + r0
        for g, w in enumerate(POOL_WINDOWS):
            cols = slice(g * POOL_GROUP_CH, (g + 1) * POOL_GROUP_CH)
            acc = jnp.zeros((rows, POOL_GROUP_CH), F32)
            for i in range(-(w // 2), w - w // 2):
                acc = acc + pad_ref[PAD + r0 + i:PAD + r0 + i + rows, cols]
            lo = jnp.maximum(t - w // 2, 0)
            hi = jnp.minimum(t + (w - w // 2), L)
            cnt = (hi - lo).astype(F32)
            pooled = acc / cnt - pad_ref[PAD + r0:PAD + r0 + rows, cols]
            mixed = jnp.dot(pooled.astype(BF16), w_ref[g], preferred_element_type=F32)
            o_ref[r0:r0 + rows, cols] = (mixed * sc_ref[:, cols]).astype(BF16)


def _pool(proj, pw_bf, pscale, L, nseq, row0, prev=None):
    body, pspec, parg, alias = _with_prev(functools.partial(_pool_body, L), 3, prev)
    return pl.pallas_call(
        body,
        grid=(nseq,),
        in_specs=[pl.BlockSpec((L, 512), lambda s: (row0 + s, COL_POOL // 512)),
                  pl.BlockSpec((4, POOL_GROUP_CH, POOL_GROUP_CH), lambda s: (0, 0, 0)),
                  pl.BlockSpec((1, 512), lambda s: (0, 0))] + pspec,
        out_specs=pl.BlockSpec((L, 512), lambda s: (row0 + s, 0)),
        out_shape=jax.ShapeDtypeStruct((T_ALL, 512), BF16),
        input_output_aliases=alias,
        scratch_shapes=[pltpu.VMEM((L + 2 * PAD, 512), F32)],
        compiler_params=_cparams(("parallel",), 48),
        name="pool_L%d" % L,
    )(proj, pw_bf, pscale, *parg)


def _channel_dft(u, wc_ref, z_ref, r0, L):
    rows = u.shape[0]
    for h in range(FOURIER_HEADS):
        cols = slice(h * FOURIER_HEAD_CH, (h + 1) * FOURIER_HEAD_CH)
        a = jnp.dot(u[:, cols].astype(BF16), wc_ref[...], preferred_element_type=F32)
        z_ref[r0:r0 + rows, cols] = a[:, :FOURIER_HEAD_CH].astype(BF16)
        z_ref[L + r0:L + r0 + rows, cols] = a[:, FOURIER_HEAD_CH:].astype(BF16)


def _four_body(L, tr, u_ref, wc_ref, wl_ref, o_ref, z_ref):
    @pl.when(pl.program_id(1) == 0)
    def _():
        rows = min(L, 512)
        for c in range(L // rows):
            _channel_dft(u_ref[c * rows:(c + 1) * rows, :], wc_ref, z_ref, c * rows, L)

    scale = (L * FOURIER_HEAD_CH) ** -0.5
    o_ref[...] = (jnp.dot(wl_ref[...], z_ref[...], preferred_element_type=F32) * scale).astype(BF16)


def _four(proj, wc_bf, wl_bf, L, nseq, row0, prev=None):
    tr = min(L, 256)
    body, pspec, parg, alias = _with_prev(functools.partial(_four_body, L, tr), 3, prev)
    return pl.pallas_call(
        body,
        grid=(nseq, L // tr),
        in_specs=[pl.BlockSpec((L, 512), lambda s, r: (row0 + s, COL_FOUR // 512)),
                  pl.BlockSpec((FOURIER_HEAD_CH, 2 * FOURIER_HEAD_CH), lambda s, r: (0, 0)),
                  pl.BlockSpec((tr, 2 * L), lambda s, r: (r, 0))] + pspec,
        out_specs=pl.BlockSpec((tr, 512), lambda s, r: ((row0 + s) * (L // tr) + r, 0)),
        out_shape=jax.ShapeDtypeStruct((T_ALL, 512), BF16),
        input_output_aliases=alias,
        scratch_shapes=[pltpu.VMEM((2 * L, 512), BF16)],
        compiler_params=_cparams(("parallel", "arbitrary"), 48),
        name="four_L%d" % L,
    )(proj, wc_bf, wl_bf, *parg)


def _route(logits):
    lane = lax.broadcasted_iota(jnp.int32, logits.shape, 1).astype(F32)
    neg = jnp.float32(-jnp.inf)
    big = jnp.float32(1 << 20)
    is_g = (lane >= N_EXPERTS) & (lane < N_EXPERTS + N_EXPERT_GROUPS)
    gl = jnp.where(is_g, logits, neg)
    gmax = jnp.max(gl, axis=-1, keepdims=True)
    g_idx = jnp.min(jnp.where(gl == gmax, lane, big), axis=-1, keepdims=True) - N_EXPERTS
    denom = jnp.sum(jnp.where(is_g, jnp.exp(gl - gmax), 0.0), axis=-1, keepdims=True)
    g_w = 1.0 / denom
    in_grp = (lane >= g_idx * EXPERTS_PER_GROUP) & (lane < (g_idx + 1) * EXPERTS_PER_GROUP)
    el = jnp.where(in_grp, logits, neg)
    v1 = jnp.max(el, axis=-1, keepdims=True)
    i1 = jnp.min(jnp.where(el == v1, lane, big), axis=-1, keepdims=True)
    el2 = jnp.where(lane == i1, neg, el)
    v2 = jnp.max(el2, axis=-1, keepdims=True)
    i2 = jnp.min(jnp.where(el2 == v2, lane, big), axis=-1, keepdims=True)
    e2 = jnp.exp(v2 - v1)
    p1 = 1.0 / (1.0 + e2)
    p2 = e2 / (1.0 + e2)
    return g_w * jnp.where(lane == i1, p1, jnp.where(lane == i2, p2, 0.0))


def _outproj_body(nx, *refs):
    (at_ref, cv_ref, po_ref, fo_ref, w_ref, g1_ref, gain_ref, sc_ref, sh_ref, wr_ref, br_ref,
     xo_ref, h_ref, gates_ref) = refs[nx:]
    mix = jnp.dot(at_ref[...], w_ref[0:512, :], preferred_element_type=F32)
    mix = mix + jnp.dot(cv_ref[...], w_ref[512:1024, :], preferred_element_type=F32)
    mix = mix + jnp.dot(po_ref[...], w_ref[1024:1536, :], preferred_element_type=F32)
    mix = mix + jnp.dot(fo_ref[...], w_ref[1536:2048, :], preferred_element_type=F32)
    x = _load_x(refs[:nx]) + g1_ref[...] * mix
    xo_ref[...] = x
    ms = jnp.mean(x * x, axis=-1, keepdims=True)
    h = x * lax.rsqrt(ms + EPS) * gain_ref[...]
    h = h * (1.0 + sc_ref[...]) + sh_ref[...]
    h_hi = h.astype(BF16)
    h_lo = (h - h_hi.astype(F32)).astype(BF16)
    h_ref[...] = h_hi
    t = (jnp.dot(h_hi, wr_ref[...], preferred_element_type=F32)
         + jnp.dot(h_lo, wr_ref[...], preferred_element_type=F32))
    logits = t + pltpu.roll(t, LANES - ROUTER_LO_LANE, axis=1) + br_ref[...]
    gates_ref[...] = _route(logits)


def _outproj(attn, conv, pool, four, xs, w_out_bf, mod4, gain, wr, br):
    tile = lambda w: pl.BlockSpec((TM, w), lambda i: (i, 0))
    const = lambda r, c: pl.BlockSpec((r, c), lambda i: (0, 0))
    return pl.pallas_call(
        functools.partial(_outproj_body, len(xs)),
        grid=(T_ALL // TM,),
        in_specs=_x_specs(len(xs)) + [tile(512), tile(512), tile(512), tile(512), const(2048, D),
                                      _mod_spec(2, TM), const(1, D), _mod_spec(4, TM), _mod_spec(3, TM),
                                      const(D, LANES), const(1, LANES)],
        out_specs=[tile(D), tile(D), tile(LANES)],
        out_shape=[jax.ShapeDtypeStruct((T_ALL, D), F32),
                   jax.ShapeDtypeStruct((T_ALL, D), BF16),
                   jax.ShapeDtypeStruct((T_ALL, LANES), F32)],
        compiler_params=_cparams(("parallel",), 48),
        name="out_proj",
    )(*xs, attn, conv, pool, four, w_out_bf, mod4, gain, mod4, mod4, wr, br)


def _moe_body(final, h_ref, gates_ref, wg_ref, wu_ref, wd_ref, x_ref, g2_ref, fg_ref, *out_and_scratch):
    acc_ref = out_and_scratch[-1]
    e = pl.program_id(1)

    @pl.when(e == 0)
    def _():
        acc_ref[...] = jnp.zeros_like(acc_ref)

    h = h_ref[...]
    a = jnp.dot(h, wg_ref[...], preferred_element_type=F32)
    b = jnp.dot(h, wu_ref[...], preferred_element_type=F32)
    gates = gates_ref[...]
    lane = lax.broadcasted_iota(jnp.int32, gates.shape, 1)
    gate = jnp.sum(jnp.where(lane == e, gates, 0.0), axis=-1, keepdims=True)
    hid = _silu(a) * b * gate
    acc_ref[...] += jnp.dot(hid.astype(BF16), wd_ref[...], preferred_element_type=F32)

    @pl.when(e == N_EXPERTS - 1)
    def _():
        x = x_ref[...] + g2_ref[...] * acc_ref[...]
        if not final:
            out_and_scratch[0][...] = x
        else:
            ms = jnp.mean(x * x, axis=-1, keepdims=True)
            y = x * lax.rsqrt(ms + EPS) * fg_ref[...]
            is_ctx = pl.program_id(0) < T_CTX // TM

            @pl.when(is_ctx)
            def _():
                out_and_scratch[0][...] = y

            @pl.when(jnp.logical_not(is_ctx))
            def _():
                out_and_scratch[1][...] = y


def _moe(h, gates, wg_bf, wu_bf, wd_bf, x, mod4, final_gain, final):
    tile = lambda w: pl.BlockSpec((TM, w), lambda i, e: (i, 0))
    if final:
        n = T_CTX // TM
        out_specs = [pl.BlockSpec((TM, D), lambda i, e: (jnp.minimum(i, n - 1), 0)),
                     pl.BlockSpec((TM, D), lambda i, e: (jnp.maximum(i - n, 0), 0))]
        out_shape = [jax.ShapeDtypeStruct((T_CTX, D), F32), jax.ShapeDtypeStruct((T_LAT, D), F32)]
    else:
        out_specs = tile(D)
        out_shape = jax.ShapeDtypeStruct((T_ALL, D), F32)
    return pl.pallas_call(
        functools.partial(_moe_body, final),
        grid=(T_ALL // TM, N_EXPERTS),
        in_specs=[tile(D), tile(LANES),
                  pl.BlockSpec((None, D, EXPERT_FF), lambda i, e: (e, 0, 0)),
                  pl.BlockSpec((None, D, EXPERT_FF), lambda i, e: (e, 0, 0)),
                  pl.BlockSpec((None, EXPERT_FF, D), lambda i, e: (e, 0, 0)),
                  tile(D),
                  pl.BlockSpec((None, None, 1, D), lambda i, e: (_mod_row(i, TM), 5, 0, 0)),
                  pl.BlockSpec((1, D), lambda i, e: (0, 0))],
        out_specs=out_specs,
        out_shape=out_shape,
        scratch_shapes=[pltpu.VMEM((TM, D), F32)],
        compiler_params=_cparams(("arbitrary", "arbitrary"), 40),
        name="moe",
    )(h, gates, wg_bf, wu_bf, wd_bf, x, mod4, final_gain)


def _rope_tables():
    t = np.arange(DEC_SEQ)
    pos = np.stack([t // GRID_W, t % GRID_W], axis=1).astype(np.float64)
    inv = ROPE_THETA ** (-np.arange(16, dtype=np.float64) / 16.0)
    d = np.arange(LANES) % HEAD_DIM
    pair = d // 2
    ang = pos[:, (pair >= 16).astype(np.int64)] * inv[pair % 16][None, :]
    sign = np.where(d % 2 == 0, -1.0, 1.0)
    return np.cos(ang).astype(np.float32), (np.sin(ang) * sign[None, :]).astype(np.float32)


def _dft_pos(L):
    j = np.arange(L)
    ang = 2.0 * np.pi * ((j[:, None] * j[None, :]) % L) / L
    return np.concatenate([np.cos(ang), -np.sin(ang)], axis=1).astype(np.float32)


def _dft_ch():
    c = np.arange(FOURIER_HEAD_CH)
    ang = 2.0 * np.pi * ((c[:, None] * c[None, :]) % FOURIER_HEAD_CH) / FOURIER_HEAD_CH
    return np.concatenate([np.cos(ang), np.sin(ang)], axis=1).astype(np.float32)


def kernel(x_prompt, x_sample, cache_k, cache_v, c, c_ctx, w_ada, b_ada, norm_mix, norm_ffn, w_in, w_out, q_norm, k_norm, conv_w, conv_b, conv_ln_g, conv_ln_b, pool_w, pool_scale, router_group_w, router_group_b, router_expert_w, router_expert_b, moe_w_gate, moe_w_up, moe_w_down, final_norm):
    xs = (x_prompt.reshape(T_CTX, D), x_sample.reshape(T_LAT, D))
    cond8 =jnp.concatenate([c_ctx[None, :], c, jnp.zeros((3, D), F32)], axis=0)
    mod = _ada(cond8, w_ada, b_ada)

    o1, o2, o3 = ATTN_CH, ATTN_CH + KV_CH, ATTN_CH + 2 * KV_CH
    w_in_p = jnp.concatenate([w_in[:, :, :o1], w_in[:, :, o3:], w_in[:, :, o1:o3]], axis=-1).astype(BF16)
    w_out_bf = w_out.astype(BF16)
    wg_bf, wu_bf, wd_bf = moe_w_gate.astype(BF16), moe_w_up.astype(BF16), moe_w_down.astype(BF16)
    pw_bf = pool_w.astype(BF16)
    n_route = N_EXPERTS + N_EXPERT_GROUPS
    wr = jnp.concatenate([router_expert_w, router_group_w], axis=-1)
    wr_hi = wr.astype(BF16)
    wr_lo = (wr - wr_hi.astype(F32)).astype(BF16)
    w_router = jnp.concatenate([wr_hi, jnp.zeros((DEPTH, D, ROUTER_LO_LANE - n_route), BF16), wr_lo,
                                jnp.zeros((DEPTH, D, LANES - ROUTER_LO_LANE - n_route), BF16)], axis=-1)
    b_router = jnp.concatenate([router_expert_b, router_group_b,
                                jnp.zeros((DEPTH, LANES - N_EXPERTS - N_EXPERT_GROUPS), F32)], axis=-1)
    conv_w32 = jnp.concatenate([conv_w, jnp.zeros((DEPTH, 1, CONV_CH), F32)], axis=1)

    cos_np, sin_np = _rope_tables()
    cos, sin = jnp.asarray(cos_np), jnp.asarray(sin_np)
    wc_bf = jnp.asarray(_dft_ch()).astype(BF16)
    wl_ctx = jnp.asarray(_dft_pos(SEQ)).astype(BF16)
    wl_lat = jnp.asarray(_dft_pos(DEC_SEQ)).astype(BF16)

    cache_k4 = cache_k.reshape(DEC_BATCH, DEPTH, PAST_LEN, KV_CH)
    cache_v4 = cache_v.reshape(DEC_BATCH, DEPTH, PAST_LEN, KV_CH)
    new_k = jnp.zeros((BATCH, DEPTH, SEQ, KV_CH), F32)
    new_v = jnp.zeros((BATCH, DEPTH, SEQ, KV_CH), F32)
    fgain = final_norm.reshape(1, D)

    for l in range(DEPTH):
        mod4 = mod[l].reshape(8, N_MOD, 1, D)
        proj = _inproj(xs, norm_mix[l].reshape(1, D), mod4, w_in_p[l])
        qg = jnp.tile(q_norm[l], 2).reshape(1, LANES)
        kg = jnp.tile(k_norm[l], 2).reshape(1, LANES)
        attn, new_k, new_v = _ctx_attn(proj, qg, kg, new_k, new_v, l)
        attn = _lat_attn(proj, cache_k4, cache_v4, qg, kg, cos, sin, attn, l)
        cb, lg, lb = conv_b[l].reshape(1, -1), conv_ln_g[l].reshape(1, -1), conv_ln_b[l].reshape(1, -1)
        ps = pool_scale[l].reshape(1, -1)
        lat0 = T_CTX // DEC_SEQ
        conv = _conv(proj, conv_w32[l], cb, lg, lb, SEQ, BATCH, 0)
        conv = _conv(proj, conv_w32[l], cb, lg, lb, DEC_SEQ, DEC_BATCH, lat0, conv)
        pool = _pool(proj, pw_bf[l], ps, SEQ, BATCH, 0)
        pool = _pool(proj, pw_bf[l], ps, DEC_SEQ, DEC_BATCH, lat0, pool)
        four = _four(proj, wc_bf, wl_ctx, SEQ, BATCH, 0)
        four = _four(proj, wc_bf, wl_lat, DEC_SEQ, DEC_BATCH, lat0, four)
        x, h, gates = _outproj(attn, conv, pool, four, xs, w_out_bf[l], mod4, norm_ffn[l].reshape(1, D),
                               w_router[l], b_router[l].reshape(1, LANES))
        out = _moe(h, gates, wg_bf[l], wu_bf[l], wd_bf[l], x, mod4, fgain, l == DEPTH - 1)
        xs = (out,)

    y_prompt = out[0].reshape(BATCH, SEQ, D)
    y_sample = out[1].reshape(DEC_BATCH, DEC_SEQ, D)
    new_k = new_k.reshape(BATCH, DEPTH, SEQ, 2, HEAD_DIM)
    new_v = new_v.reshape(BATCH, DEPTH, SEQ, 2, HEAD_DIM)
    return (y_prompt, y_sample, new_k, new_v)
```

```python
import functools

import numpy as np
import jax
import jax.numpy as jnp
from jax import lax
from jax.experimental import pallas as pl
from jax.experimental.pallas import tpu as pltpu

F32 = jnp.float32
BF16 = jnp.bfloat16

D = 1024
BATCH = 32
SEQ = 256
DEPTH = 2
DEC_BATCH = 4
DEC_SEQ = 2048
PAST_LEN = 256
GRID_W = 64
HEAD_DIM = 64
ATTN_CH = 512
KV_CH = 128
CONV_CH = 512
CONV_WIDTH = 31
POOL_WINDOWS = (2, 4, 8, 16)
POOL_GROUP_CH = 128
FOURIER_HEAD_CH = 128
FOURIER_HEADS = 4
N_EXPERT_GROUPS = 4
EXPERTS_PER_GROUP = 4
N_EXPERTS = 16
EXPERT_FF = 256
ROPE_THETA = 10000.0
EPS = 1e-6
N_MOD = 6

T_CTX = BATCH * SEQ
T_LAT = DEC_BATCH * DEC_SEQ
T_ALL = T_CTX + T_LAT
LANES = 128
SUBLANES = 8
PAD = 16
CONV_ROWS = 128
ROUTER_LO_LANE = 32
IN_COLS = 2816
COL_Q, COL_CA, COL_CG, COL_POOL, COL_FOUR, COL_K, COL_V = 0, 512, 1024, 1536, 2048, 2560, 2688

TM = 512
RT = 512
DT = 512
TS = 256
NT = T_ALL // TS + N_EXPERT_GROUPS
NSLOT = NT * TS
HG_ROWS = D // LANES + 1
TQ = 256
LK = PAST_LEN + DEC_SEQ


def _cparams(sem, vmem_mb):
    return pltpu.CompilerParams(dimension_semantics=sem, vmem_limit_bytes=vmem_mb * 1024 * 1024)


def _mod_row(i, tm):
    nctx = T_CTX // tm
    per = DEC_SEQ // tm
    return jnp.where(i < nctx, 0, 1 + (i - nctx) // per)


def _mod_spec(k, tm):
    return pl.BlockSpec((None, None, 1, D), lambda i: (_mod_row(i, tm), k, 0, 0))


def _silu(x):
    return x * jax.nn.sigmoid(x)


def _ada_body(c_ref, w_ref, b_ref, o_ref):
    s = _silu(c_ref[...])
    o_ref[...] = jnp.dot(s.astype(BF16), w_ref[...].astype(BF16), preferred_element_type=F32) + b_ref[...]


def _ada(cond8, w_ada, b_ada):
    tn = 1536
    n = N_MOD * D
    return pl.pallas_call(
        _ada_body,
        grid=(DEPTH, n // tn),
        in_specs=[pl.BlockSpec((8, D), lambda l, j: (0, 0)),
                  pl.BlockSpec((None, D, tn), lambda l, j: (l, 0, j)),
                  pl.BlockSpec((None, 1, tn), lambda l, j: (l, 0, j))],
        out_specs=pl.BlockSpec((None, 8, tn), lambda l, j: (l, 0, j)),
        out_shape=jax.ShapeDtypeStruct((DEPTH, 8, n), F32),
        compiler_params=_cparams(("arbitrary", "arbitrary"), 40),
        name="ada_mod",
    )(cond8, w_ada, b_ada.reshape(DEPTH, 1, n))


def _x_specs(nx):
    if nx == 1:
        return [pl.BlockSpec((TM, D), lambda i: (i, 0))]
    n = T_CTX // TM
    return [pl.BlockSpec((TM, D), lambda i: (jnp.minimum(i, n - 1), 0)),
            pl.BlockSpec((TM, D), lambda i: (jnp.maximum(i - n, 0), 0))]


def _load_x(x_refs):
    if len(x_refs) == 1:
        return x_refs[0][...]
    return jnp.where(pl.program_id(0) < T_CTX // TM, x_refs[0][...], x_refs[1][...])


def _inproj_body(nx, *refs):
    g_ref, sc_ref, sh_ref, w_ref, o_ref = refs[nx:]
    x = _load_x(refs[:nx])
    ms = jnp.mean(x * x, axis=-1, keepdims=True)
    h = x * lax.rsqrt(ms + EPS) * g_ref[...]
    h = h * (1.0 + sc_ref[...]) + sh_ref[...]
    o_ref[...] = jnp.dot(h.astype(BF16), w_ref[...], preferred_element_type=F32)


def _inproj(xs, gain, mod4, w_in_bf):
    return pl.pallas_call(
        functools.partial(_inproj_body, len(xs)),
        grid=(T_ALL // TM,),
        in_specs=_x_specs(len(xs)) + [pl.BlockSpec((1, D), lambda i: (0, 0)),
                                      _mod_spec(1, TM), _mod_spec(0, TM),
                                      pl.BlockSpec((D, IN_COLS), lambda i: (0, 0))],
        out_specs=pl.BlockSpec((TM, IN_COLS), lambda i: (i, 0)),
        out_shape=jax.ShapeDtypeStruct((T_ALL, IN_COLS), F32),
        compiler_params=_cparams(("parallel",), 56),
        name="in_proj",
    )(*xs, gain, mod4, mod4, w_in_bf)


def _lane_lo():
    return lax.broadcasted_iota(jnp.int32, (1, LANES), 1) < HEAD_DIM


def _head_norm(x, gain):
    lo = _lane_lo()
    x2 = x * x
    s_lo = jnp.sum(jnp.where(lo, x2, 0.0), axis=-1, keepdims=True)
    s_hi = jnp.sum(jnp.where(lo, 0.0, x2), axis=-1, keepdims=True)
    r = jnp.where(lo, lax.rsqrt(s_lo * (1.0 / HEAD_DIM) + EPS), lax.rsqrt(s_hi * (1.0 / HEAD_DIM) + EPS))
    return x * r * gain


def _rope(x, cos, sin_signed):
    even = (lax.broadcasted_iota(jnp.int32, (1, LANES), 1) % 2) == 0
    swapped = jnp.where(even, pltpu.roll(x, LANES - 1, axis=1), pltpu.roll(x, 1, axis=1))
    return x * cos + swapped * sin_signed


def _split_heads(x, g):
    lo = _lane_lo()
    own = jnp.where(lo if g == 0 else jnp.logical_not(lo), x, 0.0)
    other = pltpu.roll(own, HEAD_DIM, axis=1)
    return (own, other) if g == 0 else (other, own)


def _attend_tile(q_bf, k_cat, v_cat, lk):
    s = lax.dot_general(q_bf, k_cat, (((1,), (1,)), ((), ())), preferred_element_type=F32)
    s_lo, s_hi = s[:, :lk], s[:, lk:]
    p_lo = jnp.exp(s_lo - jnp.max(s_lo, axis=-1, keepdims=True))
    p_hi = jnp.exp(s_hi - jnp.max(s_hi, axis=-1, keepdims=True))
    l_lo = jnp.sum(p_lo, axis=-1, keepdims=True)
    l_hi = jnp.sum(p_hi, axis=-1, keepdims=True)
    p = jnp.concatenate([p_lo, p_hi], axis=1).astype(BF16)
    o = jnp.dot(p, v_cat, preferred_element_type=F32)
    return o * jnp.where(_lane_lo(), 1.0 / l_lo, 1.0 / l_hi)


def _ctx_attn_body(q_ref, k_ref, v_ref, qg_ref, kg_ref, kc_in, vc_in, o_ref, kc_ref, vc_ref):
    del kc_in, vc_in
    kn = _head_norm(k_ref[...], kg_ref[...])
    v = v_ref[...]
    kc_ref[...] = kn
    vc_ref[...] = v
    scale = HEAD_DIM ** -0.5
    for g in range(2):
        k_lo, k_hi = _split_heads(kn, g)
        v_lo, v_hi = _split_heads(v, g)
        k_cat = jnp.concatenate([k_lo, k_hi], axis=0).astype(BF16)
        v_cat = jnp.concatenate([v_lo, v_hi], axis=0).astype(BF16)
        for tt in range(2):
            t = 2 * g + tt
            q = _head_norm(q_ref[:, t * LANES:(t + 1) * LANES], qg_ref[...]) * scale
            o = _attend_tile(q.astype(BF16), k_cat, v_cat, SEQ)
            o_ref[:, t * LANES:(t + 1) * LANES] = o.astype(BF16)


def _ctx_attn(proj, qg, kg, kcache, vcache, layer):
    nq = ATTN_CH // LANES
    return pl.pallas_call(
        _ctx_attn_body,
        grid=(BATCH,),
        in_specs=[pl.BlockSpec((SEQ, ATTN_CH), lambda b: (b, COL_Q // ATTN_CH)),
                  pl.BlockSpec((SEQ, KV_CH), lambda b: (b, COL_K // KV_CH)),
                  pl.BlockSpec((SEQ, KV_CH), lambda b: (b, COL_V // KV_CH)),
                  pl.BlockSpec((1, LANES), lambda b: (0, 0)),
                  pl.BlockSpec((1, LANES), lambda b: (0, 0)),
                  pl.BlockSpec(memory_space=pl.ANY),
                  pl.BlockSpec(memory_space=pl.ANY)],
        out_specs=[pl.BlockSpec((SEQ, ATTN_CH), lambda b: (b, 0)),
                   pl.BlockSpec((None, None, SEQ, KV_CH), lambda b: (b, layer, 0, 0)),
                   pl.BlockSpec((None, None, SEQ, KV_CH), lambda b: (b, layer, 0, 0))],
        out_shape=[jax.ShapeDtypeStruct((T_CTX, ATTN_CH), BF16),
                   jax.ShapeDtypeStruct((BATCH, DEPTH, SEQ, KV_CH), F32),
                   jax.ShapeDtypeStruct((BATCH, DEPTH, SEQ, KV_CH), F32)],
        input_output_aliases={5: 1, 6: 2},
        compiler_params=_cparams(("parallel",), 32),
        name="ctx_attn",
    )(proj, proj, proj, qg, kg, kcache, vcache)


def _lat_attn_body(q_ref, k_ref, v_ref, ck_ref, cv_ref, qg_ref, kg_ref, cosk_ref, sink_ref, cosq_ref, sinq_ref,
                   o_ref, kcat_ref, vcat_ref):
    @pl.when(pl.program_id(1) == 0)
    def _():
        rows = 256
        for c in range(LK // rows):
            if c == 0:
                kn = ck_ref[...]
                v = cv_ref[...]
            else:
                r0 = (c - 1) * rows
                kn = _head_norm(k_ref[r0:r0 + rows, :], kg_ref[...])
                kn = _rope(kn, cosk_ref[r0:r0 + rows, :], sink_ref[r0:r0 + rows, :])
                v = v_ref[r0:r0 + rows, :]
            for g in range(2):
                k_lo, k_hi = _split_heads(kn, g)
                v_lo, v_hi = _split_heads(v, g)
                kcat_ref[g, c * rows:(c + 1) * rows, :] = k_lo.astype(BF16)
                kcat_ref[g, LK + c * rows:LK + (c + 1) * rows, :] = k_hi.astype(BF16)
                vcat_ref[g, c * rows:(c + 1) * rows, :] = v_lo.astype(BF16)
                vcat_ref[g, LK + c * rows:LK + (c + 1) * rows, :] = v_hi.astype(BF16)

    scale = HEAD_DIM ** -0.5
    for t in range(ATTN_CH // LANES):
        g = t // 2
        q = _head_norm(q_ref[:, t * LANES:(t + 1) * LANES], qg_ref[...])
        q = _rope(q, cosq_ref[...], sinq_ref[...]) * scale
        o = _attend_tile(q.astype(BF16), kcat_ref[g], vcat_ref[g], LK)
        o_ref[:, t * LANES:(t + 1) * LANES] = o.astype(BF16)


def _lat_attn(proj, cache_k4, cache_v4, qg, kg, cos, sin, layer):
    nqb = DEC_SEQ // TQ
    row0 = T_CTX // DEC_SEQ
    return pl.pallas_call(
        _lat_attn_body,
        grid=(DEC_BATCH, nqb),
        in_specs=[pl.BlockSpec((TQ, ATTN_CH), lambda b, i: (T_CTX // TQ + b * nqb + i, COL_Q // ATTN_CH)),
                  pl.BlockSpec((DEC_SEQ, KV_CH), lambda b, i: (row0 + b, COL_K // KV_CH)),
                  pl.BlockSpec((DEC_SEQ, KV_CH), lambda b, i: (row0 + b, COL_V // KV_CH)),
                  pl.BlockSpec((None, None, PAST_LEN, KV_CH), lambda b, i: (b, layer, 0, 0)),
                  pl.BlockSpec((None, None, PAST_LEN, KV_CH), lambda b, i: (b, layer, 0, 0)),
                  pl.BlockSpec((1, LANES), lambda b, i: (0, 0)),
                  pl.BlockSpec((1, LANES), lambda b, i: (0, 0)),
                  pl.BlockSpec((DEC_SEQ, LANES), lambda b, i: (0, 0)),
                  pl.BlockSpec((DEC_SEQ, LANES), lambda b, i: (0, 0)),
                  pl.BlockSpec((TQ, LANES), lambda b, i: (i, 0)),
                  pl.BlockSpec((TQ, LANES), lambda b, i: (i, 0))],
        out_specs=pl.BlockSpec((TQ, ATTN_CH), lambda b, i: (b * nqb + i, 0)),
        out_shape=jax.ShapeDtypeStruct((T_LAT, ATTN_CH), BF16),
        scratch_shapes=[pltpu.VMEM((2, 2 * LK, LANES), BF16), pltpu.VMEM((2, 2 * LK, LANES), BF16)],
        compiler_params=_cparams(("parallel", "arbitrary"), 56),
        name="lat_attn",
    )(proj, proj, proj, cache_k4, cache_v4, qg, kg, cos, sin, cos, sin)


def _conv_body(L, a_ref, g_ref, w_ref, b_ref, lg_ref, lb_ref, o_ref, pad_ref, y_ref):
    rows = CONV_ROWS
    zeros = jnp.zeros((PAD, CONV_CH), F32)
    pad_ref[0:PAD, :] = zeros
    pad_ref[PAD + L:2 * PAD + L, :] = zeros
    pad_ref[PAD:PAD + L, :] = a_ref[...] * jax.nn.sigmoid(g_ref[...])
    shift0 = PAD - CONV_WIDTH // 2

    def chunk(c, carry):
        base = pl.multiple_of(c * rows, rows)
        for lg in range(CONV_CH // LANES):
            lanes = slice(lg * LANES, (lg + 1) * LANES)
            y = jnp.zeros((rows, LANES), F32)
            for r in range(SUBLANES):
                z = None
                for q in range((CONV_WIDTH + shift0) // SUBLANES + 1):
                    j = SUBLANES * q + r - shift0
                    if 0 <= j < CONV_WIDTH:
                        term = w_ref[j:j + 1, lanes] * pad_ref[pl.ds(base + SUBLANES * q, rows + SUBLANES), lanes]
                        z = term if z is None else z + term
                y = y + z[r:r + rows, :]
            y_ref[:, lanes] = y + b_ref[:, lanes]
        half = rows // 2
        for p in range(2):
            y = y_ref[p * half:(p + 1) * half, :]
            mu = jnp.mean(y, axis=-1, keepdims=True)
            yc = y - mu
            var = jnp.mean(yc * yc, axis=-1, keepdims=True)
            yn = yc * lax.rsqrt(var + EPS) * lg_ref[...] + lb_ref[...]
            o_ref[pl.ds(base + p * half, half), :] = _silu(yn).astype(BF16)
        return carry

    lax.fori_loop(0, L // rows, chunk, 0)


def _conv(proj, w32, b, lg, lb, L, nseq, row0):
    vec = pl.BlockSpec((1, CONV_CH), lambda s: (0, 0))
    return pl.pallas_call(
        functools.partial(_conv_body, L),
        grid=(nseq,),
        in_specs=[pl.BlockSpec((L, CONV_CH), lambda s: (row0 + s, COL_CA // CONV_CH)),
                  pl.BlockSpec((L, CONV_CH), lambda s: (row0 + s, COL_CG // CONV_CH)),
                  pl.BlockSpec((32, CONV_CH), lambda s: (0, 0)), vec, vec, vec],
        out_specs=pl.BlockSpec((L, CONV_CH), lambda s: (s, 0)),
        out_shape=jax.ShapeDtypeStruct((nseq * L, CONV_CH), BF16),
        scratch_shapes=[pltpu.VMEM((L + 2 * PAD, CONV_CH), F32), pltpu.VMEM((CONV_ROWS, CONV_CH), F32)],
        compiler_params=_cparams(("parallel",), 48),
        name="conv_L%d" % L,
    )(proj, proj, w32, b, lg, lb)


def _pool_body(L, u_ref, w_ref, sc_ref, o_ref, pad_ref):
    rows = 256
    zeros = jnp.zeros((PAD, 512), F32)
    pad_ref[0:PAD, :] = zeros
    pad_ref[PAD + L:2 * PAD + L, :] = zeros
    pad_ref[PAD:PAD + L, :] = u_ref[...]
    for c in range(L // rows):
        r0 = c * rows
        t = lax.broadcasted_iota(jnp.int32, (rows, 1), 0) + r0
        for g, w in enumerate(POOL_WINDOWS):
            cols = slice(g * POOL_GROUP_CH, (g + 1) * POOL_GROUP_CH)
            acc = jnp.zeros((rows, POOL_GROUP_CH), F32)
            for i in range(-(w // 2), w - w // 2):
                acc = acc + pad_ref[PAD + r0 + i:PAD + r0 + i + rows, cols]
            lo = jnp.maximum(t - w // 2, 0)
            hi = jnp.minimum(t + (w - w // 2), L)
            cnt = (hi - lo).astype(F32)
            pooled = acc / cnt - pad_ref[PAD + r0:PAD + r0 + rows, cols]
            mixed = jnp.dot(pooled.astype(BF16), w_ref[g], preferred_element_type=F32)
            o_ref[r0:r0 + rows, cols] = (mixed * sc_ref[:, cols]).astype(BF16)


def _pool(proj, pw_bf, pscale, L, nseq, row0):
    return pl.pallas_call(
        functools.partial(_pool_body, L),
        grid=(nseq,),
        in_specs=[pl.BlockSpec((L, 512), lambda s: (row0 + s, COL_POOL // 512)),
                  pl.BlockSpec((4, POOL_GROUP_CH, POOL_GROUP_CH), lambda s: (0, 0, 0)),
                  pl.BlockSpec((1, 512), lambda s: (0, 0))],
        out_specs=pl.BlockSpec((L, 512), lambda s: (s, 0)),
        out_shape=jax.ShapeDtypeStruct((nseq * L, 512), BF16),
        scratch_shapes=[pltpu.VMEM((L + 2 * PAD, 512), F32)],
        compiler_params=_cparams(("parallel",), 48),
        name="pool_L%d" % L,
    )(proj, pw_bf, pscale)


def _channel_dft(u, wc_ref, z_ref, r0, L):
    rows = u.shape[0]
    for h in range(FOURIER_HEADS):
        cols = slice(h * FOURIER_HEAD_CH, (h + 1) * FOURIER_HEAD_CH)
        a = jnp.dot(u[:, cols].astype(BF16), wc_ref[...], preferred_element_type=F32)
        z_ref[r0:r0 + rows, cols] = a[:, :FOURIER_HEAD_CH].astype(BF16)
        z_ref[L + r0:L + r0 + rows, cols] = a[:, FOURIER_HEAD_CH:].astype(BF16)


def _four_body(L, tr, u_ref, wc_ref, wl_ref, o_ref, z_ref):
    @pl.when(pl.program_id(1) == 0)
    def _():
        rows = min(L, 512)
        for c in range(L // rows):
            _channel_dft(u_ref[c * rows:(c + 1) * rows, :], wc_ref, z_ref, c * rows, L)

    scale = (L * FOURIER_HEAD_CH) ** -0.5
    o_ref[...] = (jnp.dot(wl_ref[...], z_ref[...], preferred_element_type=F32) * scale).astype(BF16)


def _four(proj, wc_bf, wl_bf, L, nseq, row0):
    tr = min(L, 256)
    return pl.pallas_call(
        functools.partial(_four_body, L, tr),
        grid=(nseq, L // tr),
        in_specs=[pl.BlockSpec((L, 512), lambda s, r: (row0 + s, COL_FOUR // 512)),
                  pl.BlockSpec((FOURIER_HEAD_CH, 2 * FOURIER_HEAD_CH), lambda s, r: (0, 0)),
                  pl.BlockSpec((tr, 2 * L), lambda s, r: (r, 0))],
        out_specs=pl.BlockSpec((tr, 512), lambda s, r: (s * (L // tr) + r, 0)),
        out_shape=jax.ShapeDtypeStruct((nseq * L, 512), BF16),
        scratch_shapes=[pltpu.VMEM((2 * L, 512), BF16)],
        compiler_params=_cparams(("parallel", "arbitrary"), 48),
        name="four_L%d" % L,
    )(proj, wc_bf, wl_bf)


def _route(logits):
    lane = lax.broadcasted_iota(jnp.int32, logits.shape, 1).astype(F32)
    neg = jnp.float32(-jnp.inf)
    big = jnp.float32(1 << 20)
    is_g = (lane >= N_EXPERTS) & (lane < N_EXPERTS + N_EXPERT_GROUPS)
    gl = jnp.where(is_g, logits, neg)
    gmax = jnp.max(gl, axis=-1, keepdims=True)
    g_idx = jnp.min(jnp.where(gl == gmax, lane, big), axis=-1, keepdims=True) - N_EXPERTS
    denom = jnp.sum(jnp.where(is_g, jnp.exp(gl - gmax), 0.0), axis=-1, keepdims=True)
    g_w = 1.0 / denom
    in_grp = (lane >= g_idx * EXPERTS_PER_GROUP) & (lane < (g_idx + 1) * EXPERTS_PER_GROUP)
    el = jnp.where(in_grp, logits, neg)
    v1 = jnp.max(el, axis=-1, keepdims=True)
    i1 = jnp.min(jnp.where(el == v1, lane, big), axis=-1, keepdims=True)
    el2 = jnp.where(lane == i1, neg, el)
    v2 = jnp.max(el2, axis=-1, keepdims=True)
    i2 = jnp.min(jnp.where(el2 == v2, lane, big), axis=-1, keepdims=True)
    e2 = jnp.exp(v2 - v1)
    p1 = 1.0 / (1.0 + e2)
    p2 = e2 / (1.0 + e2)
    gates = g_w * jnp.where(lane == i1, p1, jnp.where(lane == i2, p2, 0.0))
    return gates + jnp.where(lane == g_idx + N_EXPERTS, 1.0, 0.0)


def _pair_specs(width):
    n = T_CTX // TM
    return [pl.BlockSpec((TM, width), lambda i: (jnp.minimum(i, n - 1), 0)),
            pl.BlockSpec((TM, width), lambda i: (jnp.maximum(i - n, 0), 0))]


def _outproj_body(nx, *refs):
    mixers = refs[nx:nx + 8]
    (w_ref, g1_ref, gain_ref, sc_ref, sh_ref, wr_ref, br_ref, xo_ref, hg_ref, route_ref) = refs[nx + 8:]
    mix = None
    for k in range(4):
        part = jnp.dot(_load_x(mixers[2 * k:2 * k + 2]), w_ref[512 * k:512 * (k + 1), :],
                       preferred_element_type=F32)
        mix = part if mix is None else mix + part
    x = _load_x(refs[:nx]) + g1_ref[...] * mix
    xo_ref[...] = x
    ms = jnp.mean(x * x, axis=-1, keepdims=True)
    h = x * lax.rsqrt(ms + EPS) * gain_ref[...]
    h = h * (1.0 + sc_ref[...]) + sh_ref[...]
    h_hi = h.astype(BF16)
    h_lo = (h - h_hi.astype(F32)).astype(BF16)
    t = (jnp.dot(h_hi, wr_ref[...], preferred_element_type=F32)
         + jnp.dot(h_lo, wr_ref[...], preferred_element_type=F32))
    logits = t + pltpu.roll(t, LANES - ROUTER_LO_LANE, axis=1) + br_ref[...]
    route = _route(logits)
    route_ref[...] = route
    for s in range(D // LANES):
        hg_ref[:, s, :] = h[:, s * LANES:(s + 1) * LANES]
    hg_ref[:, D // LANES, :] = route


def _outproj(mixers, xs, w_out_bf, mod4, gain, wr, br):
    tile = lambda w: pl.BlockSpec((TM, w), lambda i: (i, 0))
    const = lambda r, c: pl.BlockSpec((r, c), lambda i: (0, 0))
    mix_specs = []
    for _ in range(4):
        mix_specs += _pair_specs(512)
    return pl.pallas_call(
        functools.partial(_outproj_body, len(xs)),
        grid=(T_ALL // TM,),
        in_specs=_x_specs(len(xs)) + mix_specs + [const(2048, D),
                                                  _mod_spec(2, TM), const(1, D), _mod_spec(4, TM), _mod_spec(3, TM),
                                                  const(D, LANES), const(1, LANES)],
        out_specs=[tile(D), pl.BlockSpec((TM, HG_ROWS, LANES), lambda i: (i, 0, 0)), tile(LANES)],
        out_shape=[jax.ShapeDtypeStruct((T_ALL, D), F32),
                   jax.ShapeDtypeStruct((T_ALL, HG_ROWS, LANES), F32),
                   jax.ShapeDtypeStruct((T_ALL, LANES), F32)],
        compiler_params=_cparams(("parallel",), 56),
        name="out_proj",
    )(*xs, *[a for pair in mixers for a in pair], w_out_bf, mod4, gain, mod4, mod4, wr, br)


def _rank_body(r_ref, o_ref, cnt_ref, carry_ref):
    @pl.when(pl.program_id(0) == 0)
    def _():
        carry_ref[...] = jnp.zeros_like(carry_ref)

    r = r_ref[...]
    lane = lax.broadcasted_iota(jnp.int32, r.shape, 1)
    is_g = (lane >= N_EXPERTS) & (lane < N_EXPERTS + N_EXPERT_GROUPS)
    onehot = jnp.where(is_g, r, 0.0)
    row = lax.broadcasted_iota(jnp.int32, (RT, RT), 0)
    col = lax.broadcasted_iota(jnp.int32, (RT, RT), 1)
    earlier = jnp.where(col < row, 1.0, 0.0).astype(BF16)
    before = jnp.dot(earlier, onehot.astype(BF16), preferred_element_type=F32) + carry_ref[...]
    rank = jnp.sum(onehot * before, axis=-1, keepdims=True)
    grp = jnp.sum(onehot * (lane - N_EXPERTS).astype(F32), axis=-1, keepdims=True)
    packed = jnp.where(lane == 0, rank, jnp.where(lane == 1, grp, 0.0))
    o_ref[...] = packed.T[0:SUBLANES, :].astype(jnp.int32)
    carry_ref[...] += jnp.sum(onehot, axis=0, keepdims=True)
    cnt_ref[...] = carry_ref[...]


def _rank(route):
    return pl.pallas_call(
        _rank_body,
        grid=(T_ALL // RT,),
        in_specs=[pl.BlockSpec((RT, LANES), lambda i: (i, 0))],
        out_specs=[pl.BlockSpec((SUBLANES, RT), lambda i: (i, 0)),
                   pl.BlockSpec((1, LANES), lambda i: (0, 0))],
        out_shape=[jax.ShapeDtypeStruct((T_ALL // RT * SUBLANES, RT), jnp.int32),
                   jax.ShapeDtypeStruct((1, LANES), F32)],
        scratch_shapes=[pltpu.VMEM((1, LANES), F32)],
        compiler_params=_cparams(("arbitrary",), 32),
        name="moe_rank",
    )(route)


def _wait_rows(copy, n):
    def body(t, c):
        copy.wait()
        return c
    lax.fori_loop(0, n, body, 0, unroll=8)


def _dispatch_body(slot_sm, fstart_sm, flen_sm, hg_hbm, xs_hbm, zrow_ref, sem):
    i = pl.program_id(0)
    n = pl.num_programs(0)
    base = i * DT

    @pl.when(i == 0)
    def _():
        zrow_ref[...] = jnp.zeros_like(zrow_ref)
        for g in range(N_EXPERT_GROUPS):
            def fill(r, c):
                pltpu.make_async_copy(zrow_ref, xs_hbm.at[fstart_sm[g] + r], sem.at[2]).start()
                return c
            lax.fori_loop(0, flen_sm[g], fill, 0)
        _wait_rows(pltpu.make_async_copy(zrow_ref, xs_hbm.at[0], sem.at[2]), NSLOT - T_ALL)

    def issue(t, c):
        pltpu.make_async_copy(hg_hbm.at[base + t], xs_hbm.at[slot_sm[base + t]], sem.at[i % 2]).start()
        return c
    lax.fori_loop(0, DT, issue, 0, unroll=8)

    @pl.when(i > 0)
    def _():
        _wait_rows(pltpu.make_async_copy(hg_hbm.at[0], xs_hbm.at[0], sem.at[(i + 1) % 2]), DT)

    @pl.when(i == n - 1)
    def _():
        _wait_rows(pltpu.make_async_copy(hg_hbm.at[0], xs_hbm.at[0], sem.at[i % 2]), DT)


def _dispatch(slot, fstart, flen, hg):
    return pl.pallas_call(
        _dispatch_body,
        grid_spec=pltpu.PrefetchScalarGridSpec(
            num_scalar_prefetch=3,
            grid=(T_ALL // DT,),
            in_specs=[pl.BlockSpec(memory_space=pl.ANY)],
            out_specs=pl.BlockSpec(memory_space=pl.ANY),
            scratch_shapes=[pltpu.VMEM((HG_ROWS, LANES), F32), pltpu.SemaphoreType.DMA((3,))]),
        out_shape=jax.ShapeDtypeStruct((NSLOT, HG_ROWS, LANES), F32),
        compiler_params=_cparams(("arbitrary",), 32),
        name="moe_dispatch",
    )(slot, fstart, flen, hg)


def _ffn_body(tgrp_sm, xs_ref, wg_ref, wu_ref, wd_ref, ys_ref, h_ref):
    grp = tgrp_sm[pl.program_id(0)]
    nrow = D // LANES
    for s in range(nrow):
        h_ref[:, s * LANES:(s + 1) * LANES] = xs_ref[:, s, :].astype(BF16)
    gates = xs_ref[:, nrow, :]
    h = h_ref[...]
    a = jnp.dot(h, wg_ref[...], preferred_element_type=F32)
    b = jnp.dot(h, wu_ref[...], preferred_element_type=F32)
    hid = _silu(a) * b
    lane = lax.broadcasted_iota(jnp.int32, gates.shape, 1)
    parts = []
    for e in range(EXPERTS_PER_GROUP):
        ge = jnp.sum(jnp.where(lane == grp * EXPERTS_PER_GROUP + e, gates, 0.0), axis=-1, keepdims=True)
        parts.append((hid[:, e * EXPERT_FF:(e + 1) * EXPERT_FF] * ge).astype(BF16))
    y = jnp.dot(jnp.concatenate(parts, axis=1), wd_ref[...], preferred_element_type=F32)
    for s in range(nrow):
        ys_ref[:, s, :] = y[:, s * LANES:(s + 1) * LANES]


def _ffn(tile_grp, xs, wg_grp, wu_grp, wd_grp):
    ffw = EXPERTS_PER_GROUP * EXPERT_FF
    return pl.pallas_call(
        _ffn_body,
        grid_spec=pltpu.PrefetchScalarGridSpec(
            num_scalar_prefetch=1,
            grid=(NT,),
            in_specs=[pl.BlockSpec((TS, HG_ROWS, LANES), lambda j, tg: (j, 0, 0)),
                      pl.BlockSpec((None, D, ffw), lambda j, tg: (tg[j], 0, 0)),
                      pl.BlockSpec((None, D, ffw), lambda j, tg: (tg[j], 0, 0)),
                      pl.BlockSpec((None, ffw, D), lambda j, tg: (tg[j], 0, 0))],
            out_specs=pl.BlockSpec((TS, D // LANES, LANES), lambda j, tg: (j, 0, 0)),
            scratch_shapes=[pltpu.VMEM((TS, D), BF16)]),
        out_shape=jax.ShapeDtypeStruct((NSLOT, D // LANES, LANES), F32),
        compiler_params=_cparams(("arbitrary",), 48),
        name="moe_ffn",
    )(tile_grp, xs, wg_grp, wu_grp, wd_grp)


def _combine_body(final, slot_sm, ys_hbm, x_ref, g2_ref, fg_ref, *rest):
    outs, buf, sem = rest[:-2], rest[-2], rest[-1]
    i = pl.program_id(0)
    n = pl.num_programs(0)

    def issue(step, b):
        def body(t, c):
            pltpu.make_async_copy(ys_hbm.at[slot_sm[step * TM + t]], buf.at[b, t], sem.at[b]).start()
            return c
        lax.fori_loop(0, TM, body, 0, unroll=8)

    @pl.when(i == 0)
    def _():
        issue(0, 0)

    @pl.when(i + 1 < n)
    def _():
        issue(i + 1, (i + 1) % 2)

    b = i % 2
    _wait_rows(pltpu.make_async_copy(ys_hbm.at[0], buf.at[b, 0], sem.at[b]), TM)
    y = jnp.concatenate([buf[b, :, s, :] for s in range(D // LANES)], axis=1)
    x = x_ref[...] + g2_ref[...] * y
    if not final:
        outs[0][...] = x
    else:
        ms = jnp.mean(x * x, axis=-1, keepdims=True)
        yn = x * lax.rsqrt(ms + EPS) * fg_ref[...]
        is_ctx = i < T_CTX // TM

        @pl.when(is_ctx)
        def _():
            outs[0][...] = yn

        @pl.when(jnp.logical_not(is_ctx))
        def _():
            outs[1][...] = yn


def _combine(slot, ys, x, mod4, final_gain, final):
    if final:
        nc = T_CTX // TM
        out_specs = [pl.BlockSpec((TM, D), lambda i, s: (jnp.minimum(i, nc - 1), 0)),
                     pl.BlockSpec((TM, D), lambda i, s: (jnp.maximum(i - nc, 0), 0))]
        out_shape = [jax.ShapeDtypeStruct((T_CTX, D), F32), jax.ShapeDtypeStruct((T_LAT, D), F32)]
    else:
        out_specs = pl.BlockSpec((TM, D), lambda i, s: (i, 0))
        out_shape = jax.ShapeDtypeStruct((T_ALL, D), F32)
    return pl.pallas_call(
        functools.partial(_combine_body, final),
        grid_spec=pltpu.PrefetchScalarGridSpec(
            num_scalar_prefetch=1,
            grid=(T_ALL // TM,),
            in_specs=[pl.BlockSpec(memory_space=pl.ANY),
                      pl.BlockSpec((TM, D), lambda i, s: (i, 0)),
                      pl.BlockSpec((None, None, 1, D), lambda i, s: (_mod_row(i, TM), 5, 0, 0)),
                      pl.BlockSpec((1, D), lambda i, s: (0, 0))],
            out_specs=out_specs,
            scratch_shapes=[pltpu.VMEM((2, TM, D // LANES, LANES), F32), pltpu.SemaphoreType.DMA((2,))]),
        out_shape=out_shape,
        compiler_params=_cparams(("arbitrary",), 48),
        name="moe_combine",
    )(slot, ys, x, mod4, final_gain)


def _moe_plan(rk, cnt):
    rk = rk.reshape(T_ALL // RT, SUBLANES, RT)
    rank = rk[:, 0, :].reshape(T_ALL)
    grp = rk[:, 1, :].reshape(T_ALL)
    cnt = cnt[0, N_EXPERTS:N_EXPERTS + N_EXPERT_GROUPS].astype(jnp.int32)
    padded = (cnt + TS - 1) // TS * TS
    off = jnp.cumsum(padded) - padded
    padded = padded.at[N_EXPERT_GROUPS - 1].set(NSLOT - off[N_EXPERT_GROUPS - 1])
    slot = rank
    for g in range(1, N_EXPERT_GROUPS):
        slot = slot + jnp.where(grp == g, off[g], 0)
    ends = (off + padded)[:N_EXPERT_GROUPS - 1]
    tile_grp = jnp.sum((jnp.arange(NT, dtype=jnp.int32)[:, None] * TS >= ends[None, :]).astype(jnp.int32), axis=1)
    return slot, tile_grp, off + cnt, padded - cnt


def _rope_tables():
    t = np.arange(DEC_SEQ)
    pos = np.stack([t // GRID_W, t % GRID_W], axis=1).astype(np.float64)
    inv = ROPE_THETA ** (-np.arange(16, dtype=np.float64) / 16.0)
    d = np.arange(LANES) % HEAD_DIM
    pair = d // 2
    ang = pos[:, (pair >= 16).astype(np.int64)] * inv[pair % 16][None, :]
    sign = np.where(d % 2 == 0, -1.0, 1.0)
    return np.cos(ang).astype(np.float32), (np.sin(ang) * sign[None, :]).astype(np.float32)


def _dft_pos(L):
    j = np.arange(L)
    ang = 2.0 * np.pi * ((j[:, None] * j[None, :]) % L) / L
    return np.concatenate([np.cos(ang), -np.sin(ang)], axis=1).astype(np.float32)


def _dft_ch():
    c = np.arange(FOURIER_HEAD_CH)
    ang = 2.0 * np.pi * ((c[:, None] * c[None, :]) % FOURIER_HEAD_CH) / FOURIER_HEAD_CH
    return np.concatenate([np.cos(ang), np.sin(ang)], axis=1).astype(np.float32)


def kernel(x_prompt, x_sample, cache_k, cache_v, c, c_ctx, w_ada, b_ada, norm_mix, norm_ffn, w_in, w_out, q_norm, k_norm, conv_w, conv_b, conv_ln_g, conv_ln_b, pool_w, pool_scale, router_group_w, router_group_b, router_expert_w, router_expert_b, moe_w_gate, moe_w_up, moe_w_down, final_norm):
    xs = (x_prompt.reshape(T_CTX, D), x_sample.reshape(T_LAT, D))
    cond8 =jnp.concatenate([c_ctx[None, :], c, jnp.zeros((3, D), F32)], axis=0)
    mod = _ada(cond8, w_ada, b_ada)

    o1, o2, o3 = ATTN_CH, ATTN_CH + KV_CH, ATTN_CH + 2 * KV_CH
    w_in_p = jnp.concatenate([w_in[:, :, :o1], w_in[:, :, o3:], w_in[:, :, o1:o3]], axis=-1).astype(BF16)
    w_out_bf = w_out.astype(BF16)
    ffw = EXPERTS_PER_GROUP * EXPERT_FF
    by_group = lambda w: (w.reshape(DEPTH, N_EXPERT_GROUPS, EXPERTS_PER_GROUP, D, EXPERT_FF)
                          .transpose(0, 1, 3, 2, 4).reshape(DEPTH, N_EXPERT_GROUPS, D, ffw).astype(BF16))
    wg_grp, wu_grp = by_group(moe_w_gate), by_group(moe_w_up)
    wd_grp = moe_w_down.reshape(DEPTH, N_EXPERT_GROUPS, ffw, D).astype(BF16)
    pw_bf = pool_w.astype(BF16)
    n_route = N_EXPERTS + N_EXPERT_GROUPS
    wr = jnp.concatenate([router_expert_w, router_group_w], axis=-1)
    wr_hi = wr.astype(BF16)
    wr_lo = (wr - wr_hi.astype(F32)).astype(BF16)
    w_router = jnp.concatenate([wr_hi, jnp.zeros((DEPTH, D, ROUTER_LO_LANE - n_route), BF16), wr_lo,
                                jnp.zeros((DEPTH, D, LANES - ROUTER_LO_LANE - n_route), BF16)], axis=-1)
    b_router = jnp.concatenate([router_expert_b, router_group_b,
                                jnp.zeros((DEPTH, LANES - N_EXPERTS - N_EXPERT_GROUPS), F32)], axis=-1)
    conv_w32 = jnp.concatenate([conv_w, jnp.zeros((DEPTH, 1, CONV_CH), F32)], axis=1)

    cos_np, sin_np = _rope_tables()
    cos, sin = jnp.asarray(cos_np), jnp.asarray(sin_np)
    wc_bf = jnp.asarray(_dft_ch()).astype(BF16)
    wl_ctx = jnp.asarray(_dft_pos(SEQ)).astype(BF16)
    wl_lat = jnp.asarray(_dft_pos(DEC_SEQ)).astype(BF16)

    cache_k4 = cache_k.reshape(DEC_BATCH, DEPTH, PAST_LEN, KV_CH)
    cache_v4 = cache_v.reshape(DEC_BATCH, DEPTH, PAST_LEN, KV_CH)
    new_k = jnp.zeros((BATCH, DEPTH, SEQ, KV_CH), F32)
    new_v = jnp.zeros((BATCH, DEPTH, SEQ, KV_CH), F32)
    fgain = final_norm.reshape(1, D)

    for l in range(DEPTH):
        mod4 = mod[l].reshape(8, N_MOD, 1, D)
        proj = _inproj(xs, norm_mix[l].reshape(1, D), mod4, w_in_p[l])
        qg = jnp.tile(q_norm[l], 2).reshape(1, LANES)
        kg = jnp.tile(k_norm[l], 2).reshape(1, LANES)
        attn_c, new_k, new_v = _ctx_attn(proj, qg, kg, new_k, new_v, l)
        attn_l = _lat_attn(proj, cache_k4, cache_v4, qg, kg, cos, sin, l)
        cb, lg, lb = conv_b[l].reshape(1, -1), conv_ln_g[l].reshape(1, -1), conv_ln_b[l].reshape(1, -1)
        ps = pool_scale[l].reshape(1, -1)
        lat0 = T_CTX // DEC_SEQ
        mixers = [(attn_c, attn_l),
                  (_conv(proj, conv_w32[l], cb, lg, lb, SEQ, BATCH, 0),
                   _conv(proj, conv_w32[l], cb, lg, lb, DEC_SEQ, DEC_BATCH, lat0)),
                  (_pool(proj, pw_bf[l], ps, SEQ, BATCH, 0),
                   _pool(proj, pw_bf[l], ps, DEC_SEQ, DEC_BATCH, lat0)),
                  (_four(proj, wc_bf, wl_ctx, SEQ, BATCH, 0),
                   _four(proj, wc_bf, wl_lat, DEC_SEQ, DEC_BATCH, lat0))]
        x, hg, route = _outproj(mixers, xs, w_out_bf[l], mod4, norm_ffn[l].reshape(1, D),
                                w_router[l], b_router[l].reshape(1, LANES))
        slot, tile_grp, fstart, flen = _moe_plan(*_rank(route))
        xslots = _dispatch(slot, fstart, flen, hg)
        yslots = _ffn(tile_grp, xslots, wg_grp[l], wu_grp[l], wd_grp[l])
        out = _combine(slot, yslots, x, mod4, fgain, l == DEPTH - 1)
        xs = (out,)

    y_prompt = out[0].reshape(BATCH, SEQ, D)
    y_sample = out[1].reshape(DEC_BATCH, DEC_SEQ, D)
    new_k = new_k.reshape(BATCH, DEPTH, SEQ, 2, HEAD_DIM)
    new_v = new_v.reshape(BATCH, DEPTH, SEQ, 2, HEAD_DIM)
    return (y_prompt, y_sample, new_k, new_v)
```

```python
import functools

import numpy as np
import jax
import jax.numpy as jnp
from jax import lax
from jax.experimental import pallas as pl
from jax.experimental.pallas import tpu as pltpu

F32 = jnp.float32
BF16 = jnp.bfloat16

D = 1024
BATCH = 32
SEQ = 256
DEPTH = 2
DEC_BATCH = 4
DEC_SEQ = 2048
PAST_LEN = 256
GRID_W = 64
HEAD_DIM = 64
ATTN_CH = 512
KV_CH = 128
CONV_CH = 512
CONV_WIDTH = 31
POOL_WINDOWS = (2, 4, 8, 16)
POOL_GROUP_CH = 128
FOURIER_HEAD_CH = 128
FOURIER_HEADS = 4
N_EXPERT_GROUPS = 4
EXPERTS_PER_GROUP = 4
N_EXPERTS = 16
EXPERT_FF = 256
ROPE_THETA = 10000.0
EPS = 1e-6
N_MOD = 6

T_CTX = BATCH * SEQ
T_LAT = DEC_BATCH * DEC_SEQ
T_ALL = T_CTX + T_LAT
LANES = 128
SUBLANES = 8
PAD = 16
CONV_ROWS = 128
ROUTER_LO_LANE = 32
IN_COLS = 2816
COL_Q, COL_CA, COL_CG, COL_POOL, COL_FOUR, COL_K, COL_V = 0, 512, 1024, 1536, 2048, 2560, 2688

TM = 512
RT = 512
DT = 1024
TS = 256
NT = T_ALL // TS + N_EXPERT_GROUPS
NSLOT = NT * TS
H_ROWS = D // LANES
REC = 2 * SUBLANES
TQ = 256
LK = PAST_LEN + DEC_SEQ


def _cparams(sem, vmem_mb):
    return pltpu.CompilerParams(dimension_semantics=sem, vmem_limit_bytes=vmem_mb * 1024 * 1024)


def _mod_row(i, tm):
    nctx = T_CTX // tm
    per = DEC_SEQ // tm
    return jnp.where(i < nctx, 0, 1 + (i - nctx) // per)


def _mod_spec(k, tm):
    return pl.BlockSpec((None, None, 1, D), lambda i: (_mod_row(i, tm), k, 0, 0))


def _silu(x):
    return x * jax.nn.sigmoid(x)


def _ada_body(c_ref, w_ref, b_ref, o_ref):
    s = _silu(c_ref[...])
    o_ref[...] = jnp.dot(s.astype(BF16), w_ref[...].astype(BF16), preferred_element_type=F32) + b_ref[...]


def _ada(cond8, w_ada, b_ada):
    tn = 1536
    n = N_MOD * D
    return pl.pallas_call(
        _ada_body,
        grid=(DEPTH, n // tn),
        in_specs=[pl.BlockSpec((8, D), lambda l, j: (0, 0)),
                  pl.BlockSpec((None, D, tn), lambda l, j: (l, 0, j)),
                  pl.BlockSpec((None, 1, tn), lambda l, j: (l, 0, j))],
        out_specs=pl.BlockSpec((None, 8, tn), lambda l, j: (l, 0, j)),
        out_shape=jax.ShapeDtypeStruct((DEPTH, 8, n), F32),
        compiler_params=_cparams(("arbitrary", "arbitrary"), 40),
        name="ada_mod",
    )(cond8, w_ada, b_ada.reshape(DEPTH, 1, n))


def _x_specs(nx):
    if nx == 1:
        return [pl.BlockSpec((TM, D), lambda i: (i, 0))]
    n = T_CTX // TM
    return [pl.BlockSpec((TM, D), lambda i: (jnp.minimum(i, n - 1), 0)),
            pl.BlockSpec((TM, D), lambda i: (jnp.maximum(i - n, 0), 0))]


def _load_x(x_refs):
    if len(x_refs) == 1:
        return x_refs[0][...]
    return jnp.where(pl.program_id(0) < T_CTX // TM, x_refs[0][...], x_refs[1][...])


def _inproj_body(nx, *refs):
    g_ref, sc_ref, sh_ref, w_ref, o_ref = refs[nx:]
    x = _load_x(refs[:nx])
    ms = jnp.mean(x * x, axis=-1, keepdims=True)
    h = x * lax.rsqrt(ms + EPS) * g_ref[...]
    h = h * (1.0 + sc_ref[...]) + sh_ref[...]
    o_ref[...] = jnp.dot(h.astype(BF16), w_ref[...], preferred_element_type=F32)


def _inproj(xs, gain, mod4, w_in_bf):
    return pl.pallas_call(
        functools.partial(_inproj_body, len(xs)),
        grid=(T_ALL // TM,),
        in_specs=_x_specs(len(xs)) + [pl.BlockSpec((1, D), lambda i: (0, 0)),
                                      _mod_spec(1, TM), _mod_spec(0, TM),
                                      pl.BlockSpec((D, IN_COLS), lambda i: (0, 0))],
        out_specs=pl.BlockSpec((TM, IN_COLS), lambda i: (i, 0)),
        out_shape=jax.ShapeDtypeStruct((T_ALL, IN_COLS), F32),
        compiler_params=_cparams(("parallel",), 56),
        name="in_proj",
    )(*xs, gain, mod4, mod4, w_in_bf)


def _lane_lo():
    return lax.broadcasted_iota(jnp.int32, (1, LANES), 1) < HEAD_DIM


def _head_norm(x, gain):
    lo = _lane_lo()
    x2 = x * x
    s_lo = jnp.sum(jnp.where(lo, x2, 0.0), axis=-1, keepdims=True)
    s_hi = jnp.sum(jnp.where(lo, 0.0, x2), axis=-1, keepdims=True)
    r = jnp.where(lo, lax.rsqrt(s_lo * (1.0 / HEAD_DIM) + EPS), lax.rsqrt(s_hi * (1.0 / HEAD_DIM) + EPS))
    return x * r * gain


def _rope(x, cos, sin_signed):
    even = (lax.broadcasted_iota(jnp.int32, (1, LANES), 1) % 2) == 0
    swapped = jnp.where(even, pltpu.roll(x, LANES - 1, axis=1), pltpu.roll(x, 1, axis=1))
    return x * cos + swapped * sin_signed


def _split_heads(x, g):
    lo = _lane_lo()
    own = jnp.where(lo if g == 0 else jnp.logical_not(lo), x, 0.0)
    other = pltpu.roll(own, HEAD_DIM, axis=1)
    return (own, other) if g == 0 else (other, own)


def _attend_tile(q_bf, k_cat, v_cat, lk):
    s = lax.dot_general(q_bf, k_cat, (((1,), (1,)), ((), ())), preferred_element_type=F32)
    s_lo, s_hi = s[:, :lk], s[:, lk:]
    p_lo = jnp.exp(s_lo - jnp.max(s_lo, axis=-1, keepdims=True))
    p_hi = jnp.exp(s_hi - jnp.max(s_hi, axis=-1, keepdims=True))
    l_lo = jnp.sum(p_lo, axis=-1, keepdims=True)
    l_hi = jnp.sum(p_hi, axis=-1, keepdims=True)
    p = jnp.concatenate([p_lo, p_hi], axis=1).astype(BF16)
    o = jnp.dot(p, v_cat, preferred_element_type=F32)
    return o * jnp.where(_lane_lo(), 1.0 / l_lo, 1.0 / l_hi)


def _ctx_attn_body(q_ref, k_ref, v_ref, qg_ref, kg_ref, kc_in, vc_in, o_ref, kc_ref, vc_ref):
    del kc_in, vc_in
    kn = _head_norm(k_ref[...], kg_ref[...])
    v = v_ref[...]
    kc_ref[...] = kn
    vc_ref[...] = v
    scale = HEAD_DIM ** -0.5
    for g in range(2):
        k_lo, k_hi = _split_heads(kn, g)
        v_lo, v_hi = _split_heads(v, g)
        k_cat = jnp.concatenate([k_lo, k_hi], axis=0).astype(BF16)
        v_cat = jnp.concatenate([v_lo, v_hi], axis=0).astype(BF16)
        for tt in range(2):
            t = 2 * g + tt
            q = _head_norm(q_ref[:, t * LANES:(t + 1) * LANES], qg_ref[...]) * scale
            o = _attend_tile(q.astype(BF16), k_cat, v_cat, SEQ)
            o_ref[:, t * LANES:(t + 1) * LANES] = o.astype(BF16)


def _ctx_attn(proj, qg, kg, kcache, vcache, layer):
    nq = ATTN_CH // LANES
    return pl.pallas_call(
        _ctx_attn_body,
        grid=(BATCH,),
        in_specs=[pl.BlockSpec((SEQ, ATTN_CH), lambda b: (b, COL_Q // ATTN_CH)),
                  pl.BlockSpec((SEQ, KV_CH), lambda b: (b, COL_K // KV_CH)),
                  pl.BlockSpec((SEQ, KV_CH), lambda b: (b, COL_V // KV_CH)),
                  pl.BlockSpec((1, LANES), lambda b: (0, 0)),
                  pl.BlockSpec((1, LANES), lambda b: (0, 0)),
                  pl.BlockSpec(memory_space=pl.ANY),
                  pl.BlockSpec(memory_space=pl.ANY)],
        out_specs=[pl.BlockSpec((SEQ, ATTN_CH), lambda b: (b, 0)),
                   pl.BlockSpec((None, None, SEQ, KV_CH), lambda b: (b, layer, 0, 0)),
                   pl.BlockSpec((None, None, SEQ, KV_CH), lambda b: (b, layer, 0, 0))],
        out_shape=[jax.ShapeDtypeStruct((T_CTX, ATTN_CH), BF16),
                   jax.ShapeDtypeStruct((BATCH, DEPTH, SEQ, KV_CH), F32),
                   jax.ShapeDtypeStruct((BATCH, DEPTH, SEQ, KV_CH), F32)],
        input_output_aliases={5: 1, 6: 2},
        compiler_params=_cparams(("parallel",), 32),
        name="ctx_attn",
    )(proj, proj, proj, qg, kg, kcache, vcache)


def _lat_attn_body(q_ref, k_ref, v_ref, ck_ref, cv_ref, qg_ref, kg_ref, cosk_ref, sink_ref, cosq_ref, sinq_ref,
                   o_ref, kcat_ref, vcat_ref):
    @pl.when(pl.program_id(1) == 0)
    def _():
        rows = 256
        for c in range(LK // rows):
            if c == 0:
                kn = ck_ref[...]
                v = cv_ref[...]
            else:
                r0 = (c - 1) * rows
                kn = _head_norm(k_ref[r0:r0 + rows, :], kg_ref[...])
                kn = _rope(kn, cosk_ref[r0:r0 + rows, :], sink_ref[r0:r0 + rows, :])
                v = v_ref[r0:r0 + rows, :]
            for g in range(2):
                k_lo, k_hi = _split_heads(kn, g)
                v_lo, v_hi = _split_heads(v, g)
                kcat_ref[g, c * rows:(c + 1) * rows, :] = k_lo.astype(BF16)
                kcat_ref[g, LK + c * rows:LK + (c + 1) * rows, :] = k_hi.astype(BF16)
                vcat_ref[g, c * rows:(c + 1) * rows, :] = v_lo.astype(BF16)
                vcat_ref[g, LK + c * rows:LK + (c + 1) * rows, :] = v_hi.astype(BF16)

    scale = HEAD_DIM ** -0.5
    for t in range(ATTN_CH // LANES):
        g = t // 2
        q = _head_norm(q_ref[:, t * LANES:(t + 1) * LANES], qg_ref[...])
        q = _rope(q, cosq_ref[...], sinq_ref[...]) * scale
        o = _attend_tile(q.astype(BF16), kcat_ref[g], vcat_ref[g], LK)
        o_ref[:, t * LANES:(t + 1) * LANES] = o.astype(BF16)


def _lat_attn(proj, cache_k4, cache_v4, qg, kg, cos, sin, layer):
    nqb = DEC_SEQ // TQ
    row0 = T_CTX // DEC_SEQ
    return pl.pallas_call(
        _lat_attn_body,
        grid=(DEC_BATCH, nqb),
        in_specs=[pl.BlockSpec((TQ, ATTN_CH), lambda b, i: (T_CTX // TQ + b * nqb + i, COL_Q // ATTN_CH)),
                  pl.BlockSpec((DEC_SEQ, KV_CH), lambda b, i: (row0 + b, COL_K // KV_CH)),
                  pl.BlockSpec((DEC_SEQ, KV_CH), lambda b, i: (row0 + b, COL_V // KV_CH)),
                  pl.BlockSpec((None, None, PAST_LEN, KV_CH), lambda b, i: (b, layer, 0, 0)),
                  pl.BlockSpec((None, None, PAST_LEN, KV_CH), lambda b, i: (b, layer, 0, 0)),
                  pl.BlockSpec((1, LANES), lambda b, i: (0, 0)),
                  pl.BlockSpec((1, LANES), lambda b, i: (0, 0)),
                  pl.BlockSpec((DEC_SEQ, LANES), lambda b, i: (0, 0)),
                  pl.BlockSpec((DEC_SEQ, LANES), lambda b, i: (0, 0)),
                  pl.BlockSpec((TQ, LANES), lambda b, i: (i, 0)),
                  pl.BlockSpec((TQ, LANES), lambda b, i: (i, 0))],
        out_specs=pl.BlockSpec((TQ, ATTN_CH), lambda b, i: (b * nqb + i, 0)),
        out_shape=jax.ShapeDtypeStruct((T_LAT, ATTN_CH), BF16),
        scratch_shapes=[pltpu.VMEM((2, 2 * LK, LANES), BF16), pltpu.VMEM((2, 2 * LK, LANES), BF16)],
        compiler_params=_cparams(("parallel", "arbitrary"), 56),
        name="lat_attn",
    )(proj, proj, proj, cache_k4, cache_v4, qg, kg, cos, sin, cos, sin)


def _conv_body(L, a_ref, g_ref, w_ref, b_ref, lg_ref, lb_ref, o_ref, pad_ref, y_ref):
    rows = CONV_ROWS
    zeros = jnp.zeros((PAD, CONV_CH), F32)
    pad_ref[0:PAD, :] = zeros
    pad_ref[PAD + L:2 * PAD + L, :] = zeros
    pad_ref[PAD:PAD + L, :] = a_ref[...] * jax.nn.sigmoid(g_ref[...])
    shift0 = PAD - CONV_WIDTH // 2

    def chunk(c, carry):
        base = pl.multiple_of(c * rows, rows)
        for lg in range(CONV_CH // LANES):
            lanes = slice(lg * LANES, (lg + 1) * LANES)
            y = jnp.zeros((rows, LANES), F32)
            for r in range(SUBLANES):
                z = None
                for q in range((CONV_WIDTH + shift0) // SUBLANES + 1):
                    j = SUBLANES * q + r - shift0
                    if 0 <= j < CONV_WIDTH:
                        term = w_ref[j:j + 1, lanes] * pad_ref[pl.ds(base + SUBLANES * q, rows + SUBLANES), lanes]
                        z = term if z is None else z + term
                y = y + z[r:r + rows, :]
            y_ref[:, lanes] = y + b_ref[:, lanes]
        half = rows // 2
        for p in range(2):
            y = y_ref[p * half:(p + 1) * half, :]
            mu = jnp.mean(y, axis=-1, keepdims=True)
            yc = y - mu
            var = jnp.mean(yc * yc, axis=-1, keepdims=True)
            yn = yc * lax.rsqrt(var + EPS) * lg_ref[...] + lb_ref[...]
            o_ref[pl.ds(base + p * half, half), :] = _silu(yn).astype(BF16)
        return carry

    lax.fori_loop(0, L // rows, chunk, 0)


def _conv(proj, w32, b, lg, lb, L, nseq, row0):
    vec = pl.BlockSpec((1, CONV_CH), lambda s: (0, 0))
    return pl.pallas_call(
        functools.partial(_conv_body, L),
        grid=(nseq,),
        in_specs=[pl.BlockSpec((L, CONV_CH), lambda s: (row0 + s, COL_CA // CONV_CH)),
                  pl.BlockSpec((L, CONV_CH), lambda s: (row0 + s, COL_CG // CONV_CH)),
                  pl.BlockSpec((32, CONV_CH), lambda s: (0, 0)), vec, vec, vec],
        out_specs=pl.BlockSpec((L, CONV_CH), lambda s: (s, 0)),
        out_shape=jax.ShapeDtypeStruct((nseq * L, CONV_CH), BF16),
        scratch_shapes=[pltpu.VMEM((L + 2 * PAD, CONV_CH), F32), pltpu.VMEM((CONV_ROWS, CONV_CH), F32)],
        compiler_params=_cparams(("parallel",), 48),
        name="conv_L%d" % L,
    )(proj, proj, w32, b, lg, lb)


def _pool_body(L, u_ref, w_ref, sc_ref, o_ref, pad_ref):
    rows = 256
    zeros = jnp.zeros((PAD, 512), F32)
    pad_ref[0:PAD, :] = zeros
    pad_ref[PAD + L:2 * PAD + L, :] = zeros
    pad_ref[PAD:PAD + L, :] = u_ref[...]
    for c in range(L // rows):
        r0 = c * rows
        t = lax.broadcasted_iota(jnp.int32, (rows, 1), 0) + r0
        for g, w in enumerate(POOL_WINDOWS):
            cols = slice(g * POOL_GROUP_CH, (g + 1) * POOL_GROUP_CH)
            acc = jnp.zeros((rows, POOL_GROUP_CH), F32)
            for i in range(-(w // 2), w - w // 2):
                acc = acc + pad_ref[PAD + r0 + i:PAD + r0 + i + rows, cols]
            lo = jnp.maximum(t - w // 2, 0)
            hi = jnp.minimum(t + (w - w // 2), L)
            cnt = (hi - lo).astype(F32)
            pooled = acc / cnt - pad_ref[PAD + r0:PAD + r0 + rows, cols]
            mixed = jnp.dot(pooled.astype(BF16), w_ref[g], preferred_element_type=F32)
            o_ref[r0:r0 + rows, cols] = (mixed * sc_ref[:, cols]).astype(BF16)


def _pool(proj, pw_bf, pscale, L, nseq, row0):
    return pl.pallas_call(
        functools.partial(_pool_body, L),
        grid=(nseq,),
        in_specs=[pl.BlockSpec((L, 512), lambda s: (row0 + s, COL_POOL // 512)),
                  pl.BlockSpec((4, POOL_GROUP_CH, POOL_GROUP_CH), lambda s: (0, 0, 0)),
                  pl.BlockSpec((1, 512), lambda s: (0, 0))],
        out_specs=pl.BlockSpec((L, 512), lambda s: (s, 0)),
        out_shape=jax.ShapeDtypeStruct((nseq * L, 512), BF16),
        scratch_shapes=[pltpu.VMEM((L + 2 * PAD, 512), F32)],
        compiler_params=_cparams(("parallel",), 48),
        name="pool_L%d" % L,
    )(proj, pw_bf, pscale)


def _channel_dft(u, wc_ref, z_ref, r0, L):
    rows = u.shape[0]
    for h in range(FOURIER_HEADS):
        cols = slice(h * FOURIER_HEAD_CH, (h + 1) * FOURIER_HEAD_CH)
        a = jnp.dot(u[:, cols].astype(BF16), wc_ref[...], preferred_element_type=F32)
        z_ref[r0:r0 + rows, cols] = a[:, :FOURIER_HEAD_CH].astype(BF16)
        z_ref[L + r0:L + r0 + rows, cols] = a[:, FOURIER_HEAD_CH:].astype(BF16)


def _four_body(L, tr, u_ref, wc_ref, wl_ref, o_ref, z_ref):
    @pl.when(pl.program_id(1) == 0)
    def _():
        rows = min(L, 512)
        for c in range(L // rows):
            _channel_dft(u_ref[c * rows:(c + 1) * rows, :], wc_ref, z_ref, c * rows, L)

    scale = (L * FOURIER_HEAD_CH) ** -0.5
    o_ref[...] = (jnp.dot(wl_ref[...], z_ref[...], preferred_element_type=F32) * scale).astype(BF16)


def _four(proj, wc_bf, wl_bf, L, nseq, row0):
    tr = min(L, 256)
    return pl.pallas_call(
        functools.partial(_four_body, L, tr),
        grid=(nseq, L // tr),
        in_specs=[pl.BlockSpec((L, 512), lambda s, r: (row0 + s, COL_FOUR // 512)),
                  pl.BlockSpec((FOURIER_HEAD_CH, 2 * FOURIER_HEAD_CH), lambda s, r: (0, 0)),
                  pl.BlockSpec((tr, 2 * L), lambda s, r: (r, 0))],
        out_specs=pl.BlockSpec((tr, 512), lambda s, r: (s * (L // tr) + r, 0)),
        out_shape=jax.ShapeDtypeStruct((nseq * L, 512), BF16),
        scratch_shapes=[pltpu.VMEM((2 * L, 512), BF16)],
        compiler_params=_cparams(("parallel", "arbitrary"), 48),
        name="four_L%d" % L,
    )(proj, wc_bf, wl_bf)


def _route(logits):
    lane = lax.broadcasted_iota(jnp.int32, logits.shape, 1).astype(F32)
    neg = jnp.float32(-jnp.inf)
    big = jnp.float32(1 << 20)
    is_g = (lane >= N_EXPERTS) & (lane < N_EXPERTS + N_EXPERT_GROUPS)
    gl = jnp.where(is_g, logits, neg)
    gmax = jnp.max(gl, axis=-1, keepdims=True)
    g_idx = jnp.min(jnp.where(gl == gmax, lane, big), axis=-1, keepdims=True) - N_EXPERTS
    denom = jnp.sum(jnp.where(is_g, jnp.exp(gl - gmax), 0.0), axis=-1, keepdims=True)
    g_w = 1.0 / denom
    in_grp = (lane >= g_idx * EXPERTS_PER_GROUP) & (lane < (g_idx + 1) * EXPERTS_PER_GROUP)
    el = jnp.where(in_grp, logits, neg)
    v1 = jnp.max(el, axis=-1, keepdims=True)
    i1 = jnp.min(jnp.where(el == v1, lane, big), axis=-1, keepdims=True)
    el2 = jnp.where(lane == i1, neg, el)
    v2 = jnp.max(el2, axis=-1, keepdims=True)
    i2 = jnp.min(jnp.where(el2 == v2, lane, big), axis=-1, keepdims=True)
    e2 = jnp.exp(v2 - v1)
    p1 = 1.0 / (1.0 + e2)
    p2 = e2 / (1.0 + e2)
    gates = g_w * jnp.where(lane == i1, p1, jnp.where(lane == i2, p2, 0.0))
    return gates + jnp.where(lane == g_idx + N_EXPERTS, 1.0, 0.0)


def _pair_specs(width):
    n = T_CTX // TM
    return [pl.BlockSpec((TM, width), lambda i: (jnp.minimum(i, n - 1), 0)),
            pl.BlockSpec((TM, width), lambda i: (jnp.maximum(i - n, 0), 0))]


def _outproj_body(nx, *refs):
    mixers = refs[nx:nx + 8]
    (w_ref, g1_ref, gain_ref, sc_ref, sh_ref, wr_ref, br_ref, xo_ref, hg_ref, route_ref) = refs[nx + 8:]
    mix = None
    for k in range(4):
        part = jnp.dot(_load_x(mixers[2 * k:2 * k + 2]), w_ref[512 * k:512 * (k + 1), :],
                       preferred_element_type=F32)
        mix = part if mix is None else mix + part
    x = _load_x(refs[:nx]) + g1_ref[...] * mix
    xo_ref[...] = x
    ms = jnp.mean(x * x, axis=-1, keepdims=True)
    h = x * lax.rsqrt(ms + EPS) * gain_ref[...]
    h = h * (1.0 + sc_ref[...]) + sh_ref[...]
    h_hi = h.astype(BF16)
    h_lo = (h - h_hi.astype(F32)).astype(BF16)
    t = (jnp.dot(h_hi, wr_ref[...], preferred_element_type=F32)
         + jnp.dot(h_lo, wr_ref[...], preferred_element_type=F32))
    logits = t + pltpu.roll(t, LANES - ROUTER_LO_LANE, axis=1) + br_ref[...]
    route = _route(logits)
    route_ref[...] = route
    for s in range(H_ROWS):
        hg_ref[pl.ds(s, TM, stride=REC), :] = h[:, s * LANES:(s + 1) * LANES]
    hg_ref[pl.ds(H_ROWS, TM, stride=REC), :] = route
    for s in range(H_ROWS + 1, REC):
        hg_ref[pl.ds(s, TM, stride=REC), :] = jnp.zeros((TM, LANES), F32)


def _outproj(mixers, xs, w_out_bf, mod4, gain, wr, br):
    tile = lambda w: pl.BlockSpec((TM, w), lambda i: (i, 0))
    const = lambda r, c: pl.BlockSpec((r, c), lambda i: (0, 0))
    mix_specs = []
    for _ in range(4):
        mix_specs += _pair_specs(512)
    return pl.pallas_call(
        functools.partial(_outproj_body, len(xs)),
        grid=(T_ALL // TM,),
        in_specs=_x_specs(len(xs)) + mix_specs + [const(2048, D),
                                                  _mod_spec(2, TM), const(1, D), _mod_spec(4, TM), _mod_spec(3, TM),
                                                  const(D, LANES), const(1, LANES)],
        out_specs=[tile(D), pl.BlockSpec((TM * REC, LANES), lambda i: (i, 0)), tile(LANES)],
        out_shape=[jax.ShapeDtypeStruct((T_ALL, D), F32),
                   jax.ShapeDtypeStruct((T_ALL * REC, LANES), F32),
                   jax.ShapeDtypeStruct((T_ALL, LANES), F32)],
        compiler_params=_cparams(("parallel",), 56),
        name="out_proj",
    )(*xs, *[a for pair in mixers for a in pair], w_out_bf, mod4, gain, mod4, mod4, wr, br)


def _rank_body(r_ref, o_ref, cnt_ref, carry_ref):
    @pl.when(pl.program_id(0) == 0)
    def _():
        carry_ref[...] = jnp.zeros_like(carry_ref)

    r = r_ref[...]
    lane = lax.broadcasted_iota(jnp.int32, r.shape, 1)
    is_g = (lane >= N_EXPERTS) & (lane < N_EXPERTS + N_EXPERT_GROUPS)
    onehot = jnp.where(is_g, r, 0.0)
    row = lax.broadcasted_iota(jnp.int32, (RT, RT), 0)
    col = lax.broadcasted_iota(jnp.int32, (RT, RT), 1)
    earlier = jnp.where(col < row, 1.0, 0.0).astype(BF16)
    before = jnp.dot(earlier, onehot.astype(BF16), preferred_element_type=F32) + carry_ref[...]
    rank = jnp.sum(onehot * before, axis=-1, keepdims=True)
    grp = jnp.sum(onehot * (lane - N_EXPERTS).astype(F32), axis=-1, keepdims=True)
    packed = jnp.where(lane == 0, rank, jnp.where(lane == 1, grp, 0.0))
    o_ref[...] = packed.T[0:SUBLANES, :].astype(jnp.int32)
    carry_ref[...] += jnp.sum(onehot, axis=0, keepdims=True)
    cnt_ref[...] = carry_ref[...]


def _rank(route):
    return pl.pallas_call(
        _rank_body,
        grid=(T_ALL // RT,),
        in_specs=[pl.BlockSpec((RT, LANES), lambda i: (i, 0))],
        out_specs=[pl.BlockSpec((SUBLANES, RT), lambda i: (i, 0)),
                   pl.BlockSpec((1, LANES), lambda i: (0, 0))],
        out_shape=[jax.ShapeDtypeStruct((T_ALL // RT * SUBLANES, RT), jnp.int32),
                   jax.ShapeDtypeStruct((1, LANES), F32)],
        scratch_shapes=[pltpu.VMEM((1, LANES), F32)],
        compiler_params=_cparams(("arbitrary",), 32),
        name="moe_rank",
    )(route)


def _wait_rows(copy, n):
    def body(t, c):
        copy.wait()
        return c
    lax.fori_loop(0, n, body, 0, unroll=8)


def _record(ref, idx, rows):
    start = idx * rows if isinstance(idx, int) else pl.multiple_of(idx * rows, rows)
    return ref.at[pl.ds(start, rows)]


def _dispatch_body(slot_sm, fstart_sm, flen_sm, hg_ref, xs_hbm, zrec_ref, sem):
    i = pl.program_id(0)
    base = i * DT

    @pl.when(i == 0)
    def _():
        zrec_ref[...] = jnp.zeros_like(zrec_ref)
        for g in range(N_EXPERT_GROUPS):
            def fill(r, c):
                pltpu.make_async_copy(zrec_ref, _record(xs_hbm, fstart_sm[g] + r, REC), sem.at[1]).start()
                return c
            lax.fori_loop(0, flen_sm[g], fill, 0)
        _wait_rows(pltpu.make_async_copy(zrec_ref, _record(xs_hbm, 0, REC), sem.at[1]), NSLOT - T_ALL)

    def issue(t, c):
        pltpu.make_async_copy(_record(hg_ref, t, REC), _record(xs_hbm, slot_sm[base + t], REC), sem.at[0]).start()
        return c
    lax.fori_loop(0, DT, issue, 0, unroll=8)
    _wait_rows(pltpu.make_async_copy(_record(hg_ref, 0, REC), _record(xs_hbm, 0, REC), sem.at[0]), DT)


def _dispatch(slot, fstart, flen, hg):
    return pl.pallas_call(
        _dispatch_body,
        grid_spec=pltpu.PrefetchScalarGridSpec(
            num_scalar_prefetch=3,
            grid=(T_ALL // DT,),
            in_specs=[pl.BlockSpec((DT * REC, LANES), lambda i, s, f0, f1: (i, 0))],
            out_specs=pl.BlockSpec(memory_space=pl.ANY),
            scratch_shapes=[pltpu.VMEM((REC, LANES), F32), pltpu.SemaphoreType.DMA((2,))]),
        out_shape=jax.ShapeDtypeStruct((NSLOT * REC, LANES), F32),
        compiler_params=_cparams(("arbitrary",), 48),
        name="moe_dispatch",
    )(slot, fstart, flen, hg)


def _ffn_body(tgrp_sm, xs_ref, wg_ref, wu_ref, wd_ref, ys_ref, h_ref):
    grp = tgrp_sm[pl.program_id(0)]
    for s in range(H_ROWS):
        h_ref[:, s * LANES:(s + 1) * LANES] = xs_ref[pl.ds(s, TS, stride=REC), :].astype(BF16)
    gates = xs_ref[pl.ds(H_ROWS, TS, stride=REC), :]
    h = h_ref[...]
    a = jnp.dot(h, wg_ref[...], preferred_element_type=F32)
    b = jnp.dot(h, wu_ref[...], preferred_element_type=F32)
    hid = _silu(a) * b
    lane = lax.broadcasted_iota(jnp.int32, gates.shape, 1)
    parts = []
    for e in range(EXPERTS_PER_GROUP):
        ge = jnp.sum(jnp.where(lane == grp * EXPERTS_PER_GROUP + e, gates, 0.0), axis=-1, keepdims=True)
        parts.append((hid[:, e * EXPERT_FF:(e + 1) * EXPERT_FF] * ge).astype(BF16))
    y = jnp.dot(jnp.concatenate(parts, axis=1), wd_ref[...], preferred_element_type=F32)
    for s in range(H_ROWS):
        ys_ref[pl.ds(s, TS, stride=H_ROWS), :] = y[:, s * LANES:(s + 1) * LANES]


def _ffn(tile_grp, xs, wg_grp, wu_grp, wd_grp):
    ffw = EXPERTS_PER_GROUP * EXPERT_FF
    return pl.pallas_call(
        _ffn_body,
        grid_spec=pltpu.PrefetchScalarGridSpec(
            num_scalar_prefetch=1,
            grid=(NT,),
            in_specs=[pl.BlockSpec((TS * REC, LANES), lambda j, tg: (j, 0)),
                      pl.BlockSpec((None, D, ffw), lambda j, tg: (tg[j], 0, 0)),
                      pl.BlockSpec((None, D, ffw), lambda j, tg: (tg[j], 0, 0)),
                      pl.BlockSpec((None, ffw, D), lambda j, tg: (tg[j], 0, 0))],
            out_specs=pl.BlockSpec((TS * H_ROWS, LANES), lambda j, tg: (j, 0)),
            scratch_shapes=[pltpu.VMEM((TS, D), BF16)]),
        out_shape=jax.ShapeDtypeStruct((NSLOT * H_ROWS, LANES), F32),
        compiler_params=_cparams(("arbitrary",), 48),
        name="moe_ffn",
    )(tile_grp, xs, wg_grp, wu_grp, wd_grp)


def _combine_body(final, slot_sm, ys_hbm, x_ref, g2_ref, fg_ref, *rest):
    outs, buf, sem = rest[:-2], rest[-2], rest[-1]
    i = pl.program_id(0)
    n = pl.num_programs(0)

    def issue(step, b):
        def body(t, c):
            pltpu.make_async_copy(_record(ys_hbm, slot_sm[step * TM + t], H_ROWS),
                                  _record(buf.at[b], t, H_ROWS), sem.at[b]).start()
            return c
        lax.fori_loop(0, TM, body, 0, unroll=8)

    @pl.when(i == 0)
    def _():
        issue(0, 0)

    @pl.when(i + 1 < n)
    def _():
        issue(i + 1, (i + 1) % 2)

    b = i % 2
    _wait_rows(pltpu.make_async_copy(_record(ys_hbm, 0, H_ROWS), _record(buf.at[b], 0, H_ROWS), sem.at[b]), TM)
    y = jnp.concatenate([buf[b, pl.ds(s, TM, stride=H_ROWS), :] for s in range(H_ROWS)], axis=1)
    x = x_ref[...] + g2_ref[...] * y
    if not final:
        outs[0][...] = x
    else:
        ms = jnp.mean(x * x, axis=-1, keepdims=True)
        yn = x * lax.rsqrt(ms + EPS) * fg_ref[...]
        is_ctx = i < T_CTX // TM

        @pl.when(is_ctx)
        def _():
            outs[0][...] = yn

        @pl.when(jnp.logical_not(is_ctx))
        def _():
            outs[1][...] = yn


def _combine(slot, ys, x, mod4, final_gain, final):
    if final:
        nc = T_CTX // TM
        out_specs = [pl.BlockSpec((TM, D), lambda i, s: (jnp.minimum(i, nc - 1), 0)),
                     pl.BlockSpec((TM, D), lambda i, s: (jnp.maximum(i - nc, 0), 0))]
        out_shape = [jax.ShapeDtypeStruct((T_CTX, D), F32), jax.ShapeDtypeStruct((T_LAT, D), F32)]
    else:
        out_specs = pl.BlockSpec((TM, D), lambda i, s: (i, 0))
        out_shape = jax.ShapeDtypeStruct((T_ALL, D), F32)
    return pl.pallas_call(
        functools.partial(_combine_body, final),
        grid_spec=pltpu.PrefetchScalarGridSpec(
            num_scalar_prefetch=1,
            grid=(T_ALL // TM,),
            in_specs=[pl.BlockSpec(memory_space=pl.ANY),
                      pl.BlockSpec((TM, D), lambda i, s: (i, 0)),
                      pl.BlockSpec((None, None, 1, D), lambda i, s: (_mod_row(i, TM), 5, 0, 0)),
                      pl.BlockSpec((1, D), lambda i, s: (0, 0))],
            out_specs=out_specs,
            scratch_shapes=[pltpu.VMEM((2, TM * H_ROWS, LANES), F32), pltpu.SemaphoreType.DMA((2,))]),
        out_shape=out_shape,
        compiler_params=_cparams(("arbitrary",), 48),
        name="moe_combine",
    )(slot, ys, x, mod4, final_gain)


def _moe_plan(rk, cnt):
    rk = rk.reshape(T_ALL // RT, SUBLANES, RT)
    rank = rk[:, 0, :].reshape(T_ALL)
    grp = rk[:, 1, :].reshape(T_ALL)
    cnt = cnt[0, N_EXPERTS:N_EXPERTS + N_EXPERT_GROUPS].astype(jnp.int32)
    padded = (cnt + TS - 1) // TS * TS
    off = jnp.cumsum(padded) - padded
    padded = padded.at[N_EXPERT_GROUPS - 1].set(NSLOT - off[N_EXPERT_GROUPS - 1])
    slot = rank
    for g in range(1, N_EXPERT_GROUPS):
        slot = slot + jnp.where(grp == g, off[g], 0)
    ends = (off + padded)[:N_EXPERT_GROUPS - 1]
    tile_grp = jnp.sum((jnp.arange(NT, dtype=jnp.int32)[:, None] * TS >= ends[None, :]).astype(jnp.int32), axis=1)
    return slot, tile_grp, off + cnt, padded - cnt


def _rope_tables():
    t = np.arange(DEC_SEQ)
    pos = np.stack([t // GRID_W, t % GRID_W], axis=1).astype(np.float64)
    inv = ROPE_THETA ** (-np.arange(16, dtype=np.float64) / 16.0)
    d = np.arange(LANES) % HEAD_DIM
    pair = d // 2
    ang = pos[:, (pair >= 16).astype(np.int64)] * inv[pair % 16][None, :]
    sign = np.where(d % 2 == 0, -1.0, 1.0)
    return np.cos(ang).astype(np.float32), (np.sin(ang) * sign[None, :]).astype(np.float32)


def _dft_pos(L):
    j = np.arange(L)
    ang = 2.0 * np.pi * ((j[:, None] * j[None, :]) % L) / L
    return np.concatenate([np.cos(ang), -np.sin(ang)], axis=1).astype(np.float32)


def _dft_ch():
    c = np.arange(FOURIER_HEAD_CH)
    ang = 2.0 * np.pi * ((c[:, None] * c[None, :]) % FOURIER_HEAD_CH) / FOURIER_HEAD_CH
    return np.concatenate([np.cos(ang), np.sin(ang)], axis=1).astype(np.float32)


def kernel(x_prompt, x_sample, cache_k, cache_v, c, c_ctx, w_ada, b_ada, norm_mix, norm_ffn, w_in, w_out, q_norm, k_norm, conv_w, conv_b, conv_ln_g, conv_ln_b, pool_w, pool_scale, router_group_w, router_group_b, router_expert_w, router_expert_b, moe_w_gate, moe_w_up, moe_w_down, final_norm):
    xs = (x_prompt.reshape(T_CTX, D), x_sample.reshape(T_LAT, D))
    cond8 =jnp.concatenate([c_ctx[None, :], c, jnp.zeros((3, D), F32)], axis=0)
    mod = _ada(cond8, w_ada, b_ada)

    o1, o2, o3 = ATTN_CH, ATTN_CH + KV_CH, ATTN_CH + 2 * KV_CH
    w_in_p = jnp.concatenate([w_in[:, :, :o1], w_in[:, :, o3:], w_in[:, :, o1:o3]], axis=-1).astype(BF16)
    w_out_bf = w_out.astype(BF16)
    ffw = EXPERTS_PER_GROUP * EXPERT_FF
    by_group = lambda w: (w.reshape(DEPTH, N_EXPERT_GROUPS, EXPERTS_PER_GROUP, D, EXPERT_FF)
                          .transpose(0, 1, 3, 2, 4).reshape(DEPTH, N_EXPERT_GROUPS, D, ffw).astype(BF16))
    wg_grp, wu_grp = by_group(moe_w_gate), by_group(moe_w_up)
    wd_grp = moe_w_down.reshape(DEPTH, N_EXPERT_GROUPS, ffw, D).astype(BF16)
    pw_bf = pool_w.astype(BF16)
    n_route = N_EXPERTS + N_EXPERT_GROUPS
    wr = jnp.concatenate([router_expert_w, router_group_w], axis=-1)
    wr_hi = wr.astype(BF16)
    wr_lo = (wr - wr_hi.astype(F32)).astype(BF16)
    w_router = jnp.concatenate([wr_hi, jnp.zeros((DEPTH, D, ROUTER_LO_LANE - n_route), BF16), wr_lo,
                                jnp.zeros((DEPTH, D, LANES - ROUTER_LO_LANE - n_route), BF16)], axis=-1)
    b_router = jnp.concatenate([router_expert_b, router_group_b,
                                jnp.zeros((DEPTH, LANES - N_EXPERTS - N_EXPERT_GROUPS), F32)], axis=-1)
    conv_w32 = jnp.concatenate([conv_w, jnp.zeros((DEPTH, 1, CONV_CH), F32)], axis=1)

    cos_np, sin_np = _rope_tables()
    cos, sin = jnp.asarray(cos_np), jnp.asarray(sin_np)
    wc_bf = jnp.asarray(_dft_ch()).astype(BF16)
    wl_ctx = jnp.asarray(_dft_pos(SEQ)).astype(BF16)
    wl_lat = jnp.asarray(_dft_pos(DEC_SEQ)).astype(BF16)

    cache_k4 = cache_k.reshape(DEC_BATCH, DEPTH, PAST_LEN, KV_CH)
    cache_v4 = cache_v.reshape(DEC_BATCH, DEPTH, PAST_LEN, KV_CH)
    new_k = jnp.zeros((BATCH, DEPTH, SEQ, KV_CH), F32)
    new_v = jnp.zeros((BATCH, DEPTH, SEQ, KV_CH), F32)
    fgain = final_norm.reshape(1, D)

    for l in range(DEPTH):
        mod4 = mod[l].reshape(8, N_MOD, 1, D)
        proj = _inproj(xs, norm_mix[l].reshape(1, D), mod4, w_in_p[l])
        qg = jnp.tile(q_norm[l], 2).reshape(1, LANES)
        kg = jnp.tile(k_norm[l], 2).reshape(1, LANES)
        attn_c, new_k, new_v = _ctx_attn(proj, qg, kg, new_k, new_v, l)
        attn_l = _lat_attn(proj, cache_k4, cache_v4, qg, kg, cos, sin, l)
        cb, lg, lb = conv_b[l].reshape(1, -1), conv_ln_g[l].reshape(1, -1), conv_ln_b[l].reshape(1, -1)
        ps = pool_scale[l].reshape(1, -1)
        lat0 = T_CTX // DEC_SEQ
        mixers = [(attn_c, attn_l),
                  (_conv(proj, conv_w32[l], cb, lg, lb, SEQ, BATCH, 0),
                   _conv(proj, conv_w32[l], cb, lg, lb, DEC_SEQ, DEC_BATCH, lat0)),
                  (_pool(proj, pw_bf[l], ps, SEQ, BATCH, 0),
                   _pool(proj, pw_bf[l], ps, DEC_SEQ, DEC_BATCH, lat0)),
                  (_four(proj, wc_bf, wl_ctx, SEQ, BATCH, 0),
                   _four(proj, wc_bf, wl_lat, DEC_SEQ, DEC_BATCH, lat0))]
        x, hg, route = _outproj(mixers, xs, w_out_bf[l], mod4, norm_ffn[l].reshape(1, D),
                                w_router[l], b_router[l].reshape(1, LANES))
        slot, tile_grp, fstart, flen = _moe_plan(*_rank(route))
        xslots = _dispatch(slot, fstart, flen, hg)
        yslots = _ffn(tile_grp, xslots, wg_grp[l], wu_grp[l], wd_grp[l])
        out = _combine(slot, yslots, x, mod4, fgain, l == DEPTH - 1)
        xs = (out,)

    y_prompt = out[0].reshape(BATCH, SEQ, D)
    y_sample = out[1].reshape(DEC_BATCH, DEC_SEQ, D)
    new_k = new_k.reshape(BATCH, DEPTH, SEQ, 2, HEAD_DIM)
    new_v = new_v.reshape(BATCH, DEPTH, SEQ, 2, HEAD_DIM)
    return (y_prompt, y_sample, new_k, new_v)
```

```python
import functools

import numpy as np
import jax
import jax.numpy as jnp
from jax import lax
from jax.experimental import pallas as pl
from jax.experimental.pallas import tpu as pltpu

F32 = jnp.float32
BF16 = jnp.bfloat16

D = 1024
BATCH = 32
SEQ = 256
DEPTH = 2
DEC_BATCH = 4
DEC_SEQ = 2048
PAST_LEN = 256
GRID_W = 64
HEAD_DIM = 64
ATTN_CH = 512
KV_CH = 128
CONV_CH = 512
CONV_WIDTH = 31
POOL_WINDOWS = (2, 4, 8, 16)
POOL_GROUP_CH = 128
FOURIER_HEAD_CH = 128
FOURIER_HEADS = 4
N_EXPERT_GROUPS = 4
EXPERTS_PER_GROUP = 4
N_EXPERTS = 16
EXPERT_FF = 256
ROPE_THETA = 10000.0
EPS = 1e-6
N_MOD = 6

T_CTX = BATCH * SEQ
T_LAT = DEC_BATCH * DEC_SEQ
T_ALL = T_CTX + T_LAT
LANES = 128
SUBLANES = 8
PAD = 16
CONV_ROWS = 128
ROUTER_LO_LANE = 32
IN_COLS = 2816
COL_Q, COL_CA, COL_CG, COL_POOL, COL_FOUR, COL_K, COL_V = 0, 512, 1024, 1536, 2048, 2560, 2688

TM = 512
OUT_ROWS = 256
RT = 512
DT = 1024
TS = 512
NT = T_ALL // TS + N_EXPERT_GROUPS
NSLOT = NT * TS
H_ROWS = D // LANES
REC = 2 * SUBLANES
TQ = 256
LK = PAST_LEN + DEC_SEQ


def _cparams(sem, vmem_mb):
    return pltpu.CompilerParams(dimension_semantics=sem, vmem_limit_bytes=vmem_mb * 1024 * 1024)


def _mod_row(i, tm):
    nctx = T_CTX // tm
    per = DEC_SEQ // tm
    return jnp.where(i < nctx, 0, 1 + (i - nctx) // per)


def _mod_spec(k, tm):
    return pl.BlockSpec((None, None, 1, D), lambda i: (_mod_row(i, tm), k, 0, 0))


def _silu(x):
    return x * jax.nn.sigmoid(x)


def _ada_body(c_ref, w_ref, b_ref, o_ref):
    s = _silu(c_ref[...])
    o_ref[...] = jnp.dot(s.astype(BF16), w_ref[...].astype(BF16), preferred_element_type=F32) + b_ref[...]


def _ada(cond8, w_ada, b_ada):
    tn = 1536
    n = N_MOD * D
    return pl.pallas_call(
        _ada_body,
        grid=(DEPTH, n // tn),
        in_specs=[pl.BlockSpec((8, D), lambda l, j: (0, 0)),
                  pl.BlockSpec((None, D, tn), lambda l, j: (l, 0, j)),
                  pl.BlockSpec((None, 1, tn), lambda l, j: (l, 0, j))],
        out_specs=pl.BlockSpec((None, 8, tn), lambda l, j: (l, 0, j)),
        out_shape=jax.ShapeDtypeStruct((DEPTH, 8, n), F32),
        compiler_params=_cparams(("arbitrary", "arbitrary"), 40),
        name="ada_mod",
    )(cond8, w_ada, b_ada.reshape(DEPTH, 1, n))


def _x_specs(nx):
    if nx == 1:
        return [pl.BlockSpec((TM, D), lambda i: (i, 0))]
    n = T_CTX // TM
    return [pl.BlockSpec((TM, D), lambda i: (jnp.minimum(i, n - 1), 0)),
            pl.BlockSpec((TM, D), lambda i: (jnp.maximum(i - n, 0), 0))]


def _load_x(x_refs, rows=slice(None)):
    if len(x_refs) == 1:
        return x_refs[0][rows, :]
    return jnp.where(pl.program_id(0) < T_CTX // TM, x_refs[0][rows, :], x_refs[1][rows, :])


def _resident(shape):
    return pl.BlockSpec(shape, lambda *_: (0,) * len(shape), pipeline_mode=pl.Buffered(1))


def _inproj_body(nx, *refs):
    g_ref, sc_ref, sh_ref, w_ref, o_ref, wb_ref = refs[nx:]

    @pl.when(pl.program_id(0) == 0)
    def _():
        kv0, kv1 = ATTN_CH, ATTN_CH + 2 * KV_CH
        wb_ref[:, 0:kv0] = w_ref[:, 0:kv0].astype(BF16)
        wb_ref[:, kv0:IN_COLS - 2 * KV_CH] = w_ref[:, kv1:IN_COLS].astype(BF16)
        wb_ref[:, IN_COLS - 2 * KV_CH:IN_COLS] = w_ref[:, kv0:kv1].astype(BF16)

    x = _load_x(refs[:nx])
    ms = jnp.mean(x * x, axis=-1, keepdims=True)
    h = x * lax.rsqrt(ms + EPS) * g_ref[...]
    h = h * (1.0 + sc_ref[...]) + sh_ref[...]
    o_ref[...] = jnp.dot(h.astype(BF16), wb_ref[...], preferred_element_type=F32)


def _inproj(xs, gain, mod4, w_in):
    return pl.pallas_call(
        functools.partial(_inproj_body, len(xs)),
        grid=(T_ALL // TM,),
        in_specs=_x_specs(len(xs)) + [pl.BlockSpec((1, D), lambda i: (0, 0)),
                                      _mod_spec(1, TM), _mod_spec(0, TM),
                                      _resident((D, IN_COLS))],
        out_specs=pl.BlockSpec((TM, IN_COLS), lambda i: (i, 0)),
        out_shape=jax.ShapeDtypeStruct((T_ALL, IN_COLS), F32),
        scratch_shapes=[pltpu.VMEM((D, IN_COLS), BF16)],
        compiler_params=_cparams(("arbitrary",), 56),
        name="in_proj",
    )(*xs, gain, mod4, mod4, w_in)


def _lane_lo():
    return lax.broadcasted_iota(jnp.int32, (1, LANES), 1) < HEAD_DIM


def _head_norm(x, gain):
    lo = _lane_lo()
    x2 = x * x
    s_lo = jnp.sum(jnp.where(lo, x2, 0.0), axis=-1, keepdims=True)
    s_hi = jnp.sum(jnp.where(lo, 0.0, x2), axis=-1, keepdims=True)
    r = jnp.where(lo, lax.rsqrt(s_lo * (1.0 / HEAD_DIM) + EPS), lax.rsqrt(s_hi * (1.0 / HEAD_DIM) + EPS))
    return x * r * gain


def _rope(x, cos, sin_signed):
    even = (lax.broadcasted_iota(jnp.int32, (1, LANES), 1) % 2) == 0
    swapped = jnp.where(even, pltpu.roll(x, LANES - 1, axis=1), pltpu.roll(x, 1, axis=1))
    return x * cos + swapped * sin_signed


def _split_heads(x, g):
    lo = _lane_lo()
    own = jnp.where(lo if g == 0 else jnp.logical_not(lo), x, 0.0)
    other = pltpu.roll(own, HEAD_DIM, axis=1)
    return (own, other) if g == 0 else (other, own)


Q_SCALE = HEAD_DIM ** -0.5 * 1.4426950408889634


def _ones_cols(rows, lk):
    r = lax.broadcasted_iota(jnp.int32, (rows, LANES), 0)
    lane = lax.broadcasted_iota(jnp.int32, (rows, LANES), 1)
    return jnp.where(((lane == 0) & (r < lk)) | ((lane == 1) & (r >= lk)), 1.0, 0.0)


def _scores(q_bf, k_cat):
    return lax.dot_general(q_bf, k_cat, (((1,), (1,)), ((), ())), preferred_element_type=F32)


def _attend_tile(q_bf, k_cat, v_cat, lk):
    return _softmax_pv(_scores(q_bf, k_cat), v_cat, lk)


def _softmax_pv(s, v_cat, lk):
    s_lo, s_hi = s[:, :lk], s[:, lk:]
    p_lo = jnp.exp2(s_lo - jnp.max(s_lo, axis=-1, keepdims=True))
    p_hi = jnp.exp2(s_hi - jnp.max(s_hi, axis=-1, keepdims=True))
    p = jnp.concatenate([p_lo, p_hi], axis=1).astype(BF16)
    o = jnp.dot(p, v_cat, preferred_element_type=F32)
    l_lo, l_hi = o[:, LANES:LANES + 1], o[:, LANES + 1:LANES + 2]
    return o[:, :LANES] * jnp.where(_lane_lo(), 1.0 / l_lo, 1.0 / l_hi)


def _ctx_attn_body(q_ref, k_ref, v_ref, qg_ref, kg_ref, kc_in, vc_in, o_ref, kc_ref, vc_ref):
    del kc_in, vc_in
    kn = _head_norm(k_ref[...], kg_ref[...])
    v = v_ref[...]
    kc_ref[...] = kn
    vc_ref[...] = v
    ones = _ones_cols(2 * SEQ, SEQ)
    for g in range(2):
        k_lo, k_hi = _split_heads(kn, g)
        v_lo, v_hi = _split_heads(v, g)
        k_cat = jnp.concatenate([k_lo, k_hi], axis=0).astype(BF16)
        v_cat = jnp.concatenate([jnp.concatenate([v_lo, v_hi], axis=0), ones], axis=1).astype(BF16)
        for tt in range(2):
            t = 2 * g + tt
            q = _head_norm(q_ref[:, t * LANES:(t + 1) * LANES], qg_ref[...]) * Q_SCALE
            o = _attend_tile(q.astype(BF16), k_cat, v_cat, SEQ)
            o_ref[:, t * LANES:(t + 1) * LANES] = o.astype(BF16)


def _ctx_attn(proj, qg, kg, kcache, vcache, layer):
    nq = ATTN_CH // LANES
    return pl.pallas_call(
        _ctx_attn_body,
        grid=(BATCH,),
        in_specs=[pl.BlockSpec((SEQ, ATTN_CH), lambda b: (b, COL_Q // ATTN_CH)),
                  pl.BlockSpec((SEQ, KV_CH), lambda b: (b, COL_K // KV_CH)),
                  pl.BlockSpec((SEQ, KV_CH), lambda b: (b, COL_V // KV_CH)),
                  pl.BlockSpec((1, LANES), lambda b: (0, 0)),
                  pl.BlockSpec((1, LANES), lambda b: (0, 0)),
                  pl.BlockSpec(memory_space=pl.ANY),
                  pl.BlockSpec(memory_space=pl.ANY)],
        out_specs=[pl.BlockSpec((SEQ, ATTN_CH), lambda b: (b, 0)),
                   pl.BlockSpec((None, None, SEQ, KV_CH), lambda b: (b, layer, 0, 0)),
                   pl.BlockSpec((None, None, SEQ, KV_CH), lambda b: (b, layer, 0, 0))],
        out_shape=[jax.ShapeDtypeStruct((T_CTX, ATTN_CH), BF16),
                   jax.ShapeDtypeStruct((BATCH, DEPTH, SEQ, KV_CH), F32),
                   jax.ShapeDtypeStruct((BATCH, DEPTH, SEQ, KV_CH), F32)],
        input_output_aliases={5: 1, 6: 2},
        compiler_params=_cparams(("parallel",), 32),
        name="ctx_attn",
    )(proj, proj, proj, qg, kg, kcache, vcache)


def _lat_attn_body(q_ref, k_ref, v_ref, ck_ref, cv_ref, qg_ref, kg_ref, cosk_ref, sink_ref, cosq_ref, sinq_ref,
                   o_ref, kcat_ref, vcat_ref):
    @pl.when(pl.program_id(1) == 0)
    def _():
        rows = 256
        for c in range(LK // rows):
            if c == 0:
                kn = ck_ref[...]
                v = cv_ref[...]
            else:
                r0 = (c - 1) * rows
                kn = _head_norm(k_ref[r0:r0 + rows, :], kg_ref[...])
                kn = _rope(kn, cosk_ref[r0:r0 + rows, :], sink_ref[r0:r0 + rows, :])
                v = v_ref[r0:r0 + rows, :]
            for g in range(2):
                k_lo, k_hi = _split_heads(kn, g)
                v_lo, v_hi = _split_heads(v, g)
                kcat_ref[g, c * rows:(c + 1) * rows, :] = k_lo.astype(BF16)
                kcat_ref[g, LK + c * rows:LK + (c + 1) * rows, :] = k_hi.astype(BF16)
                vcat_ref[g, c * rows:(c + 1) * rows, 0:LANES] = v_lo.astype(BF16)
                vcat_ref[g, LK + c * rows:LK + (c + 1) * rows, 0:LANES] = v_hi.astype(BF16)
                vcat_ref[g, c * rows:(c + 1) * rows, LANES:2 * LANES] = _ones_cols(rows, rows).astype(BF16)
                vcat_ref[g, LK + c * rows:LK + (c + 1) * rows, LANES:2 * LANES] = _ones_cols(rows, 0).astype(BF16)

    def scores(t):
        q = _head_norm(q_ref[:, t * LANES:(t + 1) * LANES], qg_ref[...])
        q = _rope(q, cosq_ref[...], sinq_ref[...]) * Q_SCALE
        return _scores(q.astype(BF16), kcat_ref[t // 2])

    nt = ATTN_CH // LANES
    s = scores(0)
    for t in range(nt):
        s_next = scores(t + 1) if t + 1 < nt else None
        o = _softmax_pv(s, vcat_ref[t // 2], LK)
        o_ref[:, t * LANES:(t + 1) * LANES] = o.astype(BF16)
        s = s_next


def _lat_attn(proj, cache_k4, cache_v4, qg, kg, cos, sin, layer):
    nqb = DEC_SEQ // TQ
    row0 = T_CTX // DEC_SEQ
    return pl.pallas_call(
        _lat_attn_body,
        grid=(DEC_BATCH, nqb),
        in_specs=[pl.BlockSpec((TQ, ATTN_CH), lambda b, i: (T_CTX // TQ + b * nqb + i, COL_Q // ATTN_CH)),
                  pl.BlockSpec((DEC_SEQ, KV_CH), lambda b, i: (row0 + b, COL_K // KV_CH)),
                  pl.BlockSpec((DEC_SEQ, KV_CH), lambda b, i: (row0 + b, COL_V // KV_CH)),
                  pl.BlockSpec((None, None, PAST_LEN, KV_CH), lambda b, i: (b, layer, 0, 0)),
                  pl.BlockSpec((None, None, PAST_LEN, KV_CH), lambda b, i: (b, layer, 0, 0)),
                  pl.BlockSpec((1, LANES), lambda b, i: (0, 0)),
                  pl.BlockSpec((1, LANES), lambda b, i: (0, 0)),
                  pl.BlockSpec((DEC_SEQ, LANES), lambda b, i: (0, 0)),
                  pl.BlockSpec((DEC_SEQ, LANES), lambda b, i: (0, 0)),
                  pl.BlockSpec((TQ, LANES), lambda b, i: (i, 0)),
                  pl.BlockSpec((TQ, LANES), lambda b, i: (i, 0))],
        out_specs=pl.BlockSpec((TQ, ATTN_CH), lambda b, i: (b * nqb + i, 0)),
        out_shape=jax.ShapeDtypeStruct((T_LAT, ATTN_CH), BF16),
        scratch_shapes=[pltpu.VMEM((2, 2 * LK, LANES), BF16), pltpu.VMEM((2, 2 * LK, 2 * LANES), BF16)],
        compiler_params=_cparams(("parallel", "arbitrary"), 56),
        name="lat_attn",
    )(proj, proj, proj, cache_k4, cache_v4, qg, kg, cos, sin, cos, sin)


def _conv_body(L, a_ref, g_ref, w_ref, b_ref, lg_ref, lb_ref, o_ref, pad_ref, y_ref):
    rows = CONV_ROWS
    zeros = jnp.zeros((PAD, CONV_CH), F32)
    pad_ref[0:PAD, :] = zeros
    pad_ref[PAD + L:2 * PAD + L, :] = zeros
    pad_ref[PAD:PAD + L, :] = a_ref[...] * jax.nn.sigmoid(g_ref[...])
    shift0 = PAD - CONV_WIDTH // 2

    def chunk(c, carry):
        base = pl.multiple_of(c * rows, rows)
        for lg in range(CONV_CH // LANES):
            lanes = slice(lg * LANES, (lg + 1) * LANES)
            y = jnp.zeros((rows, LANES), F32)
            for r in range(SUBLANES):
                z = None
                for q in range((CONV_WIDTH + shift0) // SUBLANES + 1):
                    j = SUBLANES * q + r - shift0
                    if 0 <= j < CONV_WIDTH:
                        term = w_ref[j:j + 1, lanes] * pad_ref[pl.ds(base + SUBLANES * q, rows + SUBLANES), lanes]
                        z = term if z is None else z + term
                y = y + z[r:r + rows, :]
            y_ref[:, lanes] = y + b_ref[:, lanes]
        half = rows // 2
        for p in range(2):
            y = y_ref[p * half:(p + 1) * half, :]
            mu = jnp.mean(y, axis=-1, keepdims=True)
            yc = y - mu
            var = jnp.mean(yc * yc, axis=-1, keepdims=True)
            yn = yc * lax.rsqrt(var + EPS) * lg_ref[...] + lb_ref[...]
            o_ref[pl.ds(base + p * half, half), :] = _silu(yn).astype(BF16)
        return carry

    lax.fori_loop(0, L // rows, chunk, 0)


def _conv(proj, w32, b, lg, lb, L, nseq, row0):
    vec = pl.BlockSpec((1, CONV_CH), lambda s: (0, 0))
    return pl.pallas_call(
        functools.partial(_conv_body, L),
        grid=(nseq,),
        in_specs=[pl.BlockSpec((L, CONV_CH), lambda s: (row0 + s, COL_CA // CONV_CH)),
                  pl.BlockSpec((L, CONV_CH), lambda s: (row0 + s, COL_CG // CONV_CH)),
                  pl.BlockSpec((32, CONV_CH), lambda s: (0, 0)), vec, vec, vec],
        out_specs=pl.BlockSpec((L, CONV_CH), lambda s: (s, 0)),
        out_shape=jax.ShapeDtypeStruct((nseq * L, CONV_CH), BF16),
        scratch_shapes=[pltpu.VMEM((L + 2 * PAD, CONV_CH), F32), pltpu.VMEM((CONV_ROWS, CONV_CH), F32)],
        compiler_params=_cparams(("parallel",), 48),
        name="conv_L%d" % L,
    )(proj, proj, w32, b, lg, lb)


def _pool_body(L, u_ref, w_ref, sc_ref, o_ref, pad_ref):
    rows = 256
    zeros = jnp.zeros((PAD, 512), F32)
    pad_ref[0:PAD, :] = zeros
    pad_ref[PAD + L:2 * PAD + L, :] = zeros
    pad_ref[PAD:PAD + L, :] = u_ref[...]
    for c in range(L // rows):
        r0 = c * rows
        t = lax.broadcasted_iota(jnp.int32, (rows, 1), 0) + r0
        for g, w in enumerate(POOL_WINDOWS):
            cols = slice(g * POOL_GROUP_CH, (g + 1) * POOL_GROUP_CH)
            acc = jnp.zeros((rows, POOL_GROUP_CH), F32)
            for i in range(-(w // 2), w - w // 2):
                acc = acc + pad_ref[PAD + r0 + i:PAD + r0 + i + rows, cols]
            lo = jnp.maximum(t - w // 2, 0)
            hi = jnp.minimum(t + (w - w // 2), L)
            cnt = (hi - lo).astype(F32)
            pooled = acc / cnt - pad_ref[PAD + r0:PAD + r0 + rows, cols]
            mixed = jnp.dot(pooled.astype(BF16), w_ref[g], preferred_element_type=F32)
            o_ref[r0:r0 + rows, cols] = (mixed * sc_ref[:, cols]).astype(BF16)


def _pool(proj, pw_bf, pscale, L, nseq, row0):
    return pl.pallas_call(
        functools.partial(_pool_body, L),
        grid=(nseq,),
        in_specs=[pl.BlockSpec((L, 512), lambda s: (row0 + s, COL_POOL // 512)),
                  pl.BlockSpec((4, POOL_GROUP_CH, POOL_GROUP_CH), lambda s: (0, 0, 0)),
                  pl.BlockSpec((1, 512), lambda s: (0, 0))],
        out_specs=pl.BlockSpec((L, 512), lambda s: (s, 0)),
        out_shape=jax.ShapeDtypeStruct((nseq * L, 512), BF16),
        scratch_shapes=[pltpu.VMEM((L + 2 * PAD, 512), F32)],
        compiler_params=_cparams(("parallel",), 48),
        name="pool_L%d" % L,
    )(proj, pw_bf, pscale)


def _channel_dft(u, wc_ref, z_ref, r0, L):
    rows = u.shape[0]
    for h in range(FOURIER_HEADS):
        cols = slice(h * FOURIER_HEAD_CH, (h + 1) * FOURIER_HEAD_CH)
        a = jnp.dot(u[:, cols].astype(BF16), wc_ref[...], preferred_element_type=F32)
        z_ref[r0:r0 + rows, cols] = a[:, :FOURIER_HEAD_CH].astype(BF16)
        z_ref[L + r0:L + r0 + rows, cols] = a[:, FOURIER_HEAD_CH:].astype(BF16)


def _four_body(L, tr, u_ref, wc_ref, wl_ref, o_ref, z_ref):
    @pl.when(pl.program_id(1) == 0)
    def _():
        rows = min(L, 512)
        for c in range(L // rows):
            _channel_dft(u_ref[c * rows:(c + 1) * rows, :], wc_ref, z_ref, c * rows, L)

    scale = (L * FOURIER_HEAD_CH) ** -0.5
    o_ref[...] = (jnp.dot(wl_ref[...], z_ref[...], preferred_element_type=F32) * scale).astype(BF16)


def _four(proj, wc_bf, wl_bf, L, nseq, row0):
    tr = min(L, 256)
    return pl.pallas_call(
        functools.partial(_four_body, L, tr),
        grid=(nseq, L // tr),
        in_specs=[pl.BlockSpec((L, 512), lambda s, r: (row0 + s, COL_FOUR // 512)),
                  pl.BlockSpec((FOURIER_HEAD_CH, 2 * FOURIER_HEAD_CH), lambda s, r: (0, 0)),
                  pl.BlockSpec((tr, 2 * L), lambda s, r: (r, 0))],
        out_specs=pl.BlockSpec((tr, 512), lambda s, r: (s * (L // tr) + r, 0)),
        out_shape=jax.ShapeDtypeStruct((nseq * L, 512), BF16),
        scratch_shapes=[pltpu.VMEM((2 * L, 512), BF16)],
        compiler_params=_cparams(("parallel", "arbitrary"), 48),
        name="four_L%d" % L,
    )(proj, wc_bf, wl_bf)


def _route(logits):
    lane = lax.broadcasted_iota(jnp.int32, logits.shape, 1).astype(F32)
    neg = jnp.float32(-jnp.inf)
    big = jnp.float32(1 << 20)
    is_g = (lane >= N_EXPERTS) & (lane < N_EXPERTS + N_EXPERT_GROUPS)
    gl = jnp.where(is_g, logits, neg)
    gmax = jnp.max(gl, axis=-1, keepdims=True)
    g_idx = jnp.min(jnp.where(gl == gmax, lane, big), axis=-1, keepdims=True) - N_EXPERTS
    denom = jnp.sum(jnp.where(is_g, jnp.exp(gl - gmax), 0.0), axis=-1, keepdims=True)
    g_w = 1.0 / denom
    in_grp = (lane >= g_idx * EXPERTS_PER_GROUP) & (lane < (g_idx + 1) * EXPERTS_PER_GROUP)
    el = jnp.where(in_grp, logits, neg)
    v1 = jnp.max(el, axis=-1, keepdims=True)
    i1 = jnp.min(jnp.where(el == v1, lane, big), axis=-1, keepdims=True)
    el2 = jnp.where(lane == i1, neg, el)
    v2 = jnp.max(el2, axis=-1, keepdims=True)
    i2 = jnp.min(jnp.where(el2 == v2, lane, big), axis=-1, keepdims=True)
    e2 = jnp.exp(v2 - v1)
    p1 = 1.0 / (1.0 + e2)
    p2 = e2 / (1.0 + e2)
    gates = g_w * jnp.where(lane == i1, p1, jnp.where(lane == i2, p2, 0.0))
    return gates + jnp.where(lane == g_idx + N_EXPERTS, 1.0, 0.0)


def _pair_specs(width):
    n = T_CTX // TM
    return [pl.BlockSpec((TM, width), lambda i: (jnp.minimum(i, n - 1), 0)),
            pl.BlockSpec((TM, width), lambda i: (jnp.maximum(i - n, 0), 0))]


def _outproj_body(nx, *refs):
    mixers = refs[nx:nx + 8]
    (w_ref, g1_ref, gain_ref, sc_ref, sh_ref, wr_ref, br_ref, xo_ref, hg_ref, route_ref, wb_ref) = refs[nx + 8:]

    @pl.when(pl.program_id(0) == 0)
    def _():
        wb_ref[...] = w_ref[...].astype(BF16)

    for c in range(TM // OUT_ROWS):
        rows = slice(c * OUT_ROWS, (c + 1) * OUT_ROWS)
        mix = None
        for k in range(4):
            part = jnp.dot(_load_x(mixers[2 * k:2 * k + 2], rows), wb_ref[512 * k:512 * (k + 1), :],
                           preferred_element_type=F32)
            mix = part if mix is None else mix + part
        x = _load_x(refs[:nx], rows) + g1_ref[...] * mix
        xo_ref[rows, :] = x
        ms = jnp.mean(x * x, axis=-1, keepdims=True)
        h = x * lax.rsqrt(ms + EPS) * gain_ref[...]
        h = h * (1.0 + sc_ref[...]) + sh_ref[...]
        h_hi = h.astype(BF16)
        h_lo = (h - h_hi.astype(F32)).astype(BF16)
        t = (jnp.dot(h_hi, wr_ref[...], preferred_element_type=F32)
             + jnp.dot(h_lo, wr_ref[...], preferred_element_type=F32))
        logits = t + pltpu.roll(t, LANES - ROUTER_LO_LANE, axis=1) + br_ref[...]
        route = _route(logits)
        route_ref[rows, :] = route
        rec0 = c * OUT_ROWS * REC
        for s in range(H_ROWS):
            hg_ref[pl.ds(rec0 + s, OUT_ROWS, stride=REC), :] = h[:, s * LANES:(s + 1) * LANES]
        hg_ref[pl.ds(rec0 + H_ROWS, OUT_ROWS, stride=REC), :] = route
        for s in range(H_ROWS + 1, REC):
            hg_ref[pl.ds(rec0 + s, OUT_ROWS, stride=REC), :] = jnp.zeros((OUT_ROWS, LANES), F32)


def _outproj(mixers, xs, w_out, mod4, gain, wr, br):
    tile = lambda w: pl.BlockSpec((TM, w), lambda i: (i, 0))
    const = lambda r, c: pl.BlockSpec((r, c), lambda i: (0, 0))
    mix_specs = []
    for _ in range(4):
        mix_specs += _pair_specs(512)
    return pl.pallas_call(
        functools.partial(_outproj_body, len(xs)),
        grid=(T_ALL // TM,),
        in_specs=_x_specs(len(xs)) + mix_specs + [_resident((2048, D)),
                                                  _mod_spec(2, TM), const(1, D), _mod_spec(4, TM), _mod_spec(3, TM),
                                                  const(D, LANES), const(1, LANES)],
        out_specs=[tile(D), pl.BlockSpec((TM * REC, LANES), lambda i: (i, 0)), tile(LANES)],
        out_shape=[jax.ShapeDtypeStruct((T_ALL, D), F32),
                   jax.ShapeDtypeStruct((T_ALL * REC, LANES), F32),
                   jax.ShapeDtypeStruct((T_ALL, LANES), F32)],
        scratch_shapes=[pltpu.VMEM((2048, D), BF16)],
        compiler_params=_cparams(("arbitrary",), 56),
        name="out_proj",
    )(*xs, *[a for pair in mixers for a in pair], w_out, mod4, gain, mod4, mod4, wr, br)


def _rank_body(r_ref, o_ref, cnt_ref, carry_ref):
    @pl.when(pl.program_id(0) == 0)
    def _():
        carry_ref[...] = jnp.zeros_like(carry_ref)

    r = r_ref[...]
    lane = lax.broadcasted_iota(jnp.int32, r.shape, 1)
    is_g = (lane >= N_EXPERTS) & (lane < N_EXPERTS + N_EXPERT_GROUPS)
    onehot = jnp.where(is_g, r, 0.0)
    row = lax.broadcasted_iota(jnp.int32, (RT, RT), 0)
    col = lax.broadcasted_iota(jnp.int32, (RT, RT), 1)
    earlier = jnp.where(col < row, 1.0, 0.0).astype(BF16)
    before = jnp.dot(earlier, onehot.astype(BF16), preferred_element_type=F32) + carry_ref[...]
    rank = jnp.sum(onehot * before, axis=-1, keepdims=True)
    grp = jnp.sum(onehot * (lane - N_EXPERTS).astype(F32), axis=-1, keepdims=True)
    packed = jnp.where(lane == 0, rank, jnp.where(lane == 1, grp, 0.0))
    o_ref[...] = packed.T[0:SUBLANES, :].astype(jnp.int32)
    carry_ref[...] += jnp.sum(onehot, axis=0, keepdims=True)
    cnt_ref[...] = carry_ref[...]


def _rank(route):
    return pl.pallas_call(
        _rank_body,
        grid=(T_ALL // RT,),
        in_specs=[pl.BlockSpec((RT, LANES), lambda i: (i, 0))],
        out_specs=[pl.BlockSpec((SUBLANES, RT), lambda i: (i, 0)),
                   pl.BlockSpec((1, LANES), lambda i: (0, 0))],
        out_shape=[jax.ShapeDtypeStruct((T_ALL // RT * SUBLANES, RT), jnp.int32),
                   jax.ShapeDtypeStruct((1, LANES), F32)],
        scratch_shapes=[pltpu.VMEM((1, LANES), F32)],
        compiler_params=_cparams(("arbitrary",), 32),
        name="moe_rank",
    )(route)


def _wait_rows(copy, n):
    def body(t, c):
        copy.wait()
        return c
    lax.fori_loop(0, n, body, 0, unroll=8)


def _record(ref, idx, rows):
    start = idx * rows if isinstance(idx, int) else pl.multiple_of(idx * rows, rows)
    return ref.at[pl.ds(start, rows)]


def _dispatch_body(slot_sm, fstart_sm, flen_sm, hg_ref, xs_hbm, zrec_ref, sem):
    i = pl.program_id(0)
    base = i * DT

    @pl.when(i == 0)
    def _():
        zrec_ref[...] = jnp.zeros_like(zrec_ref)
        for g in range(N_EXPERT_GROUPS):
            def fill(r, c):
                pltpu.make_async_copy(zrec_ref, _record(xs_hbm, fstart_sm[g] + r, REC), sem.at[1]).start()
                return c
            lax.fori_loop(0, flen_sm[g], fill, 0)
        _wait_rows(pltpu.make_async_copy(zrec_ref, _record(xs_hbm, 0, REC), sem.at[1]), NSLOT - T_ALL)

    def issue(t, c):
        pltpu.make_async_copy(_record(hg_ref, t, REC), _record(xs_hbm, slot_sm[base + t], REC), sem.at[0]).start()
        return c
    lax.fori_loop(0, DT, issue, 0, unroll=8)
    _wait_rows(pltpu.make_async_copy(_record(hg_ref, 0, REC), _record(xs_hbm, 0, REC), sem.at[0]), DT)


def _dispatch(slot, fstart, flen, hg):
    return pl.pallas_call(
        _dispatch_body,
        grid_spec=pltpu.PrefetchScalarGridSpec(
            num_scalar_prefetch=3,
            grid=(T_ALL // DT,),
            in_specs=[pl.BlockSpec((DT * REC, LANES), lambda i, s, f0, f1: (i, 0))],
            out_specs=pl.BlockSpec(memory_space=pl.ANY),
            scratch_shapes=[pltpu.VMEM((REC, LANES), F32), pltpu.SemaphoreType.DMA((2,))]),
        out_shape=jax.ShapeDtypeStruct((NSLOT * REC, LANES), F32),
        compiler_params=_cparams(("arbitrary",), 48),
        name="moe_dispatch",
    )(slot, fstart, flen, hg)


def _ffn_body(tgrp_sm, xs_ref, wg32_ref, wu32_ref, wd32_ref, ys_ref, h_ref, wg_ref, wu_ref, wd_ref):
    j = pl.program_id(0)
    grp = tgrp_sm[j]

    @pl.when(jnp.logical_or(j == 0, grp != tgrp_sm[jnp.maximum(j - 1, 0)]))
    def _():
        for e in range(EXPERTS_PER_GROUP):
            cols = slice(e * EXPERT_FF, (e + 1) * EXPERT_FF)
            wg_ref[:, cols] = wg32_ref[e].astype(BF16)
            wu_ref[:, cols] = wu32_ref[e].astype(BF16)
            wd_ref[cols, :] = wd32_ref[e].astype(BF16)

    for s in range(H_ROWS):
        h_ref[:, s * LANES:(s + 1) * LANES] = xs_ref[pl.ds(s, TS, stride=REC), :].astype(BF16)
    gates = xs_ref[pl.ds(H_ROWS, TS, stride=REC), :]
    h = h_ref[...]
    a = jnp.dot(h, wg_ref[...], preferred_element_type=F32)
    b = jnp.dot(h, wu_ref[...], preferred_element_type=F32)
    hid = _silu(a) * b
    lane = lax.broadcasted_iota(jnp.int32, gates.shape, 1)
    parts = []
    for e in range(EXPERTS_PER_GROUP):
        ge = jnp.sum(jnp.where(lane == grp * EXPERTS_PER_GROUP + e, gates, 0.0), axis=-1, keepdims=True)
        parts.append((hid[:, e * EXPERT_FF:(e + 1) * EXPERT_FF] * ge).astype(BF16))
    y = jnp.dot(jnp.concatenate(parts, axis=1), wd_ref[...], preferred_element_type=F32)
    for s in range(H_ROWS):
        ys_ref[pl.ds(s, TS, stride=H_ROWS), :] = y[:, s * LANES:(s + 1) * LANES]


def _ffn(tile_grp, xs, w_gate, w_up, w_down):
    ffw = EXPERTS_PER_GROUP * EXPERT_FF
    experts = lambda r, c: pl.BlockSpec((EXPERTS_PER_GROUP, r, c), lambda j, tg: (tg[j], 0, 0),
                                        pipeline_mode=pl.Buffered(1))
    return pl.pallas_call(
        _ffn_body,
        grid_spec=pltpu.PrefetchScalarGridSpec(
            num_scalar_prefetch=1,
            grid=(NT,),
            in_specs=[pl.BlockSpec((TS * REC, LANES), lambda j, tg: (j, 0)),
                      experts(D, EXPERT_FF), experts(D, EXPERT_FF), experts(EXPERT_FF, D)],
            out_specs=pl.BlockSpec((TS * H_ROWS, LANES), lambda j, tg: (j, 0)),
            scratch_shapes=[pltpu.VMEM((TS, D), BF16), pltpu.VMEM((D, ffw), BF16),
                            pltpu.VMEM((D, ffw), BF16), pltpu.VMEM((ffw, D), BF16)]),
        out_shape=jax.ShapeDtypeStruct((NSLOT * H_ROWS, LANES), F32),
        compiler_params=_cparams(("arbitrary",), 56),
        name="moe_ffn",
    )(tile_grp, xs, w_gate, w_up, w_down)


def _combine_body(final, slot_sm, ys_hbm, x_ref, g2_ref, fg_ref, *rest):
    outs, buf, sem = rest[:-2], rest[-2], rest[-1]
    i = pl.program_id(0)
    n = pl.num_programs(0)

    def issue(step, b):
        def body(t, c):
            pltpu.make_async_copy(_record(ys_hbm, slot_sm[step * TM + t], H_ROWS),
                                  _record(buf.at[b], t, H_ROWS), sem.at[b]).start()
            return c
        lax.fori_loop(0, TM, body, 0, unroll=8)

    @pl.when(i == 0)
    def _():
        issue(0, 0)

    @pl.when(i + 1 < n)
    def _():
        issue(i + 1, (i + 1) % 2)

    b = i % 2
    _wait_rows(pltpu.make_async_copy(_record(ys_hbm, 0, H_ROWS), _record(buf.at[b], 0, H_ROWS), sem.at[b]), TM)
    y = jnp.concatenate([buf[b, pl.ds(s, TM, stride=H_ROWS), :] for s in range(H_ROWS)], axis=1)
    x = x_ref[...] + g2_ref[...] * y
    if not final:
        outs[0][...] = x
    else:
        ms = jnp.mean(x * x, axis=-1, keepdims=True)
        yn = x * lax.rsqrt(ms + EPS) * fg_ref[...]
        is_ctx = i < T_CTX // TM

        @pl.when(is_ctx)
        def _():
            outs[0][...] = yn

        @pl.when(jnp.logical_not(is_ctx))
        def _():
            outs[1][...] = yn


def _combine(slot, ys, x, mod4, final_gain, final):
    if final:
        nc = T_CTX // TM
        out_specs = [pl.BlockSpec((TM, D), lambda i, s: (jnp.minimum(i, nc - 1), 0)),
                     pl.BlockSpec((TM, D), lambda i, s: (jnp.maximum(i - nc, 0), 0))]
        out_shape = [jax.ShapeDtypeStruct((T_CTX, D), F32), jax.ShapeDtypeStruct((T_LAT, D), F32)]
    else:
        out_specs = pl.BlockSpec((TM, D), lambda i, s: (i, 0))
        out_shape = jax.ShapeDtypeStruct((T_ALL, D), F32)
    return pl.pallas_call(
        functools.partial(_combine_body, final),
        grid_spec=pltpu.PrefetchScalarGridSpec(
            num_scalar_prefetch=1,
            grid=(T_ALL // TM,),
            in_specs=[pl.BlockSpec(memory_space=pl.ANY),
                      pl.BlockSpec((TM, D), lambda i, s: (i, 0)),
                      pl.BlockSpec((None, None, 1, D), lambda i, s: (_mod_row(i, TM), 5, 0, 0)),
                      pl.BlockSpec((1, D), lambda i, s: (0, 0))],
            out_specs=out_specs,
            scratch_shapes=[pltpu.VMEM((2, TM * H_ROWS, LANES), F32), pltpu.SemaphoreType.DMA((2,))]),
        out_shape=out_shape,
        compiler_params=_cparams(("arbitrary",), 48),
        name="moe_combine",
    )(slot, ys, x, mod4, final_gain)


def _moe_plan(rk, cnt):
    rk = rk.reshape(T_ALL // RT, SUBLANES, RT)
    rank = rk[:, 0, :].reshape(T_ALL)
    grp = rk[:, 1, :].reshape(T_ALL)
    cnt = cnt[0, N_EXPERTS:N_EXPERTS + N_EXPERT_GROUPS].astype(jnp.int32)
    padded = (cnt + TS - 1) // TS * TS
    off = jnp.cumsum(padded) - padded
    padded = padded.at[N_EXPERT_GROUPS - 1].set(NSLOT - off[N_EXPERT_GROUPS - 1])
    slot = rank
    for g in range(1, N_EXPERT_GROUPS):
        slot = slot + jnp.where(grp == g, off[g], 0)
    ends = (off + padded)[:N_EXPERT_GROUPS - 1]
    tile_grp = jnp.sum((jnp.arange(NT, dtype=jnp.int32)[:, None] * TS >= ends[None, :]).astype(jnp.int32), axis=1)
    return slot, tile_grp, off + cnt, padded - cnt


def _rope_tables():
    t = np.arange(DEC_SEQ)
    pos = np.stack([t // GRID_W, t % GRID_W], axis=1).astype(np.float64)
    inv = ROPE_THETA ** (-np.arange(16, dtype=np.float64) / 16.0)
    d = np.arange(LANES) % HEAD_DIM
    pair = d // 2
    ang = pos[:, (pair >= 16).astype(np.int64)] * inv[pair % 16][None, :]
    sign = np.where(d % 2 == 0, -1.0, 1.0)
    return np.cos(ang).astype(np.float32), (np.sin(ang) * sign[None, :]).astype(np.float32)


def _dft_pos(L):
    j = np.arange(L)
    ang = 2.0 * np.pi * ((j[:, None] * j[None, :]) % L) / L
    return np.concatenate([np.cos(ang), -np.sin(ang)], axis=1).astype(np.float32)


def _dft_ch():
    c = np.arange(FOURIER_HEAD_CH)
    ang = 2.0 * np.pi * ((c[:, None] * c[None, :]) % FOURIER_HEAD_CH) / FOURIER_HEAD_CH
    return np.concatenate([np.cos(ang), np.sin(ang)], axis=1).astype(np.float32)


def kernel(x_prompt, x_sample, cache_k, cache_v, c, c_ctx, w_ada, b_ada, norm_mix, norm_ffn, w_in, w_out, q_norm, k_norm, conv_w, conv_b, conv_ln_g, conv_ln_b, pool_w, pool_scale, router_group_w, router_group_b, router_expert_w, router_expert_b, moe_w_gate, moe_w_up, moe_w_down, final_norm):
    xs = (x_prompt.reshape(T_CTX, D), x_sample.reshape(T_LAT, D))
    cond8 =jnp.concatenate([c_ctx[None, :], c, jnp.zeros((3, D), F32)], axis=0)
    mod = _ada(cond8, w_ada, b_ada)

    pw_bf = pool_w.astype(BF16)
    n_route = N_EXPERTS + N_EXPERT_GROUPS
    wr = jnp.concatenate([router_expert_w, router_group_w], axis=-1)
    wr_hi = wr.astype(BF16)
    wr_lo = (wr - wr_hi.astype(F32)).astype(BF16)
    w_router = jnp.concatenate([wr_hi, jnp.zeros((DEPTH, D, ROUTER_LO_LANE - n_route), BF16), wr_lo,
                                jnp.zeros((DEPTH, D, LANES - ROUTER_LO_LANE - n_route), BF16)], axis=-1)
    b_router = jnp.concatenate([router_expert_b, router_group_b,
                                jnp.zeros((DEPTH, LANES - N_EXPERTS - N_EXPERT_GROUPS), F32)], axis=-1)
    conv_w32 = jnp.concatenate([conv_w, jnp.zeros((DEPTH, 1, CONV_CH), F32)], axis=1)

    cos_np, sin_np = _rope_tables()
    cos, sin = jnp.asarray(cos_np), jnp.asarray(sin_np)
    wc_bf = jnp.asarray(_dft_ch()).astype(BF16)
    wl_ctx = jnp.asarray(_dft_pos(SEQ)).astype(BF16)
    wl_lat = jnp.asarray(_dft_pos(DEC_SEQ)).astype(BF16)

    cache_k4 = cache_k.reshape(DEC_BATCH, DEPTH, PAST_LEN, KV_CH)
    cache_v4 = cache_v.reshape(DEC_BATCH, DEPTH, PAST_LEN, KV_CH)
    new_k = jnp.zeros((BATCH, DEPTH, SEQ, KV_CH), F32)
    new_v = jnp.zeros((BATCH, DEPTH, SEQ, KV_CH), F32)
    fgain = final_norm.reshape(1, D)

    for l in range(DEPTH):
        mod4 = mod[l].reshape(8, N_MOD, 1, D)
        proj = _inproj(xs, norm_mix[l].reshape(1, D), mod4, w_in[l])
        qg = jnp.tile(q_norm[l], 2).reshape(1, LANES)
        kg = jnp.tile(k_norm[l], 2).reshape(1, LANES)
        attn_c, new_k, new_v = _ctx_attn(proj, qg, kg, new_k, new_v, l)
        attn_l = _lat_attn(proj, cache_k4, cache_v4, qg, kg, cos, sin, l)
        cb, lg, lb = conv_b[l].reshape(1, -1), conv_ln_g[l].reshape(1, -1), conv_ln_b[l].reshape(1, -1)
        ps = pool_scale[l].reshape(1, -1)
        lat0 = T_CTX // DEC_SEQ
        mixers = [(attn_c, attn_l),
                  (_conv(proj, conv_w32[l], cb, lg, lb, SEQ, BATCH, 0),
                   _conv(proj, conv_w32[l], cb, lg, lb, DEC_SEQ, DEC_BATCH, lat0)),
                  (_pool(proj, pw_bf[l], ps, SEQ, BATCH, 0),
                   _pool(proj, pw_bf[l], ps, DEC_SEQ, DEC_BATCH, lat0)),
                  (_four(proj, wc_bf, wl_ctx, SEQ, BATCH, 0),
                   _four(proj, wc_bf, wl_lat, DEC_SEQ, DEC_BATCH, lat0))]
        x, hg, route = _outproj(mixers, xs, w_out[l], mod4, norm_ffn[l].reshape(1, D),
                                w_router[l], b_router[l].reshape(1, LANES))
        slot, tile_grp, fstart, flen = _moe_plan(*_rank(route))
        xslots = _dispatch(slot, fstart, flen, hg)
        yslots = _ffn(tile_grp, xslots, moe_w_gate[l], moe_w_up[l], moe_w_down[l])
        out = _combine(slot, yslots, x, mod4, fgain, l == DEPTH - 1)
        xs = (out,)

    y_prompt = out[0].reshape(BATCH, SEQ, D)
    y_sample = out[1].reshape(DEC_BATCH, DEC_SEQ, D)
    new_k = new_k.reshape(BATCH, DEPTH, SEQ, 2, HEAD_DIM)
    new_v = new_v.reshape(BATCH, DEPTH, SEQ, 2, HEAD_DIM)
    return (y_prompt, y_sample, new_k, new_v)
```

```python
import functools

import numpy as np
import jax
import jax.numpy as jnp
from jax import lax
from jax.experimental import pallas as pl
from jax.experimental.pallas import tpu as pltpu

F32 = jnp.float32
BF16 = jnp.bfloat16

D = 1024
BATCH = 32
SEQ = 256
DEPTH = 2
DEC_BATCH = 4
DEC_SEQ = 2048
PAST_LEN = 256
GRID_W = 64
HEAD_DIM = 64
ATTN_CH = 512
KV_CH = 128
CONV_CH = 512
CONV_WIDTH = 31
POOL_WINDOWS = (2, 4, 8, 16)
POOL_GROUP_CH = 128
FOURIER_HEAD_CH = 128
FOURIER_HEADS = 4
N_EXPERT_GROUPS = 4
EXPERTS_PER_GROUP = 4
N_EXPERTS = 16
EXPERT_FF = 256
ROPE_THETA = 10000.0
EPS = 1e-6
N_MOD = 6

T_CTX = BATCH * SEQ
T_LAT = DEC_BATCH * DEC_SEQ
T_ALL = T_CTX + T_LAT
LANES = 128
SUBLANES = 8
PAD = 16
CONV_ROWS = 128
ROUTER_LO_LANE = 32
IN_COLS = 2816
COL_Q, COL_CA, COL_CG, COL_POOL, COL_FOUR, COL_K, COL_V = 0, 512, 1024, 1536, 2048, 2560, 2688

TM = 512
OUT_ROWS = 256
RT = 512
DT = 1024
TS = 512
NT = T_ALL // TS + N_EXPERT_GROUPS
NSLOT = NT * TS
H_ROWS = D // LANES
REC = H_ROWS
TQ = 256
LK = PAST_LEN + DEC_SEQ


def _cparams(sem, vmem_mb):
    return pltpu.CompilerParams(dimension_semantics=sem, vmem_limit_bytes=vmem_mb * 1024 * 1024)


def _mod_row(i, tm):
    nctx = T_CTX // tm
    per = DEC_SEQ // tm
    return jnp.where(i < nctx, 0, 1 + (i - nctx) // per)


def _mod_spec(k, tm):
    return pl.BlockSpec((None, None, 1, D), lambda i: (_mod_row(i, tm), k, 0, 0))


def _silu(x):
    return x * jax.nn.sigmoid(x)


def _ada_body(c_ref, w_ref, b_ref, o_ref):
    s = _silu(c_ref[...])
    o_ref[...] = jnp.dot(s.astype(BF16), w_ref[...].astype(BF16), preferred_element_type=F32) + b_ref[...]


def _ada(cond8, w_ada, b_ada):
    tn = 1536
    n = N_MOD * D
    return pl.pallas_call(
        _ada_body,
        grid=(DEPTH, n // tn),
        in_specs=[pl.BlockSpec((8, D), lambda l, j: (0, 0)),
                  pl.BlockSpec((None, D, tn), lambda l, j: (l, 0, j)),
                  pl.BlockSpec((None, 1, tn), lambda l, j: (l, 0, j))],
        out_specs=pl.BlockSpec((None, 8, tn), lambda l, j: (l, 0, j)),
        out_shape=jax.ShapeDtypeStruct((DEPTH, 8, n), F32),
        compiler_params=_cparams(("arbitrary", "arbitrary"), 40),
        name="ada_mod",
    )(cond8, w_ada, b_ada.reshape(DEPTH, 1, n))


def _x_specs(nx):
    if nx == 1:
        return [pl.BlockSpec((TM, D), lambda i: (i, 0))]
    n = T_CTX // TM
    return [pl.BlockSpec((TM, D), lambda i: (jnp.minimum(i, n - 1), 0)),
            pl.BlockSpec((TM, D), lambda i: (jnp.maximum(i - n, 0), 0))]


def _load_x(x_refs, rows=slice(None)):
    if len(x_refs) == 1:
        return x_refs[0][rows, :]
    return jnp.where(pl.program_id(0) < T_CTX // TM, x_refs[0][rows, :], x_refs[1][rows, :])


def _resident(shape, layer):
    return pl.BlockSpec((None,) + shape, lambda *_: (layer,) + (0,) * len(shape), pipeline_mode=pl.Buffered(1))


def _inproj_body(nx, *refs):
    g_ref, sc_ref, sh_ref, w_ref, o_ref, wb_ref = refs[nx:]

    @pl.when(pl.program_id(0) == 0)
    def _():
        kv0, kv1 = ATTN_CH, ATTN_CH + 2 * KV_CH
        wb_ref[:, 0:kv0] = w_ref[:, 0:kv0].astype(BF16)
        wb_ref[:, kv0:IN_COLS - 2 * KV_CH] = w_ref[:, kv1:IN_COLS].astype(BF16)
        wb_ref[:, IN_COLS - 2 * KV_CH:IN_COLS] = w_ref[:, kv0:kv1].astype(BF16)

    x = _load_x(refs[:nx])
    ms = jnp.mean(x * x, axis=-1, keepdims=True)
    h = x * lax.rsqrt(ms + EPS) * g_ref[...]
    h = h * (1.0 + sc_ref[...]) + sh_ref[...]
    o_ref[...] = jnp.dot(h.astype(BF16), wb_ref[...], preferred_element_type=F32)


def _inproj(xs, gain, mod4, w_in, layer):
    return pl.pallas_call(
        functools.partial(_inproj_body, len(xs)),
        grid=(T_ALL // TM,),
        in_specs=_x_specs(len(xs)) + [pl.BlockSpec((1, D), lambda i: (0, 0)),
                                      _mod_spec(1, TM), _mod_spec(0, TM),
                                      _resident((D, IN_COLS), layer)],
        out_specs=pl.BlockSpec((TM, IN_COLS), lambda i: (i, 0)),
        out_shape=jax.ShapeDtypeStruct((T_ALL, IN_COLS), F32),
        scratch_shapes=[pltpu.VMEM((D, IN_COLS), BF16)],
        compiler_params=_cparams(("arbitrary",), 56),
        name="in_proj",
    )(*xs, gain, mod4, mod4, w_in)


def _lane_lo():
    return lax.broadcasted_iota(jnp.int32, (1, LANES), 1) < HEAD_DIM


def _head_norm(x, gain):
    lo = _lane_lo()
    x2 = x * x
    s_lo = jnp.sum(jnp.where(lo, x2, 0.0), axis=-1, keepdims=True)
    s_hi = jnp.sum(jnp.where(lo, 0.0, x2), axis=-1, keepdims=True)
    r = jnp.where(lo, lax.rsqrt(s_lo * (1.0 / HEAD_DIM) + EPS), lax.rsqrt(s_hi * (1.0 / HEAD_DIM) + EPS))
    return x * r * gain


def _rope(x, cos, sin_signed):
    even = (lax.broadcasted_iota(jnp.int32, (1, LANES), 1) % 2) == 0
    swapped = jnp.where(even, pltpu.roll(x, LANES - 1, axis=1), pltpu.roll(x, 1, axis=1))
    return x * cos + swapped * sin_signed


def _split_heads(x, g):
    lo = _lane_lo()
    own = jnp.where(lo if g == 0 else jnp.logical_not(lo), x, 0.0)
    other = pltpu.roll(own, HEAD_DIM, axis=1)
    return (own, other) if g == 0 else (other, own)


Q_SCALE = HEAD_DIM ** -0.5 * 1.4426950408889634


def _ones_cols(rows, lk):
    r = lax.broadcasted_iota(jnp.int32, (rows, LANES), 0)
    lane = lax.broadcasted_iota(jnp.int32, (rows, LANES), 1)
    return jnp.where(((lane == 0) & (r < lk)) | ((lane == 1) & (r >= lk)), 1.0, 0.0)


def _scores(q_bf, k_cat):
    return lax.dot_general(q_bf, k_cat, (((1,), (1,)), ((), ())), preferred_element_type=F32)


def _attend_tile(q_bf, k_cat, v_cat, lk):
    return _softmax_pv(_scores(q_bf, k_cat), v_cat, lk)


def _softmax_pv(s, v_cat, lk):
    s_lo, s_hi = s[:, :lk], s[:, lk:]
    p_lo = jnp.exp2(s_lo - jnp.max(s_lo, axis=-1, keepdims=True))
    p_hi = jnp.exp2(s_hi - jnp.max(s_hi, axis=-1, keepdims=True))
    p = jnp.concatenate([p_lo, p_hi], axis=1).astype(BF16)
    o = jnp.dot(p, v_cat, preferred_element_type=F32)
    l_lo, l_hi = o[:, LANES:LANES + 1], o[:, LANES + 1:LANES + 2]
    return o[:, :LANES] * jnp.where(_lane_lo(), 1.0 / l_lo, 1.0 / l_hi)


def _ctx_attn_body(q_ref, k_ref, v_ref, qg_ref, kg_ref, kc_in, vc_in, o_ref, kc_ref, vc_ref):
    del kc_in, vc_in
    kn = _head_norm(k_ref[...], kg_ref[...])
    v = v_ref[...]
    kc_ref[...] = kn
    vc_ref[...] = v
    ones = _ones_cols(2 * SEQ, SEQ)
    for g in range(2):
        k_lo, k_hi = _split_heads(kn, g)
        v_lo, v_hi = _split_heads(v, g)
        k_cat = jnp.concatenate([k_lo, k_hi], axis=0).astype(BF16)
        v_cat = jnp.concatenate([jnp.concatenate([v_lo, v_hi], axis=0), ones], axis=1).astype(BF16)
        for tt in range(2):
            t = 2 * g + tt
            q = _head_norm(q_ref[:, t * LANES:(t + 1) * LANES], qg_ref[...]) * Q_SCALE
            o = _attend_tile(q.astype(BF16), k_cat, v_cat, SEQ)
            o_ref[:, t * LANES:(t + 1) * LANES] = o.astype(BF16)


def _ctx_attn(proj, qg, kg, kcache, vcache, layer):
    nq = ATTN_CH // LANES
    return pl.pallas_call(
        _ctx_attn_body,
        grid=(BATCH,),
        in_specs=[pl.BlockSpec((SEQ, ATTN_CH), lambda b: (b, COL_Q // ATTN_CH)),
                  pl.BlockSpec((SEQ, KV_CH), lambda b: (b, COL_K // KV_CH)),
                  pl.BlockSpec((SEQ, KV_CH), lambda b: (b, COL_V // KV_CH)),
                  pl.BlockSpec((1, LANES), lambda b: (0, 0)),
                  pl.BlockSpec((1, LANES), lambda b: (0, 0)),
                  pl.BlockSpec(memory_space=pl.ANY),
                  pl.BlockSpec(memory_space=pl.ANY)],
        out_specs=[pl.BlockSpec((SEQ, ATTN_CH), lambda b: (b, 0)),
                   pl.BlockSpec((None, None, SEQ, KV_CH), lambda b: (b, layer, 0, 0)),
                   pl.BlockSpec((None, None, SEQ, KV_CH), lambda b: (b, layer, 0, 0))],
        out_shape=[jax.ShapeDtypeStruct((T_CTX, ATTN_CH), BF16),
                   jax.ShapeDtypeStruct((BATCH, DEPTH, SEQ, KV_CH), F32),
                   jax.ShapeDtypeStruct((BATCH, DEPTH, SEQ, KV_CH), F32)],
        input_output_aliases={5: 1, 6: 2},
        compiler_params=_cparams(("parallel",), 32),
        name="ctx_attn",
    )(proj, proj, proj, qg, kg, kcache, vcache)


def _lat_attn_body(q_ref, k_ref, v_ref, ck_ref, cv_ref, qg_ref, kg_ref, cosk_ref, sink_ref, cosq_ref, sinq_ref,
                   o_ref, kcat_ref, vcat_ref):
    @pl.when(pl.program_id(1) == 0)
    def _():
        rows = 256
        for c in range(LK // rows):
            if c == 0:
                kn = ck_ref[...]
                v = cv_ref[...]
            else:
                r0 = (c - 1) * rows
                kn = _head_norm(k_ref[r0:r0 + rows, :], kg_ref[...])
                kn = _rope(kn, cosk_ref[r0:r0 + rows, :], sink_ref[r0:r0 + rows, :])
                v = v_ref[r0:r0 + rows, :]
            for g in range(2):
                k_lo, k_hi = _split_heads(kn, g)
                v_lo, v_hi = _split_heads(v, g)
                kcat_ref[g, c * rows:(c + 1) * rows, :] = k_lo.astype(BF16)
                kcat_ref[g, LK + c * rows:LK + (c + 1) * rows, :] = k_hi.astype(BF16)
                vcat_ref[g, c * rows:(c + 1) * rows, 0:LANES] = v_lo.astype(BF16)
                vcat_ref[g, LK + c * rows:LK + (c + 1) * rows, 0:LANES] = v_hi.astype(BF16)
                vcat_ref[g, c * rows:(c + 1) * rows, LANES:2 * LANES] = _ones_cols(rows, rows).astype(BF16)
                vcat_ref[g, LK + c * rows:LK + (c + 1) * rows, LANES:2 * LANES] = _ones_cols(rows, 0).astype(BF16)

    def scores(t):
        q = _head_norm(q_ref[:, t * LANES:(t + 1) * LANES], qg_ref[...])
        q = _rope(q, cosq_ref[...], sinq_ref[...]) * Q_SCALE
        return _scores(q.astype(BF16), kcat_ref[t // 2])

    nt = ATTN_CH // LANES
    s = scores(0)
    for t in range(nt):
        s_next = scores(t + 1) if t + 1 < nt else None
        o = _softmax_pv(s, vcat_ref[t // 2], LK)
        o_ref[:, t * LANES:(t + 1) * LANES] = o.astype(BF16)
        s = s_next


def _lat_attn(proj, cache_k4, cache_v4, qg, kg, cos, sin, layer):
    nqb = DEC_SEQ // TQ
    row0 = T_CTX // DEC_SEQ
    return pl.pallas_call(
        _lat_attn_body,
        grid=(DEC_BATCH, nqb),
        in_specs=[pl.BlockSpec((TQ, ATTN_CH), lambda b, i: (T_CTX // TQ + b * nqb + i, COL_Q // ATTN_CH)),
                  pl.BlockSpec((DEC_SEQ, KV_CH), lambda b, i: (row0 + b, COL_K // KV_CH)),
                  pl.BlockSpec((DEC_SEQ, KV_CH), lambda b, i: (row0 + b, COL_V // KV_CH)),
                  pl.BlockSpec((None, None, PAST_LEN, KV_CH), lambda b, i: (b, layer, 0, 0)),
                  pl.BlockSpec((None, None, PAST_LEN, KV_CH), lambda b, i: (b, layer, 0, 0)),
                  pl.BlockSpec((1, LANES), lambda b, i: (0, 0)),
                  pl.BlockSpec((1, LANES), lambda b, i: (0, 0)),
                  pl.BlockSpec((DEC_SEQ, LANES), lambda b, i: (0, 0)),
                  pl.BlockSpec((DEC_SEQ, LANES), lambda b, i: (0, 0)),
                  pl.BlockSpec((TQ, LANES), lambda b, i: (i, 0)),
                  pl.BlockSpec((TQ, LANES), lambda b, i: (i, 0))],
        out_specs=pl.BlockSpec((TQ, ATTN_CH), lambda b, i: (b * nqb + i, 0)),
        out_shape=jax.ShapeDtypeStruct((T_LAT, ATTN_CH), BF16),
        scratch_shapes=[pltpu.VMEM((2, 2 * LK, LANES), BF16), pltpu.VMEM((2, 2 * LK, 2 * LANES), BF16)],
        compiler_params=_cparams(("parallel", "arbitrary"), 56),
        name="lat_attn",
    )(proj, proj, proj, cache_k4, cache_v4, qg, kg, cos, sin, cos, sin)


def _conv_body(L, a_ref, g_ref, w_ref, b_ref, lg_ref, lb_ref, o_ref, pad_ref, y_ref):
    rows = CONV_ROWS
    zeros = jnp.zeros((PAD, CONV_CH), F32)
    pad_ref[0:PAD, :] = zeros
    pad_ref[PAD + L:2 * PAD + L, :] = zeros
    pad_ref[PAD:PAD + L, :] = a_ref[...] * jax.nn.sigmoid(g_ref[...])
    shift0 = PAD - CONV_WIDTH // 2

    def chunk(c, carry):
        base = pl.multiple_of(c * rows, rows)
        for lg in range(CONV_CH // LANES):
            lanes = slice(lg * LANES, (lg + 1) * LANES)
            y = jnp.zeros((rows, LANES), F32)
            for r in range(SUBLANES):
                z = None
                for q in range((CONV_WIDTH + shift0) // SUBLANES + 1):
                    j = SUBLANES * q + r - shift0
                    if 0 <= j < CONV_WIDTH:
                        term = w_ref[j:j + 1, lanes] * pad_ref[pl.ds(base + SUBLANES * q, rows + SUBLANES), lanes]
                        z = term if z is None else z + term
                y = y + z[r:r + rows, :]
            y_ref[:, lanes] = y + b_ref[:, lanes]
        half = rows // 2
        for p in range(2):
            y = y_ref[p * half:(p + 1) * half, :]
            mu = jnp.mean(y, axis=-1, keepdims=True)
            yc = y - mu
            var = jnp.mean(yc * yc, axis=-1, keepdims=True)
            yn = yc * lax.rsqrt(var + EPS) * lg_ref[...] + lb_ref[...]
            o_ref[pl.ds(base + p * half, half), :] = _silu(yn).astype(BF16)
        return carry

    lax.fori_loop(0, L // rows, chunk, 0)


def _conv(proj, w32, b, lg, lb, L, nseq, row0):
    vec = pl.BlockSpec((1, CONV_CH), lambda s: (0, 0))
    return pl.pallas_call(
        functools.partial(_conv_body, L),
        grid=(nseq,),
        in_specs=[pl.BlockSpec((L, CONV_CH), lambda s: (row0 + s, COL_CA // CONV_CH)),
                  pl.BlockSpec((L, CONV_CH), lambda s: (row0 + s, COL_CG // CONV_CH)),
                  pl.BlockSpec((32, CONV_CH), lambda s: (0, 0)), vec, vec, vec],
        out_specs=pl.BlockSpec((L, CONV_CH), lambda s: (s, 0)),
        out_shape=jax.ShapeDtypeStruct((nseq * L, CONV_CH), BF16),
        scratch_shapes=[pltpu.VMEM((L + 2 * PAD, CONV_CH), F32), pltpu.VMEM((CONV_ROWS, CONV_CH), F32)],
        compiler_params=_cparams(("parallel",), 48),
        name="conv_L%d" % L,
    )(proj, proj, w32, b, lg, lb)


def _pool_body(L, u_ref, w_ref, sc_ref, o_ref, pad_ref):
    rows = 256
    zeros = jnp.zeros((PAD, 512), F32)
    pad_ref[0:PAD, :] = zeros
    pad_ref[PAD + L:2 * PAD + L, :] = zeros
    pad_ref[PAD:PAD + L, :] = u_ref[...]
    for c in range(L // rows):
        r0 = c * rows
        t = lax.broadcasted_iota(jnp.int32, (rows, 1), 0) + r0
        for g, w in enumerate(POOL_WINDOWS):
            cols = slice(g * POOL_GROUP_CH, (g + 1) * POOL_GROUP_CH)
            acc = jnp.zeros((rows, POOL_GROUP_CH), F32)
            for i in range(-(w // 2), w - w // 2):
                acc = acc + pad_ref[PAD + r0 + i:PAD + r0 + i + rows, cols]
            lo = jnp.maximum(t - w // 2, 0)
            hi = jnp.minimum(t + (w - w // 2), L)
            cnt = (hi - lo).astype(F32)
            pooled = acc / cnt - pad_ref[PAD + r0:PAD + r0 + rows, cols]
            mixed = jnp.dot(pooled.astype(BF16), w_ref[g], preferred_element_type=F32)
            o_ref[r0:r0 + rows, cols] = (mixed * sc_ref[:, cols]).astype(BF16)


def _pool(proj, pw_bf, pscale, L, nseq, row0):
    return pl.pallas_call(
        functools.partial(_pool_body, L),
        grid=(nseq,),
        in_specs=[pl.BlockSpec((L, 512), lambda s: (row0 + s, COL_POOL // 512)),
                  pl.BlockSpec((4, POOL_GROUP_CH, POOL_GROUP_CH), lambda s: (0, 0, 0)),
                  pl.BlockSpec((1, 512), lambda s: (0, 0))],
        out_specs=pl.BlockSpec((L, 512), lambda s: (s, 0)),
        out_shape=jax.ShapeDtypeStruct((nseq * L, 512), BF16),
        scratch_shapes=[pltpu.VMEM((L + 2 * PAD, 512), F32)],
        compiler_params=_cparams(("parallel",), 48),
        name="pool_L%d" % L,
    )(proj, pw_bf, pscale)


def _channel_dft(u, wc_ref, z_ref, r0, L):
    rows = u.shape[0]
    for h in range(FOURIER_HEADS):
        cols = slice(h * FOURIER_HEAD_CH, (h + 1) * FOURIER_HEAD_CH)
        a = jnp.dot(u[:, cols].astype(BF16), wc_ref[...], preferred_element_type=F32)
        z_ref[r0:r0 + rows, cols] = a[:, :FOURIER_HEAD_CH].astype(BF16)
        z_ref[L + r0:L + r0 + rows, cols] = a[:, FOURIER_HEAD_CH:].astype(BF16)


def _four_body(L, tr, u_ref, wc_ref, wl_ref, o_ref, z_ref):
    @pl.when(pl.program_id(1) == 0)
    def _():
        rows = min(L, 512)
        for c in range(L // rows):
            _channel_dft(u_ref[c * rows:(c + 1) * rows, :], wc_ref, z_ref, c * rows, L)

    scale = (L * FOURIER_HEAD_CH) ** -0.5
    o_ref[...] = (jnp.dot(wl_ref[...], z_ref[...], preferred_element_type=F32) * scale).astype(BF16)


def _four(proj, wc_bf, wl_bf, L, nseq, row0):
    tr = min(L, 256)
    return pl.pallas_call(
        functools.partial(_four_body, L, tr),
        grid=(nseq, L // tr),
        in_specs=[pl.BlockSpec((L, 512), lambda s, r: (row0 + s, COL_FOUR // 512)),
                  pl.BlockSpec((FOURIER_HEAD_CH, 2 * FOURIER_HEAD_CH), lambda s, r: (0, 0)),
                  pl.BlockSpec((tr, 2 * L), lambda s, r: (r, 0))],
        out_specs=pl.BlockSpec((tr, 512), lambda s, r: (s * (L // tr) + r, 0)),
        out_shape=jax.ShapeDtypeStruct((nseq * L, 512), BF16),
        scratch_shapes=[pltpu.VMEM((2 * L, 512), BF16)],
        compiler_params=_cparams(("parallel", "arbitrary"), 48),
        name="four_L%d" % L,
    )(proj, wc_bf, wl_bf)


def _route(logits):
    lane = lax.broadcasted_iota(jnp.int32, logits.shape, 1).astype(F32)
    neg = jnp.float32(-jnp.inf)
    big = jnp.float32(1 << 20)
    is_g = (lane >= N_EXPERTS) & (lane < N_EXPERTS + N_EXPERT_GROUPS)
    gl = jnp.where(is_g, logits, neg)
    gmax = jnp.max(gl, axis=-1, keepdims=True)
    g_idx = jnp.min(jnp.where(gl == gmax, lane, big), axis=-1, keepdims=True) - N_EXPERTS
    denom = jnp.sum(jnp.where(is_g, jnp.exp(gl - gmax), 0.0), axis=-1, keepdims=True)
    g_w = 1.0 / denom
    in_grp = (lane >= g_idx * EXPERTS_PER_GROUP) & (lane < (g_idx + 1) * EXPERTS_PER_GROUP)
    el = jnp.where(in_grp, logits, neg)
    v1 = jnp.max(el, axis=-1, keepdims=True)
    i1 = jnp.min(jnp.where(el == v1, lane, big), axis=-1, keepdims=True)
    el2 = jnp.where(lane == i1, neg, el)
    v2 = jnp.max(el2, axis=-1, keepdims=True)
    i2 = jnp.min(jnp.where(el2 == v2, lane, big), axis=-1, keepdims=True)
    e2 = jnp.exp(v2 - v1)
    p1 = 1.0 / (1.0 + e2)
    p2 = e2 / (1.0 + e2)
    gates = g_w * jnp.where(lane == i1, p1, jnp.where(lane == i2, p2, 0.0))
    return gates + jnp.where(lane == g_idx + N_EXPERTS, 1.0, 0.0)


def _pair_specs(width):
    n = T_CTX // TM
    return [pl.BlockSpec((TM, width), lambda i: (jnp.minimum(i, n - 1), 0)),
            pl.BlockSpec((TM, width), lambda i: (jnp.maximum(i - n, 0), 0))]


def _outproj_body(nx, *refs):
    mixers = refs[nx:nx + 8]
    (w_ref, g1_ref, gain_ref, sc_ref, sh_ref, wr_ref, br_ref, xo_ref, hg_ref, route_ref, wb_ref) = refs[nx + 8:]

    @pl.when(pl.program_id(0) == 0)
    def _():
        wb_ref[...] = w_ref[...].astype(BF16)

    for c in range(TM // OUT_ROWS):
        rows = slice(c * OUT_ROWS, (c + 1) * OUT_ROWS)
        mix = None
        for k in range(4):
            part = jnp.dot(_load_x(mixers[2 * k:2 * k + 2], rows), wb_ref[512 * k:512 * (k + 1), :],
                           preferred_element_type=F32)
            mix = part if mix is None else mix + part
        x = _load_x(refs[:nx], rows) + g1_ref[...] * mix
        xo_ref[rows, :] = x
        ms = jnp.mean(x * x, axis=-1, keepdims=True)
        h = x * lax.rsqrt(ms + EPS) * gain_ref[...]
        h = h * (1.0 + sc_ref[...]) + sh_ref[...]
        h_hi = h.astype(BF16)
        h_lo = (h - h_hi.astype(F32)).astype(BF16)
        t = (jnp.dot(h_hi, wr_ref[...], preferred_element_type=F32)
             + jnp.dot(h_lo, wr_ref[...], preferred_element_type=F32))
        logits = t + pltpu.roll(t, LANES - ROUTER_LO_LANE, axis=1) + br_ref[...]
        route = _route(logits)
        route_ref[rows, :] = route
        rec0 = c * OUT_ROWS * REC
        for s in range(H_ROWS):
            hg_ref[pl.ds(rec0 + s, OUT_ROWS, stride=REC), :] = h[:, s * LANES:(s + 1) * LANES]


def _outproj(mixers, xs, w_out, mod4, gain, wr, br, layer):
    tile = lambda w: pl.BlockSpec((TM, w), lambda i: (i, 0))
    const = lambda r, c: pl.BlockSpec((r, c), lambda i: (0, 0))
    mix_specs = []
    for _ in range(4):
        mix_specs += _pair_specs(512)
    return pl.pallas_call(
        functools.partial(_outproj_body, len(xs)),
        grid=(T_ALL // TM,),
        in_specs=_x_specs(len(xs)) + mix_specs + [_resident((2048, D), layer),
                                                  _mod_spec(2, TM), const(1, D), _mod_spec(4, TM), _mod_spec(3, TM),
                                                  const(D, LANES), const(1, LANES)],
        out_specs=[tile(D), pl.BlockSpec((TM * REC, LANES), lambda i: (i, 0)), tile(LANES)],
        out_shape=[jax.ShapeDtypeStruct((T_ALL, D), F32),
                   jax.ShapeDtypeStruct((T_ALL * REC, LANES), F32),
                   jax.ShapeDtypeStruct((T_ALL, LANES), F32)],
        scratch_shapes=[pltpu.VMEM((2048, D), BF16)],
        compiler_params=_cparams(("arbitrary",), 56),
        name="out_proj",
    )(*xs, *[a for pair in mixers for a in pair], w_out, mod4, gain, mod4, mod4, wr, br)


def _rank_body(r_ref, o_ref, cnt_ref, carry_ref):
    @pl.when(pl.program_id(0) == 0)
    def _():
        carry_ref[...] = jnp.zeros_like(carry_ref)

    r = r_ref[...]
    lane = lax.broadcasted_iota(jnp.int32, r.shape, 1)
    is_g = (lane >= N_EXPERTS) & (lane < N_EXPERTS + N_EXPERT_GROUPS)
    onehot = jnp.where(is_g, r, 0.0)
    row = lax.broadcasted_iota(jnp.int32, (RT, RT), 0)
    col = lax.broadcasted_iota(jnp.int32, (RT, RT), 1)
    earlier = jnp.where(col < row, 1.0, 0.0).astype(BF16)
    before = jnp.dot(earlier, onehot.astype(BF16), preferred_element_type=F32) + carry_ref[...]
    rank = jnp.sum(onehot * before, axis=-1, keepdims=True)
    grp = jnp.sum(onehot * (lane - N_EXPERTS).astype(F32), axis=-1, keepdims=True)
    packed = jnp.where(lane == 0, rank, jnp.where(lane == 1, grp, 0.0))
    o_ref[...] = packed.T[0:SUBLANES, :].astype(jnp.int32)
    carry_ref[...] += jnp.sum(onehot, axis=0, keepdims=True)
    cnt_ref[...] = carry_ref[...]


def _rank(route):
    return pl.pallas_call(
        _rank_body,
        grid=(T_ALL // RT,),
        in_specs=[pl.BlockSpec((RT, LANES), lambda i: (i, 0))],
        out_specs=[pl.BlockSpec((SUBLANES, RT), lambda i: (i, 0)),
                   pl.BlockSpec((1, LANES), lambda i: (0, 0))],
        out_shape=[jax.ShapeDtypeStruct((T_ALL // RT * SUBLANES, RT), jnp.int32),
                   jax.ShapeDtypeStruct((1, LANES), F32)],
        scratch_shapes=[pltpu.VMEM((1, LANES), F32)],
        compiler_params=_cparams(("arbitrary",), 32),
        name="moe_rank",
    )(route)


def _wait_rows(copy, n):
    def body(t, c):
        copy.wait()
        return c
    lax.fori_loop(0, n, body, 0, unroll=8)


def _record(ref, idx, rows):
    start = idx * rows if isinstance(idx, int) else pl.multiple_of(idx * rows, rows)
    return ref.at[pl.ds(start, rows)]


def _dispatch_body(slot_sm, fstart_sm, flen_sm, hg_ref, route_ref, xs_hbm, gs_ref, zrec_ref, sem):
    i = pl.program_id(0)
    base = i * DT

    @pl.when(i == 0)
    def _():
        gs_ref[...] = jnp.zeros_like(gs_ref)
        zrec_ref[...] = jnp.zeros_like(zrec_ref)
        for g in range(N_EXPERT_GROUPS):
            def fill(r, c):
                pltpu.make_async_copy(zrec_ref, _record(xs_hbm, fstart_sm[g] + r, REC), sem.at[1]).start()
                return c
            lax.fori_loop(0, flen_sm[g], fill, 0)
        _wait_rows(pltpu.make_async_copy(zrec_ref, _record(xs_hbm, 0, REC), sem.at[1]), NSLOT - T_ALL)

    def issue(t, c):
        slot = slot_sm[base + t]
        pltpu.make_async_copy(_record(hg_ref, t, REC), _record(xs_hbm, slot, REC), sem.at[0]).start()
        gs_ref[pl.ds(slot, 1), :] = route_ref[pl.ds(t, 1), :]
        return c
    lax.fori_loop(0, DT, issue, 0, unroll=8)
    _wait_rows(pltpu.make_async_copy(_record(hg_ref, 0, REC), _record(xs_hbm, 0, REC), sem.at[0]), DT)


def _dispatch(slot, fstart, flen, hg, route):
    return pl.pallas_call(
        _dispatch_body,
        grid_spec=pltpu.PrefetchScalarGridSpec(
            num_scalar_prefetch=3,
            grid=(T_ALL // DT,),
            in_specs=[pl.BlockSpec((DT * REC, LANES), lambda i, s, f0, f1: (i, 0)),
                      pl.BlockSpec((DT, LANES), lambda i, s, f0, f1: (i, 0))],
            out_specs=[pl.BlockSpec(memory_space=pl.ANY),
                       pl.BlockSpec((NSLOT, LANES), lambda i, s, f0, f1: (0, 0), pipeline_mode=pl.Buffered(1))],
            scratch_shapes=[pltpu.VMEM((REC, LANES), F32), pltpu.SemaphoreType.DMA((2,))]),
        out_shape=[jax.ShapeDtypeStruct((NSLOT * REC, LANES), F32),
                   jax.ShapeDtypeStruct((NSLOT, LANES), F32)],
        compiler_params=_cparams(("arbitrary",), 48),
        name="moe_dispatch",
    )(slot, fstart, flen, hg, route)


def _ffn_body(tgrp_sm, xs_ref, gs_ref, wg32_ref, wu32_ref, wd32_ref, ys_ref, h_ref, wg_ref, wu_ref, wd_ref):
    j = pl.program_id(0)
    grp = tgrp_sm[j]

    @pl.when(jnp.logical_or(j == 0, grp != tgrp_sm[jnp.maximum(j - 1, 0)]))
    def _():
        for e in range(EXPERTS_PER_GROUP):
            cols = slice(e * EXPERT_FF, (e + 1) * EXPERT_FF)
            wg_ref[:, cols] = wg32_ref[e].astype(BF16)
            wu_ref[:, cols] = wu32_ref[e].astype(BF16)
            wd_ref[cols, :] = wd32_ref[e].astype(BF16)

    for s in range(H_ROWS):
        h_ref[:, s * LANES:(s + 1) * LANES] = xs_ref[pl.ds(s, TS, stride=REC), :].astype(BF16)
    gates = gs_ref[...]
    h = h_ref[...]
    a = jnp.dot(h, wg_ref[...], preferred_element_type=F32)
    b = jnp.dot(h, wu_ref[...], preferred_element_type=F32)
    hid = _silu(a) * b
    lane = lax.broadcasted_iota(jnp.int32, gates.shape, 1)
    parts = []
    for e in range(EXPERTS_PER_GROUP):
        ge = jnp.sum(jnp.where(lane == grp * EXPERTS_PER_GROUP + e, gates, 0.0), axis=-1, keepdims=True)
        parts.append((hid[:, e * EXPERT_FF:(e + 1) * EXPERT_FF] * ge).astype(BF16))
    y = jnp.dot(jnp.concatenate(parts, axis=1), wd_ref[...], preferred_element_type=F32)
    for s in range(H_ROWS):
        ys_ref[pl.ds(s, TS, stride=H_ROWS), :] = y[:, s * LANES:(s + 1) * LANES]


def _ffn(tile_grp, xs, gs, w_gate, w_up, w_down, layer):
    ffw = EXPERTS_PER_GROUP * EXPERT_FF
    experts = lambda r, c: pl.BlockSpec((None, EXPERTS_PER_GROUP, r, c), lambda j, tg: (layer, tg[j], 0, 0),
                                        pipeline_mode=pl.Buffered(1))
    return pl.pallas_call(
        _ffn_body,
        grid_spec=pltpu.PrefetchScalarGridSpec(
            num_scalar_prefetch=1,
            grid=(NT,),
            in_specs=[pl.BlockSpec((TS * REC, LANES), lambda j, tg: (j, 0)),
                      pl.BlockSpec((TS, LANES), lambda j, tg: (j, 0)),
                      experts(D, EXPERT_FF), experts(D, EXPERT_FF), experts(EXPERT_FF, D)],
            out_specs=pl.BlockSpec((TS * H_ROWS, LANES), lambda j, tg: (j, 0)),
            scratch_shapes=[pltpu.VMEM((TS, D), BF16), pltpu.VMEM((D, ffw), BF16),
                            pltpu.VMEM((D, ffw), BF16), pltpu.VMEM((ffw, D), BF16)]),
        out_shape=jax.ShapeDtypeStruct((NSLOT * H_ROWS, LANES), F32),
        compiler_params=_cparams(("arbitrary",), 56),
        name="moe_ffn",
    )(tile_grp, xs, gs, w_gate, w_up, w_down)


def _combine_body(final, slot_sm, ys_hbm, x_ref, g2_ref, fg_ref, *rest):
    outs, buf, sem = rest[:-2], rest[-2], rest[-1]
    i = pl.program_id(0)
    n = pl.num_programs(0)

    def issue(step, b):
        def body(t, c):
            pltpu.make_async_copy(_record(ys_hbm, slot_sm[step * TM + t], H_ROWS),
                                  _record(buf.at[b], t, H_ROWS), sem.at[b]).start()
            return c
        lax.fori_loop(0, TM, body, 0, unroll=8)

    @pl.when(i == 0)
    def _():
        issue(0, 0)

    @pl.when(i + 1 < n)
    def _():
        issue(i + 1, (i + 1) % 2)

    b = i % 2
    _wait_rows(pltpu.make_async_copy(_record(ys_hbm, 0, H_ROWS), _record(buf.at[b], 0, H_ROWS), sem.at[b]), TM)
    y = jnp.concatenate([buf[b, pl.ds(s, TM, stride=H_ROWS), :] for s in range(H_ROWS)], axis=1)
    x = x_ref[...] + g2_ref[...] * y
    if not final:
        outs[0][...] = x
    else:
        ms = jnp.mean(x * x, axis=-1, keepdims=True)
        yn = x * lax.rsqrt(ms + EPS) * fg_ref[...]
        is_ctx = i < T_CTX // TM

        @pl.when(is_ctx)
        def _():
            outs[0][...] = yn

        @pl.when(jnp.logical_not(is_ctx))
        def _():
            outs[1][...] = yn


def _combine(slot, ys, x, mod4, final_gain, final):
    if final:
        nc = T_CTX // TM
        out_specs = [pl.BlockSpec((TM, D), lambda i, s: (jnp.minimum(i, nc - 1), 0)),
                     pl.BlockSpec((TM, D), lambda i, s: (jnp.maximum(i - nc, 0), 0))]
        out_shape = [jax.ShapeDtypeStruct((T_CTX, D), F32), jax.ShapeDtypeStruct((T_LAT, D), F32)]
    else:
        out_specs = pl.BlockSpec((TM, D), lambda i, s: (i, 0))
        out_shape = jax.ShapeDtypeStruct((T_ALL, D), F32)
    return pl.pallas_call(
        functools.partial(_combine_body, final),
        grid_spec=pltpu.PrefetchScalarGridSpec(
            num_scalar_prefetch=1,
            grid=(T_ALL // TM,),
            in_specs=[pl.BlockSpec(memory_space=pl.ANY),
                      pl.BlockSpec((TM, D), lambda i, s: (i, 0)),
                      pl.BlockSpec((None, None, 1, D), lambda i, s: (_mod_row(i, TM), 5, 0, 0)),
                      pl.BlockSpec((1, D), lambda i, s: (0, 0))],
            out_specs=out_specs,
            scratch_shapes=[pltpu.VMEM((2, TM * H_ROWS, LANES), F32), pltpu.SemaphoreType.DMA((2,))]),
        out_shape=out_shape,
        compiler_params=_cparams(("arbitrary",), 48),
        name="moe_combine",
    )(slot, ys, x, mod4, final_gain)


def _moe_plan(rk, cnt):
    rk = rk.reshape(T_ALL // RT, SUBLANES, RT)
    rank = rk[:, 0, :].reshape(T_ALL)
    grp = rk[:, 1, :].reshape(T_ALL)
    cnt = cnt[0, N_EXPERTS:N_EXPERTS + N_EXPERT_GROUPS].astype(jnp.int32)
    padded = (cnt + TS - 1) // TS * TS
    off = jnp.cumsum(padded) - padded
    padded = padded.at[N_EXPERT_GROUPS - 1].set(NSLOT - off[N_EXPERT_GROUPS - 1])
    slot = rank
    for g in range(1, N_EXPERT_GROUPS):
        slot = slot + jnp.where(grp == g, off[g], 0)
    ends = (off + padded)[:N_EXPERT_GROUPS - 1]
    tile_grp = jnp.sum((jnp.arange(NT, dtype=jnp.int32)[:, None] * TS >= ends[None, :]).astype(jnp.int32), axis=1)
    return slot, tile_grp, off + cnt, padded - cnt


def _rope_tables():
    t = np.arange(DEC_SEQ)
    pos = np.stack([t // GRID_W, t % GRID_W], axis=1).astype(np.float64)
    inv = ROPE_THETA ** (-np.arange(16, dtype=np.float64) / 16.0)
    d = np.arange(LANES) % HEAD_DIM
    pair = d // 2
    ang = pos[:, (pair >= 16).astype(np.int64)] * inv[pair % 16][None, :]
    sign = np.where(d % 2 == 0, -1.0, 1.0)
    return np.cos(ang).astype(np.float32), (np.sin(ang) * sign[None, :]).astype(np.float32)


def _dft_pos(L):
    j = np.arange(L)
    ang = 2.0 * np.pi * ((j[:, None] * j[None, :]) % L) / L
    return np.concatenate([np.cos(ang), -np.sin(ang)], axis=1).astype(np.float32)


def _dft_ch():
    c = np.arange(FOURIER_HEAD_CH)
    ang = 2.0 * np.pi * ((c[:, None] * c[None, :]) % FOURIER_HEAD_CH) / FOURIER_HEAD_CH
    return np.concatenate([np.cos(ang), np.sin(ang)], axis=1).astype(np.float32)


def kernel(x_prompt, x_sample, cache_k, cache_v, c, c_ctx, w_ada, b_ada, norm_mix, norm_ffn, w_in, w_out, q_norm, k_norm, conv_w, conv_b, conv_ln_g, conv_ln_b, pool_w, pool_scale, router_group_w, router_group_b, router_expert_w, router_expert_b, moe_w_gate, moe_w_up, moe_w_down, final_norm):
    xs = (x_prompt.reshape(T_CTX, D), x_sample.reshape(T_LAT, D))
    cond8 =jnp.concatenate([c_ctx[None, :], c, jnp.zeros((3, D), F32)], axis=0)
    mod = _ada(cond8, w_ada, b_ada)

    pw_bf = pool_w.astype(BF16)
    n_route = N_EXPERTS + N_EXPERT_GROUPS
    wr = jnp.concatenate([router_expert_w, router_group_w], axis=-1)
    wr_hi = wr.astype(BF16)
    wr_lo = (wr - wr_hi.astype(F32)).astype(BF16)
    w_router = jnp.concatenate([wr_hi, jnp.zeros((DEPTH, D, ROUTER_LO_LANE - n_route), BF16), wr_lo,
                                jnp.zeros((DEPTH, D, LANES - ROUTER_LO_LANE - n_route), BF16)], axis=-1)
    b_router = jnp.concatenate([router_expert_b, router_group_b,
                                jnp.zeros((DEPTH, LANES - N_EXPERTS - N_EXPERT_GROUPS), F32)], axis=-1)
    conv_w32 = jnp.concatenate([conv_w, jnp.zeros((DEPTH, 1, CONV_CH), F32)], axis=1)

    cos_np, sin_np = _rope_tables()
    cos, sin = jnp.asarray(cos_np), jnp.asarray(sin_np)
    wc_bf = jnp.asarray(_dft_ch()).astype(BF16)
    wl_ctx = jnp.asarray(_dft_pos(SEQ)).astype(BF16)
    wl_lat = jnp.asarray(_dft_pos(DEC_SEQ)).astype(BF16)

    cache_k4 = cache_k.reshape(DEC_BATCH, DEPTH, PAST_LEN, KV_CH)
    cache_v4 = cache_v.reshape(DEC_BATCH, DEPTH, PAST_LEN, KV_CH)
    new_k = jnp.zeros((BATCH, DEPTH, SEQ, KV_CH), F32)
    new_v = jnp.zeros((BATCH, DEPTH, SEQ, KV_CH), F32)
    fgain = final_norm.reshape(1, D)

    for l in range(DEPTH):
        mod4 = mod[l].reshape(8, N_MOD, 1, D)
        proj = _inproj(xs, norm_mix[l].reshape(1, D), mod4, w_in, l)
        qg = jnp.tile(q_norm[l], 2).reshape(1, LANES)
        kg = jnp.tile(k_norm[l], 2).reshape(1, LANES)
        attn_c, new_k, new_v = _ctx_attn(proj, qg, kg, new_k, new_v, l)
        attn_l = _lat_attn(proj, cache_k4, cache_v4, qg, kg, cos, sin, l)
        cb, lg, lb = conv_b[l].reshape(1, -1), conv_ln_g[l].reshape(1, -1), conv_ln_b[l].reshape(1, -1)
        ps = pool_scale[l].reshape(1, -1)
        lat0 = T_CTX // DEC_SEQ
        mixers = [(attn_c, attn_l),
                  (_conv(proj, conv_w32[l], cb, lg, lb, SEQ, BATCH, 0),
                   _conv(proj, conv_w32[l], cb, lg, lb, DEC_SEQ, DEC_BATCH, lat0)),
                  (_pool(proj, pw_bf[l], ps, SEQ, BATCH, 0),
                   _pool(proj, pw_bf[l], ps, DEC_SEQ, DEC_BATCH, lat0)),
                  (_four(proj, wc_bf, wl_ctx, SEQ, BATCH, 0),
                   _four(proj, wc_bf, wl_lat, DEC_SEQ, DEC_BATCH, lat0))]
        x, hg, route = _outproj(mixers, xs, w_out, mod4, norm_ffn[l].reshape(1, D),
                                w_router[l], b_router[l].reshape(1, LANES), l)
        slot, tile_grp, fstart, flen = _moe_plan(*_rank(route))
        xslots, gslots = _dispatch(slot, fstart, flen, hg, route)
        yslots = _ffn(tile_grp, xslots, gslots, moe_w_gate, moe_w_up, moe_w_down, l)
        out = _combine(slot, yslots, x, mod4, fgain, l == DEPTH - 1)
        xs = (out,)

    y_prompt = out[0].reshape(BATCH, SEQ, D)
    y_sample = out[1].reshape(DEC_BATCH, DEC_SEQ, D)
    new_k = new_k.reshape(BATCH, DEPTH, SEQ, 2, HEAD_DIM)
    new_v = new_v.reshape(BATCH, DEPTH, SEQ, 2, HEAD_DIM)
    return (y_prompt, y_sample, new_k, new_v)
```

```python
import functools

import numpy as np
import jax
import jax.numpy as jnp
from jax import lax
from jax.experimental import pallas as pl
from jax.experimental.pallas import tpu as pltpu

F32 = jnp.float32
BF16 = jnp.bfloat16

D = 1024
BATCH = 32
SEQ = 256
DEPTH = 2
DEC_BATCH = 4
DEC_SEQ = 2048
PAST_LEN = 256
GRID_W = 64
HEAD_DIM = 64
ATTN_CH = 512
KV_CH = 128
CONV_CH = 512
CONV_WIDTH = 31
POOL_WINDOWS = (2, 4, 8, 16)
POOL_GROUP_CH = 128
FOURIER_HEAD_CH = 128
FOURIER_HEADS = 4
N_EXPERT_GROUPS = 4
EXPERTS_PER_GROUP = 4
N_EXPERTS = 16
EXPERT_FF = 256
ROPE_THETA = 10000.0
EPS = 1e-6
N_MOD = 6

T_CTX = BATCH * SEQ
T_LAT = DEC_BATCH * DEC_SEQ
T_ALL = T_CTX + T_LAT
LANES = 128
SUBLANES = 8
PAD = 16
CONV_ROWS = 128
ROUTER_LO_LANE = 32
IN_COLS = 2816
COL_Q, COL_CA, COL_CG, COL_POOL, COL_K, COL_V, COL_FOUR = 0, 512, 1024, 1536, 2048, 2176, 2304
F32_COLS = COL_FOUR

TM = 512
OUT_ROWS = 256
RT = 512
DT = 2048
CTX_G = 2
MIX_G = 4
TS = 512
NT = T_ALL // TS + N_EXPERT_GROUPS
NSLOT = NT * TS
H_ROWS = D // LANES
REC = H_ROWS
TQ = 256
LK = PAST_LEN + DEC_SEQ


def _cparams(sem, vmem_mb):
    return pltpu.CompilerParams(dimension_semantics=sem, vmem_limit_bytes=vmem_mb * 1024 * 1024)


def _mod_row(i, tm):
    nctx = T_CTX // tm
    per = DEC_SEQ // tm
    return jnp.where(i < nctx, 0, 1 + (i - nctx) // per)


def _mod_spec(k, tm):
    return pl.BlockSpec((None, None, 1, D), lambda i: (_mod_row(i, tm), k, 0, 0))


def _silu(x):
    return x * jax.nn.sigmoid(x)


def _ada_body(c_ref, w_ref, b_ref, o_ref):
    s = _silu(c_ref[...])
    o_ref[...] = jnp.dot(s.astype(BF16), w_ref[...].astype(BF16), preferred_element_type=F32) + b_ref[...]


def _ada(cond8, w_ada, b_ada):
    tn = 1536
    n = N_MOD * D
    return pl.pallas_call(
        _ada_body,
        grid=(DEPTH, n // tn),
        in_specs=[pl.BlockSpec((8, D), lambda l, j: (0, 0)),
                  pl.BlockSpec((None, D, tn), lambda l, j: (l, 0, j)),
                  pl.BlockSpec((None, 1, tn), lambda l, j: (l, 0, j))],
        out_specs=pl.BlockSpec((None, 8, tn), lambda l, j: (l, 0, j)),
        out_shape=jax.ShapeDtypeStruct((DEPTH, 8, n), F32),
        compiler_params=_cparams(("arbitrary", "arbitrary"), 40),
        name="ada_mod",
    )(cond8, w_ada, b_ada.reshape(DEPTH, 1, n))


def _x_specs(nx):
    if nx == 1:
        return [pl.BlockSpec((TM, D), lambda i: (i, 0))]
    n = T_CTX // TM
    return [pl.BlockSpec((TM, D), lambda i: (jnp.minimum(i, n - 1), 0)),
            pl.BlockSpec((TM, D), lambda i: (jnp.maximum(i - n, 0), 0))]


def _load_x(x_refs, rows=slice(None)):
    if len(x_refs) == 1:
        return x_refs[0][rows, :]
    return jnp.where(pl.program_id(0) < T_CTX // TM, x_refs[0][rows, :], x_refs[1][rows, :])


def _resident(shape, layer):
    return pl.BlockSpec((None,) + shape, lambda *_: (layer,) + (0,) * len(shape), pipeline_mode=pl.Buffered(1))


def _inproj_body(nx, *refs):
    g_ref, sc_ref, sh_ref, w_ref, o_ref, of_ref, wb_ref = refs[nx:]

    @pl.when(pl.program_id(0) == 0)
    def _():
        kv0, kv1 = ATTN_CH, ATTN_CH + 2 * KV_CH
        wb_ref[:, 0:kv0] = w_ref[:, 0:kv0].astype(BF16)
        wb_ref[:, kv0:COL_K] = w_ref[:, kv1:kv1 + COL_K - kv0].astype(BF16)
        wb_ref[:, COL_K:COL_FOUR] = w_ref[:, kv0:kv1].astype(BF16)
        wb_ref[:, COL_FOUR:IN_COLS] = w_ref[:, COL_FOUR:IN_COLS].astype(BF16)

    x = _load_x(refs[:nx])
    ms = jnp.mean(x * x, axis=-1, keepdims=True)
    h = x * lax.rsqrt(ms + EPS) * g_ref[...]
    h = h * (1.0 + sc_ref[...]) + sh_ref[...]
    hb = h.astype(BF16)
    o_ref[...] = jnp.dot(hb, wb_ref[:, 0:F32_COLS], preferred_element_type=F32)
    of_ref[...] = jnp.dot(hb, wb_ref[:, F32_COLS:IN_COLS], preferred_element_type=F32).astype(BF16)


def _inproj(xs, gain, mod4, w_in, layer):
    return pl.pallas_call(
        functools.partial(_inproj_body, len(xs)),
        grid=(T_ALL // TM,),
        in_specs=_x_specs(len(xs)) + [pl.BlockSpec((1, D), lambda i: (0, 0)),
                                      _mod_spec(1, TM), _mod_spec(0, TM),
                                      _resident((D, IN_COLS), layer)],
        out_specs=[pl.BlockSpec((TM, F32_COLS), lambda i: (i, 0)),
                   pl.BlockSpec((TM, IN_COLS - F32_COLS), lambda i: (i, 0))],
        out_shape=[jax.ShapeDtypeStruct((T_ALL, F32_COLS), F32),
                   jax.ShapeDtypeStruct((T_ALL, IN_COLS - F32_COLS), BF16)],
        scratch_shapes=[pltpu.VMEM((D, IN_COLS), BF16)],
        compiler_params=_cparams(("arbitrary",), 56),
        name="in_proj",
    )(*xs, gain, mod4, mod4, w_in)


def _lane_lo():
    return lax.broadcasted_iota(jnp.int32, (1, LANES), 1) < HEAD_DIM


def _head_norm(x, gain):
    lo = _lane_lo()
    x2 = x * x
    s_lo = jnp.sum(jnp.where(lo, x2, 0.0), axis=-1, keepdims=True)
    s_hi = jnp.sum(jnp.where(lo, 0.0, x2), axis=-1, keepdims=True)
    r = jnp.where(lo, lax.rsqrt(s_lo * (1.0 / HEAD_DIM) + EPS), lax.rsqrt(s_hi * (1.0 / HEAD_DIM) + EPS))
    return x * r * gain


def _rope(x, cos, sin_signed):
    even = (lax.broadcasted_iota(jnp.int32, (1, LANES), 1) % 2) == 0
    swapped = jnp.where(even, pltpu.roll(x, LANES - 1, axis=1), pltpu.roll(x, 1, axis=1))
    return x * cos + swapped * sin_signed


def _split_heads(x, g):
    lo = _lane_lo()
    own = jnp.where(lo if g == 0 else jnp.logical_not(lo), x, 0.0)
    other = pltpu.roll(own, HEAD_DIM, axis=1)
    return (own, other) if g == 0 else (other, own)


Q_SCALE = HEAD_DIM ** -0.5 * 1.4426950408889634


def _ones_cols(rows, lk):
    r = lax.broadcasted_iota(jnp.int32, (rows, LANES), 0)
    lane = lax.broadcasted_iota(jnp.int32, (rows, LANES), 1)
    return jnp.where(((lane == 0) & (r < lk)) | ((lane == 1) & (r >= lk)), 1.0, 0.0)


def _scores(q_bf, k_cat):
    return lax.dot_general(q_bf, k_cat, (((1,), (1,)), ((), ())), preferred_element_type=F32)


def _attend_tile(q_bf, k_cat, v_cat, lk):
    return _softmax_pv(_scores(q_bf, k_cat), v_cat, lk)


def _softmax_pv(s, v_cat, lk):
    s_lo, s_hi = s[:, :lk], s[:, lk:]
    p_lo = jnp.exp2(s_lo - jnp.max(s_lo, axis=-1, keepdims=True))
    p_hi = jnp.exp2(s_hi - jnp.max(s_hi, axis=-1, keepdims=True))
    p = jnp.concatenate([p_lo, p_hi], axis=1).astype(BF16)
    o = jnp.dot(p, v_cat, preferred_element_type=F32)
    l_lo, l_hi = o[:, LANES:LANES + 1], o[:, LANES + 1:LANES + 2]
    return o[:, :LANES] * jnp.where(_lane_lo(), 1.0 / l_lo, 1.0 / l_hi)


def _ctx_attn_body(q_ref, k_ref, v_ref, qg_ref, kg_ref, kc_in, vc_in, o_ref, kc_ref, vc_ref):
    del kc_in, vc_in
    ones = _ones_cols(2 * SEQ, SEQ)
    for s in range(CTX_G):
        rows = slice(s * SEQ, (s + 1) * SEQ)
        kn = _head_norm(k_ref[rows, :], kg_ref[...])
        v = v_ref[rows, :]
        kc_ref[s] = kn
        vc_ref[s] = v
        for g in range(2):
            k_lo, k_hi = _split_heads(kn, g)
            v_lo, v_hi = _split_heads(v, g)
            k_cat = jnp.concatenate([k_lo, k_hi], axis=0).astype(BF16)
            v_cat = jnp.concatenate([jnp.concatenate([v_lo, v_hi], axis=0), ones], axis=1).astype(BF16)
            for tt in range(2):
                t = 2 * g + tt
                q = _head_norm(q_ref[rows, t * LANES:(t + 1) * LANES], qg_ref[...]) * Q_SCALE
                o = _attend_tile(q.astype(BF16), k_cat, v_cat, SEQ)
                o_ref[rows, t * LANES:(t + 1) * LANES] = o.astype(BF16)


def _ctx_attn(proj, qg, kg, kcache, vcache, layer):
    G = CTX_G
    return pl.pallas_call(
        _ctx_attn_body,
        grid=(BATCH // G,),
        in_specs=[pl.BlockSpec((G * SEQ, ATTN_CH), lambda b: (b, COL_Q // ATTN_CH)),
                  pl.BlockSpec((G * SEQ, KV_CH), lambda b: (b, COL_K // KV_CH)),
                  pl.BlockSpec((G * SEQ, KV_CH), lambda b: (b, COL_V // KV_CH)),
                  pl.BlockSpec((1, LANES), lambda b: (0, 0)),
                  pl.BlockSpec((1, LANES), lambda b: (0, 0)),
                  pl.BlockSpec(memory_space=pl.ANY),
                  pl.BlockSpec(memory_space=pl.ANY)],
        out_specs=[pl.BlockSpec((G * SEQ, ATTN_CH), lambda b: (b, 0)),
                   pl.BlockSpec((G, None, SEQ, KV_CH), lambda b: (b, layer, 0, 0)),
                   pl.BlockSpec((G, None, SEQ, KV_CH), lambda b: (b, layer, 0, 0))],
        out_shape=[jax.ShapeDtypeStruct((T_CTX, ATTN_CH), BF16),
                   jax.ShapeDtypeStruct((BATCH, DEPTH, SEQ, KV_CH), F32),
                   jax.ShapeDtypeStruct((BATCH, DEPTH, SEQ, KV_CH), F32)],
        input_output_aliases={5: 1, 6: 2},
        compiler_params=_cparams(("parallel",), 32),
        name="ctx_attn",
    )(proj, proj, proj, qg, kg, kcache, vcache)


def _lat_attn_body(q_ref, k_ref, v_ref, ck_ref, cv_ref, qg_ref, kg_ref, cosk_ref, sink_ref, cosq_ref, sinq_ref,
                   o_ref, kcat_ref, vcat_ref):
    @pl.when(pl.program_id(1) == 0)
    def _():
        rows = 256
        for c in range(LK // rows):
            if c == 0:
                kn = ck_ref[...]
                v = cv_ref[...]
            else:
                r0 = (c - 1) * rows
                kn = _head_norm(k_ref[r0:r0 + rows, :], kg_ref[...])
                kn = _rope(kn, cosk_ref[r0:r0 + rows, :], sink_ref[r0:r0 + rows, :])
                v = v_ref[r0:r0 + rows, :]
            for g in range(2):
                k_lo, k_hi = _split_heads(kn, g)
                v_lo, v_hi = _split_heads(v, g)
                kcat_ref[g, c * rows:(c + 1) * rows, :] = k_lo.astype(BF16)
                kcat_ref[g, LK + c * rows:LK + (c + 1) * rows, :] = k_hi.astype(BF16)
                vcat_ref[g, c * rows:(c + 1) * rows, 0:LANES] = v_lo.astype(BF16)
                vcat_ref[g, LK + c * rows:LK + (c + 1) * rows, 0:LANES] = v_hi.astype(BF16)
                vcat_ref[g, c * rows:(c + 1) * rows, LANES:2 * LANES] = _ones_cols(rows, rows).astype(BF16)
                vcat_ref[g, LK + c * rows:LK + (c + 1) * rows, LANES:2 * LANES] = _ones_cols(rows, 0).astype(BF16)

    def scores(t):
        q = _head_norm(q_ref[:, t * LANES:(t + 1) * LANES], qg_ref[...])
        q = _rope(q, cosq_ref[...], sinq_ref[...]) * Q_SCALE
        return _scores(q.astype(BF16), kcat_ref[t // 2])

    nt = ATTN_CH // LANES
    s = scores(0)
    for t in range(nt):
        s_next = scores(t + 1) if t + 1 < nt else None
        o = _softmax_pv(s, vcat_ref[t // 2], LK)
        o_ref[:, t * LANES:(t + 1) * LANES] = o.astype(BF16)
        s = s_next


def _lat_attn(proj, cache_k4, cache_v4, qg, kg, cos, sin, layer):
    nqb = DEC_SEQ // TQ
    row0 = T_CTX // DEC_SEQ
    return pl.pallas_call(
        _lat_attn_body,
        grid=(DEC_BATCH, nqb),
        in_specs=[pl.BlockSpec((TQ, ATTN_CH), lambda b, i: (T_CTX // TQ + b * nqb + i, COL_Q // ATTN_CH)),
                  pl.BlockSpec((DEC_SEQ, KV_CH), lambda b, i: (row0 + b, COL_K // KV_CH)),
                  pl.BlockSpec((DEC_SEQ, KV_CH), lambda b, i: (row0 + b, COL_V // KV_CH)),
                  pl.BlockSpec((None, None, PAST_LEN, KV_CH), lambda b, i: (b, layer, 0, 0)),
                  pl.BlockSpec((None, None, PAST_LEN, KV_CH), lambda b, i: (b, layer, 0, 0)),
                  pl.BlockSpec((1, LANES), lambda b, i: (0, 0)),
                  pl.BlockSpec((1, LANES), lambda b, i: (0, 0)),
                  pl.BlockSpec((DEC_SEQ, LANES), lambda b, i: (0, 0)),
                  pl.BlockSpec((DEC_SEQ, LANES), lambda b, i: (0, 0)),
                  pl.BlockSpec((TQ, LANES), lambda b, i: (i, 0)),
                  pl.BlockSpec((TQ, LANES), lambda b, i: (i, 0))],
        out_specs=pl.BlockSpec((TQ, ATTN_CH), lambda b, i: (b * nqb + i, 0)),
        out_shape=jax.ShapeDtypeStruct((T_LAT, ATTN_CH), BF16),
        scratch_shapes=[pltpu.VMEM((2, 2 * LK, LANES), BF16), pltpu.VMEM((2, 2 * LK, 2 * LANES), BF16)],
        compiler_params=_cparams(("parallel", "arbitrary"), 56),
        name="lat_attn",
    )(proj, proj, proj, cache_k4, cache_v4, qg, kg, cos, sin, cos, sin)


def _conv_body(L, a_ref, g_ref, w_ref, b_ref, lg_ref, lb_ref, o_ref, pad_ref, y_ref):
    rows = CONV_ROWS
    zeros = jnp.zeros((PAD, CONV_CH), F32)
    pad_ref[0:PAD, :] = zeros
    pad_ref[PAD + L:2 * PAD + L, :] = zeros
    pad_ref[PAD:PAD + L, :] = a_ref[...] * jax.nn.sigmoid(g_ref[...])
    shift0 = PAD - CONV_WIDTH // 2

    def chunk(c, carry):
        base = pl.multiple_of(c * rows, rows)
        for lg in range(CONV_CH // LANES):
            lanes = slice(lg * LANES, (lg + 1) * LANES)
            y = jnp.zeros((rows, LANES), F32)
            for r in range(SUBLANES):
                z = None
                for q in range((CONV_WIDTH + shift0) // SUBLANES + 1):
                    j = SUBLANES * q + r - shift0
                    if 0 <= j < CONV_WIDTH:
                        term = w_ref[j:j + 1, lanes] * pad_ref[pl.ds(base + SUBLANES * q, rows + SUBLANES), lanes]
                        z = term if z is None else z + term
                y = y + z[r:r + rows, :]
            y_ref[:, lanes] = y + b_ref[:, lanes]
        half = rows // 2
        for p in range(2):
            y = y_ref[p * half:(p + 1) * half, :]
            mu = jnp.mean(y, axis=-1, keepdims=True)
            yc = y - mu
            var = jnp.mean(yc * yc, axis=-1, keepdims=True)
            yn = yc * lax.rsqrt(var + EPS) * lg_ref[...] + lb_ref[...]
            o_ref[pl.ds(base + p * half, half), :] = _silu(yn).astype(BF16)
        return carry

    lax.fori_loop(0, L // rows, chunk, 0)


def _conv(proj, w32, b, lg, lb, L, nseq, row0):
    vec = pl.BlockSpec((1, CONV_CH), lambda s: (0, 0))
    return pl.pallas_call(
        functools.partial(_conv_body, L),
        grid=(nseq,),
        in_specs=[pl.BlockSpec((L, CONV_CH), lambda s: (row0 + s, COL_CA // CONV_CH)),
                  pl.BlockSpec((L, CONV_CH), lambda s: (row0 + s, COL_CG // CONV_CH)),
                  pl.BlockSpec((32, CONV_CH), lambda s: (0, 0)), vec, vec, vec],
        out_specs=pl.BlockSpec((L, CONV_CH), lambda s: (s, 0)),
        out_shape=jax.ShapeDtypeStruct((nseq * L, CONV_CH), BF16),
        scratch_shapes=[pltpu.VMEM((L + 2 * PAD, CONV_CH), F32), pltpu.VMEM((CONV_ROWS, CONV_CH), F32)],
        compiler_params=_cparams(("parallel",), 48),
        name="conv_L%d" % L,
    )(proj, proj, w32, b, lg, lb)


def _pool_body(L, G, u_ref, w_ref, sc_ref, o_ref, pad_ref):
    rows = 256
    zeros = jnp.zeros((PAD, 512), F32)
    pad_ref[0:PAD, :] = zeros
    pad_ref[PAD + L:2 * PAD + L, :] = zeros
    for s in range(G):
        pad_ref[PAD:PAD + L, :] = u_ref[s * L:(s + 1) * L, :]
        for c in range(L // rows):
            r0 = c * rows
            t = lax.broadcasted_iota(jnp.int32, (rows, 1), 0) + r0
            for g, w in enumerate(POOL_WINDOWS):
                cols = slice(g * POOL_GROUP_CH, (g + 1) * POOL_GROUP_CH)
                acc = jnp.zeros((rows, POOL_GROUP_CH), F32)
                for i in range(-(w // 2), w - w // 2):
                    acc = acc + pad_ref[PAD + r0 + i:PAD + r0 + i + rows, cols]
                lo = jnp.maximum(t - w // 2, 0)
                hi = jnp.minimum(t + (w - w // 2), L)
                cnt = (hi - lo).astype(F32)
                pooled = acc / cnt - pad_ref[PAD + r0:PAD + r0 + rows, cols]
                mixed = jnp.dot(pooled.astype(BF16), w_ref[g], preferred_element_type=F32)
                o_ref[s * L + r0:s * L + r0 + rows, cols] = (mixed * sc_ref[:, cols]).astype(BF16)


def _pool(proj, pw_bf, pscale, L, nseq, row0, G):
    return pl.pallas_call(
        functools.partial(_pool_body, L, G),
        grid=(nseq // G,),
        in_specs=[pl.BlockSpec((G * L, 512), lambda s: (row0 + s, COL_POOL // 512)),
                  pl.BlockSpec((4, POOL_GROUP_CH, POOL_GROUP_CH), lambda s: (0, 0, 0)),
                  pl.BlockSpec((1, 512), lambda s: (0, 0))],
        out_specs=pl.BlockSpec((G * L, 512), lambda s: (s, 0)),
        out_shape=jax.ShapeDtypeStruct((nseq * L, 512), BF16),
        scratch_shapes=[pltpu.VMEM((L + 2 * PAD, 512), F32)],
        compiler_params=_cparams(("parallel",), 48),
        name="pool_L%d" % L,
    )(proj, pw_bf, pscale)


def _channel_dft(u, wc_ref, z_ref, r0, L):
    rows = u.shape[0]
    for h in range(FOURIER_HEADS):
        cols = slice(h * FOURIER_HEAD_CH, (h + 1) * FOURIER_HEAD_CH)
        a = jnp.dot(u[:, cols].astype(BF16), wc_ref[...], preferred_element_type=F32)
        z_ref[r0:r0 + rows, cols] = a[:, :FOURIER_HEAD_CH].astype(BF16)
        z_ref[L + r0:L + r0 + rows, cols] = a[:, FOURIER_HEAD_CH:].astype(BF16)


def _four_body(L, G, u_ref, wc_ref, wl_ref, o_ref, z_ref):
    scale = (L * FOURIER_HEAD_CH) ** -0.5
    if wl_ref.shape[0] == L:
        for s in range(G):
            rows = slice(s * L, (s + 1) * L)
            _channel_dft(u_ref[rows, :], wc_ref, z_ref, 0, L)
            o_ref[rows, :] = (jnp.dot(wl_ref[...], z_ref[...], preferred_element_type=F32) * scale).astype(BF16)
    else:
        @pl.when(pl.program_id(1) == 0)
        def _():
            rows = 512
            for c in range(L // rows):
                _channel_dft(u_ref[c * rows:(c + 1) * rows, :], wc_ref, z_ref, c * rows, L)

        o_ref[...] = (jnp.dot(wl_ref[...], z_ref[...], preferred_element_type=F32) * scale).astype(BF16)


def _four(pf, wc_bf, wl_bf, L, nseq, row0, G):
    tr = min(L, 512)
    return pl.pallas_call(
        functools.partial(_four_body, L, G),
        grid=(nseq // G, L // tr),
        in_specs=[pl.BlockSpec((G * L, 512), lambda s, r: (row0 + s, 0)),
                  pl.BlockSpec((FOURIER_HEAD_CH, 2 * FOURIER_HEAD_CH), lambda s, r: (0, 0)),
                  pl.BlockSpec((tr, 2 * L), lambda s, r: (r, 0))],
        out_specs=pl.BlockSpec((G * tr, 512), lambda s, r: (s * (L // tr) + r, 0)),
        out_shape=jax.ShapeDtypeStruct((nseq * L, 512), BF16),
        scratch_shapes=[pltpu.VMEM((2 * L, 512), BF16)],
        compiler_params=_cparams(("parallel", "arbitrary"), 48),
        name="four_L%d" % L,
    )(pf, wc_bf, wl_bf)


def _route(logits):
    lane = lax.broadcasted_iota(jnp.int32, logits.shape, 1).astype(F32)
    neg = jnp.float32(-jnp.inf)
    big = jnp.float32(1 << 20)
    is_g = (lane >= N_EXPERTS) & (lane < N_EXPERTS + N_EXPERT_GROUPS)
    gl = jnp.where(is_g, logits, neg)
    gmax = jnp.max(gl, axis=-1, keepdims=True)
    g_idx = jnp.min(jnp.where(gl == gmax, lane, big), axis=-1, keepdims=True) - N_EXPERTS
    denom = jnp.sum(jnp.where(is_g, jnp.exp(gl - gmax), 0.0), axis=-1, keepdims=True)
    g_w = 1.0 / denom
    in_grp = (lane >= g_idx * EXPERTS_PER_GROUP) & (lane < (g_idx + 1) * EXPERTS_PER_GROUP)
    el = jnp.where(in_grp, logits, neg)
    v1 = jnp.max(el, axis=-1, keepdims=True)
    i1 = jnp.min(jnp.where(el == v1, lane, big), axis=-1, keepdims=True)
    el2 = jnp.where(lane == i1, neg, el)
    v2 = jnp.max(el2, axis=-1, keepdims=True)
    i2 = jnp.min(jnp.where(el2 == v2, lane, big), axis=-1, keepdims=True)
    e2 = jnp.exp(v2 - v1)
    p1 = 1.0 / (1.0 + e2)
    p2 = e2 / (1.0 + e2)
    gates = g_w * jnp.where(lane == i1, p1, jnp.where(lane == i2, p2, 0.0))
    return gates + jnp.where(lane == g_idx + N_EXPERTS, 1.0, 0.0)


def _pair_specs(width):
    n = T_CTX // TM
    return [pl.BlockSpec((TM, width), lambda i: (jnp.minimum(i, n - 1), 0)),
            pl.BlockSpec((TM, width), lambda i: (jnp.maximum(i - n, 0), 0))]


def _outproj_body(nx, *refs):
    mixers = refs[nx:nx + 8]
    (w_ref, g1_ref, gain_ref, sc_ref, sh_ref, wr_ref, br_ref, xo_ref, hg_ref, route_ref, wb_ref) = refs[nx + 8:]

    @pl.when(pl.program_id(0) == 0)
    def _():
        wb_ref[...] = w_ref[...].astype(BF16)

    for c in range(TM // OUT_ROWS):
        rows = slice(c * OUT_ROWS, (c + 1) * OUT_ROWS)
        mix = None
        for k in range(4):
            part = jnp.dot(_load_x(mixers[2 * k:2 * k + 2], rows), wb_ref[512 * k:512 * (k + 1), :],
                           preferred_element_type=F32)
            mix = part if mix is None else mix + part
        x = _load_x(refs[:nx], rows) + g1_ref[...] * mix
        xo_ref[rows, :] = x
        ms = jnp.mean(x * x, axis=-1, keepdims=True)
        h = x * lax.rsqrt(ms + EPS) * gain_ref[...]
        h = h * (1.0 + sc_ref[...]) + sh_ref[...]
        h_hi = h.astype(BF16)
        h_lo = (h - h_hi.astype(F32)).astype(BF16)
        t = (jnp.dot(h_hi, wr_ref[...], preferred_element_type=F32)
             + jnp.dot(h_lo, wr_ref[...], preferred_element_type=F32))
        logits = t + pltpu.roll(t, LANES - ROUTER_LO_LANE, axis=1) + br_ref[...]
        route = _route(logits)
        route_ref[rows, :] = route
        rec0 = c * OUT_ROWS * REC
        for s in range(H_ROWS):
            hg_ref[pl.ds(rec0 + s, OUT_ROWS, stride=REC), :] = h[:, s * LANES:(s + 1) * LANES]


def _outproj(mixers, xs, w_out, mod4, gain, wr, br, layer):
    tile = lambda w: pl.BlockSpec((TM, w), lambda i: (i, 0))
    const = lambda r, c: pl.BlockSpec((r, c), lambda i: (0, 0))
    mix_specs = []
    for _ in range(4):
        mix_specs += _pair_specs(512)
    return pl.pallas_call(
        functools.partial(_outproj_body, len(xs)),
        grid=(T_ALL // TM,),
        in_specs=_x_specs(len(xs)) + mix_specs + [_resident((2048, D), layer),
                                                  _mod_spec(2, TM), const(1, D), _mod_spec(4, TM), _mod_spec(3, TM),
                                                  const(D, LANES), const(1, LANES)],
        out_specs=[tile(D), pl.BlockSpec((TM * REC, LANES), lambda i: (i, 0)), tile(LANES)],
        out_shape=[jax.ShapeDtypeStruct((T_ALL, D), F32),
                   jax.ShapeDtypeStruct((T_ALL * REC, LANES), F32),
                   jax.ShapeDtypeStruct((T_ALL, LANES), F32)],
        scratch_shapes=[pltpu.VMEM((2048, D), BF16)],
        compiler_params=_cparams(("arbitrary",), 56),
        name="out_proj",
    )(*xs, *[a for pair in mixers for a in pair], w_out, mod4, gain, mod4, mod4, wr, br)


def _rank_body(r_ref, o_ref, cnt_ref, carry_ref):
    @pl.when(pl.program_id(0) == 0)
    def _():
        carry_ref[...] = jnp.zeros_like(carry_ref)

    r = r_ref[...]
    lane = lax.broadcasted_iota(jnp.int32, r.shape, 1)
    is_g = (lane >= N_EXPERTS) & (lane < N_EXPERTS + N_EXPERT_GROUPS)
    onehot = jnp.where(is_g, r, 0.0)
    row = lax.broadcasted_iota(jnp.int32, (RT, RT), 0)
    col = lax.broadcasted_iota(jnp.int32, (RT, RT), 1)
    earlier = jnp.where(col < row, 1.0, 0.0).astype(BF16)
    before = jnp.dot(earlier, onehot.astype(BF16), preferred_element_type=F32) + carry_ref[...]
    rank = jnp.sum(onehot * before, axis=-1, keepdims=True)
    grp = jnp.sum(onehot * (lane - N_EXPERTS).astype(F32), axis=-1, keepdims=True)
    packed = jnp.where(lane == 0, rank, jnp.where(lane == 1, grp, 0.0))
    o_ref[...] = packed.T[0:SUBLANES, :].astype(jnp.int32)
    carry_ref[...] += jnp.sum(onehot, axis=0, keepdims=True)
    cnt_ref[...] = carry_ref[...]


def _rank(route):
    return pl.pallas_call(
        _rank_body,
        grid=(T_ALL // RT,),
        in_specs=[pl.BlockSpec((RT, LANES), lambda i: (i, 0))],
        out_specs=[pl.BlockSpec((SUBLANES, RT), lambda i: (i, 0)),
                   pl.BlockSpec((1, LANES), lambda i: (0, 0))],
        out_shape=[jax.ShapeDtypeStruct((T_ALL // RT * SUBLANES, RT), jnp.int32),
                   jax.ShapeDtypeStruct((1, LANES), F32)],
        scratch_shapes=[pltpu.VMEM((1, LANES), F32)],
        compiler_params=_cparams(("arbitrary",), 32),
        name="moe_rank",
    )(route)


def _wait_rows(copy, n):
    def body(t, c):
        copy.wait()
        return c
    lax.fori_loop(0, n, body, 0, unroll=8)


def _record(ref, idx, rows):
    start = idx * rows if isinstance(idx, int) else pl.multiple_of(idx * rows, rows)
    return ref.at[pl.ds(start, rows)]


def _dispatch_body(slot_sm, fstart_sm, flen_sm, hg_ref, route_ref, xs_hbm, gs_ref, zrec_ref, sem):
    i = pl.program_id(0)
    base = i * DT

    @pl.when(i == 0)
    def _():
        gs_ref[...] = jnp.zeros_like(gs_ref)
        zrec_ref[...] = jnp.zeros_like(zrec_ref)
        for g in range(N_EXPERT_GROUPS):
            def fill(r, c):
                pltpu.make_async_copy(zrec_ref, _record(xs_hbm, fstart_sm[g] + r, REC), sem.at[1]).start()
                return c
            lax.fori_loop(0, flen_sm[g], fill, 0)
        _wait_rows(pltpu.make_async_copy(zrec_ref, _record(xs_hbm, 0, REC), sem.at[1]), NSLOT - T_ALL)

    def issue(t, c):
        slot = slot_sm[base + t]
        pltpu.make_async_copy(_record(hg_ref, t, REC), _record(xs_hbm, slot, REC), sem.at[0]).start()
        gs_ref[pl.ds(slot, 1), :] = route_ref[pl.ds(t, 1), :]
        return c
    lax.fori_loop(0, DT, issue, 0, unroll=8)
    _wait_rows(pltpu.make_async_copy(_record(hg_ref, 0, REC), _record(xs_hbm, 0, REC), sem.at[0]), DT)


def _dispatch(slot, fstart, flen, hg, route):
    return pl.pallas_call(
        _dispatch_body,
        grid_spec=pltpu.PrefetchScalarGridSpec(
            num_scalar_prefetch=3,
            grid=(T_ALL // DT,),
            in_specs=[pl.BlockSpec((DT * REC, LANES), lambda i, s, f0, f1: (i, 0)),
                      pl.BlockSpec((DT, LANES), lambda i, s, f0, f1: (i, 0))],
            out_specs=[pl.BlockSpec(memory_space=pl.ANY),
                       pl.BlockSpec((NSLOT, LANES), lambda i, s, f0, f1: (0, 0), pipeline_mode=pl.Buffered(1))],
            scratch_shapes=[pltpu.VMEM((REC, LANES), F32), pltpu.SemaphoreType.DMA((2,))]),
        out_shape=[jax.ShapeDtypeStruct((NSLOT * REC, LANES), F32),
                   jax.ShapeDtypeStruct((NSLOT, LANES), F32)],
        compiler_params=_cparams(("arbitrary",), 48),
        name="moe_dispatch",
    )(slot, fstart, flen, hg, route)


def _ffn_body(tgrp_sm, xs_ref, gs_ref, wg32_ref, wu32_ref, wd32_ref, ys_ref, h_ref, wg_ref, wu_ref, wd_ref):
    j = pl.program_id(0)
    grp = tgrp_sm[j]

    @pl.when(jnp.logical_or(j == 0, grp != tgrp_sm[jnp.maximum(j - 1, 0)]))
    def _():
        for e in range(EXPERTS_PER_GROUP):
            cols = slice(e * EXPERT_FF, (e + 1) * EXPERT_FF)
            wg_ref[:, cols] = wg32_ref[e].astype(BF16)
            wu_ref[:, cols] = wu32_ref[e].astype(BF16)
            wd_ref[cols, :] = wd32_ref[e].astype(BF16)

    for s in range(H_ROWS):
        h_ref[:, s * LANES:(s + 1) * LANES] = xs_ref[pl.ds(s, TS, stride=REC), :].astype(BF16)
    gates = gs_ref[...]
    h = h_ref[...]
    a = jnp.dot(h, wg_ref[...], preferred_element_type=F32)
    b = jnp.dot(h, wu_ref[...], preferred_element_type=F32)
    hid = _silu(a) * b
    lane = lax.broadcasted_iota(jnp.int32, gates.shape, 1)
    parts = []
    for e in range(EXPERTS_PER_GROUP):
        ge = jnp.sum(jnp.where(lane == grp * EXPERTS_PER_GROUP + e, gates, 0.0), axis=-1, keepdims=True)
        parts.append((hid[:, e * EXPERT_FF:(e + 1) * EXPERT_FF] * ge).astype(BF16))
    y = jnp.dot(jnp.concatenate(parts, axis=1), wd_ref[...], preferred_element_type=F32)
    for s in range(H_ROWS):
        ys_ref[pl.ds(s, TS, stride=H_ROWS), :] = y[:, s * LANES:(s + 1) * LANES]


def _ffn(tile_grp, xs, gs, w_gate, w_up, w_down, layer):
    ffw = EXPERTS_PER_GROUP * EXPERT_FF
    experts = lambda r, c: pl.BlockSpec((None, EXPERTS_PER_GROUP, r, c), lambda j, tg: (layer, tg[j], 0, 0),
                                        pipeline_mode=pl.Buffered(1))
    return pl.pallas_call(
        _ffn_body,
        grid_spec=pltpu.PrefetchScalarGridSpec(
            num_scalar_prefetch=1,
            grid=(NT,),
            in_specs=[pl.BlockSpec((TS * REC, LANES), lambda j, tg: (j, 0)),
                      pl.BlockSpec((TS, LANES), lambda j, tg: (j, 0)),
                      experts(D, EXPERT_FF), experts(D, EXPERT_FF), experts(EXPERT_FF, D)],
            out_specs=pl.BlockSpec((TS * H_ROWS, LANES), lambda j, tg: (j, 0)),
            scratch_shapes=[pltpu.VMEM((TS, D), BF16), pltpu.VMEM((D, ffw), BF16),
                            pltpu.VMEM((D, ffw), BF16), pltpu.VMEM((ffw, D), BF16)]),
        out_shape=jax.ShapeDtypeStruct((NSLOT * H_ROWS, LANES), F32),
        compiler_params=_cparams(("arbitrary",), 56),
        name="moe_ffn",
    )(tile_grp, xs, gs, w_gate, w_up, w_down)


def _combine_body(final, slot_sm, ys_hbm, x_ref, g2_ref, fg_ref, *rest):
    outs, buf, sem = rest[:-2], rest[-2], rest[-1]
    i = pl.program_id(0)
    n = pl.num_programs(0)

    def issue(step, b):
        def body(t, c):
            pltpu.make_async_copy(_record(ys_hbm, slot_sm[step * TM + t], H_ROWS),
                                  _record(buf.at[b], t, H_ROWS), sem.at[b]).start()
            return c
        lax.fori_loop(0, TM, body, 0, unroll=8)

    @pl.when(i == 0)
    def _():
        issue(0, 0)

    @pl.when(i + 1 < n)
    def _():
        issue(i + 1, (i + 1) % 2)

    b = i % 2
    _wait_rows(pltpu.make_async_copy(_record(ys_hbm, 0, H_ROWS), _record(buf.at[b], 0, H_ROWS), sem.at[b]), TM)
    y = jnp.concatenate([buf[b, pl.ds(s, TM, stride=H_ROWS), :] for s in range(H_ROWS)], axis=1)
    x = x_ref[...] + g2_ref[...] * y
    if not final:
        outs[0][...] = x
    else:
        ms = jnp.mean(x * x, axis=-1, keepdims=True)
        yn = x * lax.rsqrt(ms + EPS) * fg_ref[...]
        is_ctx = i < T_CTX // TM

        @pl.when(is_ctx)
        def _():
            outs[0][...] = yn

        @pl.when(jnp.logical_not(is_ctx))
        def _():
            outs[1][...] = yn


def _combine(slot, ys, x, mod4, final_gain, final):
    if final:
        nc = T_CTX // TM
        out_specs = [pl.BlockSpec((TM, D), lambda i, s: (jnp.minimum(i, nc - 1), 0)),
                     pl.BlockSpec((TM, D), lambda i, s: (jnp.maximum(i - nc, 0), 0))]
        out_shape = [jax.ShapeDtypeStruct((T_CTX, D), F32), jax.ShapeDtypeStruct((T_LAT, D), F32)]
    else:
        out_specs = pl.BlockSpec((TM, D), lambda i, s: (i, 0))
        out_shape = jax.ShapeDtypeStruct((T_ALL, D), F32)
    return pl.pallas_call(
        functools.partial(_combine_body, final),
        grid_spec=pltpu.PrefetchScalarGridSpec(
            num_scalar_prefetch=1,
            grid=(T_ALL // TM,),
            in_specs=[pl.BlockSpec(memory_space=pl.ANY),
                      pl.BlockSpec((TM, D), lambda i, s: (i, 0)),
                      pl.BlockSpec((None, None, 1, D), lambda i, s: (_mod_row(i, TM), 5, 0, 0)),
                      pl.BlockSpec((1, D), lambda i, s: (0, 0))],
            out_specs=out_specs,
            scratch_shapes=[pltpu.VMEM((2, TM * H_ROWS, LANES), F32), pltpu.SemaphoreType.DMA((2,))]),
        out_shape=out_shape,
        compiler_params=_cparams(("arbitrary",), 48),
        name="moe_combine",
    )(slot, ys, x, mod4, final_gain)


def _moe_plan(rk, cnt):
    rk = rk.reshape(T_ALL // RT, SUBLANES, RT)
    rank = rk[:, 0, :].reshape(T_ALL)
    grp = rk[:, 1, :].reshape(T_ALL)
    cnt = cnt[0, N_EXPERTS:N_EXPERTS + N_EXPERT_GROUPS].astype(jnp.int32)
    padded = (cnt + TS - 1) // TS * TS
    off = jnp.cumsum(padded) - padded
    padded = padded.at[N_EXPERT_GROUPS - 1].set(NSLOT - off[N_EXPERT_GROUPS - 1])
    slot = rank
    for g in range(1, N_EXPERT_GROUPS):
        slot = slot + jnp.where(grp == g, off[g], 0)
    ends = (off + padded)[:N_EXPERT_GROUPS - 1]
    tile_grp = jnp.sum((jnp.arange(NT, dtype=jnp.int32)[:, None] * TS >= ends[None, :]).astype(jnp.int32), axis=1)
    return slot, tile_grp, off + cnt, padded - cnt


def _rope_tables():
    t = np.arange(DEC_SEQ)
    pos = np.stack([t // GRID_W, t % GRID_W], axis=1).astype(np.float64)
    inv = ROPE_THETA ** (-np.arange(16, dtype=np.float64) / 16.0)
    d = np.arange(LANES) % HEAD_DIM
    pair = d // 2
    ang = pos[:, (pair >= 16).astype(np.int64)] * inv[pair % 16][None, :]
    sign = np.where(d % 2 == 0, -1.0, 1.0)
    return np.cos(ang).astype(np.float32), (np.sin(ang) * sign[None, :]).astype(np.float32)


def _dft_pos(L):
    j = np.arange(L)
    ang = 2.0 * np.pi * ((j[:, None] * j[None, :]) % L) / L
    return np.concatenate([np.cos(ang), -np.sin(ang)], axis=1).astype(np.float32)


def _dft_ch():
    c = np.arange(FOURIER_HEAD_CH)
    ang = 2.0 * np.pi * ((c[:, None] * c[None, :]) % FOURIER_HEAD_CH) / FOURIER_HEAD_CH
    return np.concatenate([np.cos(ang), np.sin(ang)], axis=1).astype(np.float32)


def kernel(x_prompt, x_sample, cache_k, cache_v, c, c_ctx, w_ada, b_ada, norm_mix, norm_ffn, w_in, w_out, q_norm, k_norm, conv_w, conv_b, conv_ln_g, conv_ln_b, pool_w, pool_scale, router_group_w, router_group_b, router_expert_w, router_expert_b, moe_w_gate, moe_w_up, moe_w_down, final_norm):
    xs = (x_prompt.reshape(T_CTX, D), x_sample.reshape(T_LAT, D))
    cond8 =jnp.concatenate([c_ctx[None, :], c, jnp.zeros((3, D), F32)], axis=0)
    mod = _ada(cond8, w_ada, b_ada)

    pw_bf = pool_w.astype(BF16)
    n_route = N_EXPERTS + N_EXPERT_GROUPS
    wr = jnp.concatenate([router_expert_w, router_group_w], axis=-1)
    wr_hi = wr.astype(BF16)
    wr_lo = (wr - wr_hi.astype(F32)).astype(BF16)
    w_router = jnp.concatenate([wr_hi, jnp.zeros((DEPTH, D, ROUTER_LO_LANE - n_route), BF16), wr_lo,
                                jnp.zeros((DEPTH, D, LANES - ROUTER_LO_LANE - n_route), BF16)], axis=-1)
    b_router = jnp.concatenate([router_expert_b, router_group_b,
                                jnp.zeros((DEPTH, LANES - N_EXPERTS - N_EXPERT_GROUPS), F32)], axis=-1)
    conv_w32 = jnp.concatenate([conv_w, jnp.zeros((DEPTH, 1, CONV_CH), F32)], axis=1)

    cos_np, sin_np = _rope_tables()
    cos, sin = jnp.asarray(cos_np), jnp.asarray(sin_np)
    wc_bf = jnp.asarray(_dft_ch()).astype(BF16)
    wl_ctx = jnp.asarray(_dft_pos(SEQ)).astype(BF16)
    wl_lat = jnp.asarray(_dft_pos(DEC_SEQ)).astype(BF16)

    cache_k4 = cache_k.reshape(DEC_BATCH, DEPTH, PAST_LEN, KV_CH)
    cache_v4 = cache_v.reshape(DEC_BATCH, DEPTH, PAST_LEN, KV_CH)
    new_k = jnp.zeros((BATCH, DEPTH, SEQ, KV_CH), F32)
    new_v = jnp.zeros((BATCH, DEPTH, SEQ, KV_CH), F32)
    fgain = final_norm.reshape(1, D)

    for l in range(DEPTH):
        mod4 = mod[l].reshape(8, N_MOD, 1, D)
        proj, pf = _inproj(xs, norm_mix[l].reshape(1, D), mod4, w_in, l)
        qg = jnp.tile(q_norm[l], 2).reshape(1, LANES)
        kg = jnp.tile(k_norm[l], 2).reshape(1, LANES)
        attn_c, new_k, new_v = _ctx_attn(proj, qg, kg, new_k, new_v, l)
        attn_l = _lat_attn(proj, cache_k4, cache_v4, qg, kg, cos, sin, l)
        cb, lg, lb = conv_b[l].reshape(1, -1), conv_ln_g[l].reshape(1, -1), conv_ln_b[l].reshape(1, -1)
        ps = pool_scale[l].reshape(1, -1)
        lat0 = T_CTX // DEC_SEQ
        mixers = [(attn_c, attn_l),
                  (_conv(proj, conv_w32[l], cb, lg, lb, SEQ, BATCH, 0),
                   _conv(proj, conv_w32[l], cb, lg, lb, DEC_SEQ, DEC_BATCH, lat0)),
                  (_pool(proj, pw_bf[l], ps, SEQ, BATCH, 0, MIX_G),
                   _pool(proj, pw_bf[l], ps, DEC_SEQ, DEC_BATCH, lat0, 1)),
                  (_four(pf, wc_bf, wl_ctx, SEQ, BATCH, 0, MIX_G),
                   _four(pf, wc_bf, wl_lat, DEC_SEQ, DEC_BATCH, lat0, 1))]
        x, hg, route = _outproj(mixers, xs, w_out, mod4, norm_ffn[l].reshape(1, D),
                                w_router[l], b_router[l].reshape(1, LANES), l)
        slot, tile_grp, fstart, flen = _moe_plan(*_rank(route))
        xslots, gslots = _dispatch(slot, fstart, flen, hg, route)
        yslots = _ffn(tile_grp, xslots, gslots, moe_w_gate, moe_w_up, moe_w_down, l)
        out = _combine(slot, yslots, x, mod4, fgain, l == DEPTH - 1)
        xs = (out,)

    y_prompt = out[0].reshape(BATCH, SEQ, D)
    y_sample = out[1].reshape(DEC_BATCH, DEC_SEQ, D)
    new_k = new_k.reshape(BATCH, DEPTH, SEQ, 2, HEAD_DIM)
    new_v = new_v.reshape(BATCH, DEPTH, SEQ, 2, HEAD_DIM)
    return (y_prompt, y_sample, new_k, new_v)
```

```python
import functools

import numpy as np
import jax
import jax.numpy as jnp
from jax import lax
from jax.experimental import pallas as pl
from jax.experimental.pallas import tpu as pltpu

F32 = jnp.float32
BF16 = jnp.bfloat16

D = 1024
BATCH = 32
SEQ = 256
DEPTH = 2
DEC_BATCH = 4
DEC_SEQ = 2048
PAST_LEN = 256
GRID_W = 64
HEAD_DIM = 64
ATTN_CH = 512
KV_CH = 128
CONV_CH = 512
CONV_WIDTH = 31
POOL_WINDOWS = (2, 4, 8, 16)
POOL_GROUP_CH = 128
FOURIER_HEAD_CH = 128
FOURIER_HEADS = 4
N_EXPERT_GROUPS = 4
EXPERTS_PER_GROUP = 4
N_EXPERTS = 16
EXPERT_FF = 256
ROPE_THETA = 10000.0
EPS = 1e-6
N_MOD = 6

T_CTX = BATCH * SEQ
T_LAT = DEC_BATCH * DEC_SEQ
T_ALL = T_CTX + T_LAT
LANES = 128
SUBLANES = 8
PAD = 16
CONV_ROWS = 128
ROUTER_LO_LANE = 32
IN_COLS = 2816
COL_Q, COL_CA, COL_CG, COL_POOL, COL_K, COL_V, COL_FOUR = 0, 512, 1024, 1536, 2048, 2176, 2304
F32_COLS = COL_FOUR

TM = 512
OUT_ROWS = 256
RT = 512
DT = 2048
CTX_G = 2
MIX_G = 4
TS = 512
NT = T_ALL // TS + N_EXPERT_GROUPS
NSLOT = NT * TS
H_ROWS = D // LANES
REC = H_ROWS
TQ = 256
LK = PAST_LEN + DEC_SEQ


def _cparams(sem, vmem_mb):
    return pltpu.CompilerParams(dimension_semantics=sem, vmem_limit_bytes=vmem_mb * 1024 * 1024)


def _mod_row(i, tm):
    nctx = T_CTX // tm
    per = DEC_SEQ // tm
    return jnp.where(i < nctx, 0, 1 + (i - nctx) // per)


def _mod_spec(k, tm):
    return pl.BlockSpec((None, None, 1, D), lambda i: (_mod_row(i, tm), k, 0, 0))


def _silu(x):
    return x * jax.nn.sigmoid(x)


def _ada_body(c_ref, w_ref, b_ref, o_ref):
    s = _silu(c_ref[...])
    o_ref[...] = jnp.dot(s.astype(BF16), w_ref[...].astype(BF16), preferred_element_type=F32) + b_ref[...]


def _ada(cond8, w_ada, b_ada):
    tn = 1536
    n = N_MOD * D
    return pl.pallas_call(
        _ada_body,
        grid=(DEPTH, n // tn),
        in_specs=[pl.BlockSpec((8, D), lambda l, j: (0, 0)),
                  pl.BlockSpec((None, D, tn), lambda l, j: (l, 0, j)),
                  pl.BlockSpec((None, 1, tn), lambda l, j: (l, 0, j))],
        out_specs=pl.BlockSpec((None, 8, tn), lambda l, j: (l, 0, j)),
        out_shape=jax.ShapeDtypeStruct((DEPTH, 8, n), F32),
        compiler_params=_cparams(("arbitrary", "arbitrary"), 40),
        name="ada_mod",
    )(cond8, w_ada, b_ada.reshape(DEPTH, 1, n))


def _x_specs(nx):
    if nx == 1:
        return [pl.BlockSpec((TM, D), lambda i: (i, 0))]
    n = T_CTX // TM
    return [pl.BlockSpec((TM, D), lambda i: (jnp.minimum(i, n - 1), 0)),
            pl.BlockSpec((TM, D), lambda i: (jnp.maximum(i - n, 0), 0))]


def _load_x(x_refs, rows=slice(None)):
    if len(x_refs) == 1:
        return x_refs[0][rows, :]
    return jnp.where(pl.program_id(0) < T_CTX // TM, x_refs[0][rows, :], x_refs[1][rows, :])


def _resident(shape, layer):
    return pl.BlockSpec((None,) + shape, lambda *_: (layer,) + (0,) * len(shape), pipeline_mode=pl.Buffered(1))


def _inproj_body(nx, *refs):
    g_ref, sc_ref, sh_ref, w_ref, o_ref, of_ref, wb_ref = refs[nx:]

    @pl.when(pl.program_id(0) == 0)
    def _():
        kv0, kv1 = ATTN_CH, ATTN_CH + 2 * KV_CH
        wb_ref[:, 0:kv0] = w_ref[:, 0:kv0].astype(BF16)
        wb_ref[:, kv0:COL_K] = w_ref[:, kv1:kv1 + COL_K - kv0].astype(BF16)
        wb_ref[:, COL_K:COL_FOUR] = w_ref[:, kv0:kv1].astype(BF16)
        wb_ref[:, COL_FOUR:IN_COLS] = w_ref[:, COL_FOUR:IN_COLS].astype(BF16)

    x = _load_x(refs[:nx])
    ms = jnp.mean(x * x, axis=-1, keepdims=True)
    h = x * lax.rsqrt(ms + EPS) * g_ref[...]
    h = h * (1.0 + sc_ref[...]) + sh_ref[...]
    hb = h.astype(BF16)
    o_ref[...] = jnp.dot(hb, wb_ref[:, 0:F32_COLS], preferred_element_type=F32)
    of_ref[...] = jnp.dot(hb, wb_ref[:, F32_COLS:IN_COLS], preferred_element_type=F32).astype(BF16)


def _inproj(xs, gain, mod4, w_in, layer):
    return pl.pallas_call(
        functools.partial(_inproj_body, len(xs)),
        grid=(T_ALL // TM,),
        in_specs=_x_specs(len(xs)) + [pl.BlockSpec((1, D), lambda i: (0, 0)),
                                      _mod_spec(1, TM), _mod_spec(0, TM),
                                      _resident((D, IN_COLS), layer)],
        out_specs=[pl.BlockSpec((TM, F32_COLS), lambda i: (i, 0)),
                   pl.BlockSpec((TM, IN_COLS - F32_COLS), lambda i: (i, 0))],
        out_shape=[jax.ShapeDtypeStruct((T_ALL, F32_COLS), F32),
                   jax.ShapeDtypeStruct((T_ALL, IN_COLS - F32_COLS), BF16)],
        scratch_shapes=[pltpu.VMEM((D, IN_COLS), BF16)],
        compiler_params=_cparams(("arbitrary",), 56),
        name="in_proj",
    )(*xs, gain, mod4, mod4, w_in)


def _lane_lo():
    return lax.broadcasted_iota(jnp.int32, (1, LANES), 1) < HEAD_DIM


def _head_norm(x, gain):
    lo = _lane_lo()
    x2 = x * x
    s_lo = jnp.sum(jnp.where(lo, x2, 0.0), axis=-1, keepdims=True)
    s_hi = jnp.sum(jnp.where(lo, 0.0, x2), axis=-1, keepdims=True)
    r = jnp.where(lo, lax.rsqrt(s_lo * (1.0 / HEAD_DIM) + EPS), lax.rsqrt(s_hi * (1.0 / HEAD_DIM) + EPS))
    return x * r * gain


def _rope(x, cos, sin_signed):
    even = (lax.broadcasted_iota(jnp.int32, (1, LANES), 1) % 2) == 0
    swapped = jnp.where(even, pltpu.roll(x, LANES - 1, axis=1), pltpu.roll(x, 1, axis=1))
    return x * cos + swapped * sin_signed


def _split_heads(x, g):
    lo = _lane_lo()
    own = jnp.where(lo if g == 0 else jnp.logical_not(lo), x, 0.0)
    other = pltpu.roll(own, HEAD_DIM, axis=1)
    return (own, other) if g == 0 else (other, own)


Q_SCALE = HEAD_DIM ** -0.5 * 1.4426950408889634


def _ones_cols(rows, lk):
    r = lax.broadcasted_iota(jnp.int32, (rows, LANES), 0)
    lane = lax.broadcasted_iota(jnp.int32, (rows, LANES), 1)
    return jnp.where(((lane == 0) & (r < lk)) | ((lane == 1) & (r >= lk)), 1.0, 0.0)


def _scores(q_bf, k_cat):
    return lax.dot_general(q_bf, k_cat, (((1,), (1,)), ((), ())), preferred_element_type=F32)


def _attend_tile(q_bf, k_cat, v_cat, lk):
    return _softmax_pv(_scores(q_bf, k_cat), v_cat, lk)


def _softmax_pv(s, v_cat, lk):
    s_lo, s_hi = s[:, :lk], s[:, lk:]
    p_lo = jnp.exp2(s_lo - jnp.max(s_lo, axis=-1, keepdims=True))
    p_hi = jnp.exp2(s_hi - jnp.max(s_hi, axis=-1, keepdims=True))
    p = jnp.concatenate([p_lo, p_hi], axis=1).astype(BF16)
    o = jnp.dot(p, v_cat, preferred_element_type=F32)
    l_lo, l_hi = o[:, LANES:LANES + 1], o[:, LANES + 1:LANES + 2]
    return o[:, :LANES] * jnp.where(_lane_lo(), 1.0 / l_lo, 1.0 / l_hi)


def _ctx_attn_body(q_ref, k_ref, v_ref, qg_ref, kg_ref, kc_in, vc_in, o_ref, kc_ref, vc_ref):
    del kc_in, vc_in
    ones = _ones_cols(2 * SEQ, SEQ)
    kv = []
    for s in range(CTX_G):
        rows = slice(s * SEQ, (s + 1) * SEQ)
        kn = _head_norm(k_ref[rows, :], kg_ref[...])
        v = v_ref[rows, :]
        kc_ref[s] = kn
        vc_ref[s] = v
        for g in range(2):
            k_lo, k_hi = _split_heads(kn, g)
            v_lo, v_hi = _split_heads(v, g)
            kv.append((jnp.concatenate([k_lo, k_hi], axis=0).astype(BF16),
                       jnp.concatenate([jnp.concatenate([v_lo, v_hi], axis=0), ones], axis=1).astype(BF16)))
    for t in range(ATTN_CH // LANES):
        for s in range(CTX_G):
            rows = slice(s * SEQ, (s + 1) * SEQ)
            k_cat, v_cat = kv[2 * s + t // 2]
            q = _head_norm(q_ref[rows, t * LANES:(t + 1) * LANES], qg_ref[...]) * Q_SCALE
            o = _attend_tile(q.astype(BF16), k_cat, v_cat, SEQ)
            o_ref[rows, t * LANES:(t + 1) * LANES] = o.astype(BF16)


def _ctx_attn(proj, qg, kg, kcache, vcache, layer):
    G = CTX_G
    return pl.pallas_call(
        _ctx_attn_body,
        grid=(BATCH // G,),
        in_specs=[pl.BlockSpec((G * SEQ, ATTN_CH), lambda b: (b, COL_Q // ATTN_CH)),
                  pl.BlockSpec((G * SEQ, KV_CH), lambda b: (b, COL_K // KV_CH)),
                  pl.BlockSpec((G * SEQ, KV_CH), lambda b: (b, COL_V // KV_CH)),
                  pl.BlockSpec((1, LANES), lambda b: (0, 0)),
                  pl.BlockSpec((1, LANES), lambda b: (0, 0)),
                  pl.BlockSpec(memory_space=pl.ANY),
                  pl.BlockSpec(memory_space=pl.ANY)],
        out_specs=[pl.BlockSpec((G * SEQ, ATTN_CH), lambda b: (b, 0)),
                   pl.BlockSpec((G, None, SEQ, KV_CH), lambda b: (b, layer, 0, 0)),
                   pl.BlockSpec((G, None, SEQ, KV_CH), lambda b: (b, layer, 0, 0))],
        out_shape=[jax.ShapeDtypeStruct((T_CTX, ATTN_CH), BF16),
                   jax.ShapeDtypeStruct((BATCH, DEPTH, SEQ, KV_CH), F32),
                   jax.ShapeDtypeStruct((BATCH, DEPTH, SEQ, KV_CH), F32)],
        input_output_aliases={5: 1, 6: 2},
        compiler_params=_cparams(("parallel",), 32),
        name="ctx_attn",
    )(proj, proj, proj, qg, kg, kcache, vcache)


def _lat_attn_body(q_ref, k_ref, v_ref, ck_ref, cv_ref, qg_ref, kg_ref, cosk_ref, sink_ref, cosq_ref, sinq_ref,
                   o_ref, kcat_ref, vcat_ref):
    @pl.when(pl.program_id(1) == 0)
    def _():
        rows = 256
        for c in range(LK // rows):
            if c == 0:
                kn = ck_ref[...]
                v = cv_ref[...]
            else:
                r0 = (c - 1) * rows
                kn = _head_norm(k_ref[r0:r0 + rows, :], kg_ref[...])
                kn = _rope(kn, cosk_ref[r0:r0 + rows, :], sink_ref[r0:r0 + rows, :])
                v = v_ref[r0:r0 + rows, :]
            for g in range(2):
                k_lo, k_hi = _split_heads(kn, g)
                v_lo, v_hi = _split_heads(v, g)
                kcat_ref[g, c * rows:(c + 1) * rows, :] = k_lo.astype(BF16)
                kcat_ref[g, LK + c * rows:LK + (c + 1) * rows, :] = k_hi.astype(BF16)
                vcat_ref[g, c * rows:(c + 1) * rows, 0:LANES] = v_lo.astype(BF16)
                vcat_ref[g, LK + c * rows:LK + (c + 1) * rows, 0:LANES] = v_hi.astype(BF16)
                vcat_ref[g, c * rows:(c + 1) * rows, LANES:2 * LANES] = _ones_cols(rows, rows).astype(BF16)
                vcat_ref[g, LK + c * rows:LK + (c + 1) * rows, LANES:2 * LANES] = _ones_cols(rows, 0).astype(BF16)

    def scores(t):
        q = _head_norm(q_ref[:, t * LANES:(t + 1) * LANES], qg_ref[...])
        q = _rope(q, cosq_ref[...], sinq_ref[...]) * Q_SCALE
        return _scores(q.astype(BF16), kcat_ref[t // 2])

    nt = ATTN_CH // LANES
    s = scores(0)
    for t in range(nt):
        s_next = scores(t + 1) if t + 1 < nt else None
        o = _softmax_pv(s, vcat_ref[t // 2], LK)
        o_ref[:, t * LANES:(t + 1) * LANES] = o.astype(BF16)
        s = s_next


def _lat_attn(proj, cache_k4, cache_v4, qg, kg, cos, sin, layer):
    nqb = DEC_SEQ // TQ
    row0 = T_CTX // DEC_SEQ
    return pl.pallas_call(
        _lat_attn_body,
        grid=(DEC_BATCH, nqb),
        in_specs=[pl.BlockSpec((TQ, ATTN_CH), lambda b, i: (T_CTX // TQ + b * nqb + i, COL_Q // ATTN_CH)),
                  pl.BlockSpec((DEC_SEQ, KV_CH), lambda b, i: (row0 + b, COL_K // KV_CH)),
                  pl.BlockSpec((DEC_SEQ, KV_CH), lambda b, i: (row0 + b, COL_V // KV_CH)),
                  pl.BlockSpec((None, None, PAST_LEN, KV_CH), lambda b, i: (b, layer, 0, 0)),
                  pl.BlockSpec((None, None, PAST_LEN, KV_CH), lambda b, i: (b, layer, 0, 0)),
                  pl.BlockSpec((1, LANES), lambda b, i: (0, 0)),
                  pl.BlockSpec((1, LANES), lambda b, i: (0, 0)),
                  pl.BlockSpec((DEC_SEQ, LANES), lambda b, i: (0, 0)),
                  pl.BlockSpec((DEC_SEQ, LANES), lambda b, i: (0, 0)),
                  pl.BlockSpec((TQ, LANES), lambda b, i: (i, 0)),
                  pl.BlockSpec((TQ, LANES), lambda b, i: (i, 0))],
        out_specs=pl.BlockSpec((TQ, ATTN_CH), lambda b, i: (b * nqb + i, 0)),
        out_shape=jax.ShapeDtypeStruct((T_LAT, ATTN_CH), BF16),
        scratch_shapes=[pltpu.VMEM((2, 2 * LK, LANES), BF16), pltpu.VMEM((2, 2 * LK, 2 * LANES), BF16)],
        compiler_params=_cparams(("parallel", "arbitrary"), 56),
        name="lat_attn",
    )(proj, proj, proj, cache_k4, cache_v4, qg, kg, cos, sin, cos, sin)


def _conv_body(L, a_ref, g_ref, w_ref, b_ref, lg_ref, lb_ref, o_ref, pad_ref, y_ref):
    rows = CONV_ROWS
    zeros = jnp.zeros((PAD, CONV_CH), F32)
    pad_ref[0:PAD, :] = zeros
    pad_ref[PAD + L:2 * PAD + L, :] = zeros
    pad_ref[PAD:PAD + L, :] = a_ref[...] * jax.nn.sigmoid(g_ref[...])
    shift0 = PAD - CONV_WIDTH // 2

    def chunk(c, carry):
        base = pl.multiple_of(c * rows, rows)
        for lg in range(CONV_CH // LANES):
            lanes = slice(lg * LANES, (lg + 1) * LANES)
            y = jnp.zeros((rows, LANES), F32)
            for r in range(SUBLANES):
                z = None
                for q in range((CONV_WIDTH + shift0) // SUBLANES + 1):
                    j = SUBLANES * q + r - shift0
                    if 0 <= j < CONV_WIDTH:
                        term = w_ref[j:j + 1, lanes] * pad_ref[pl.ds(base + SUBLANES * q, rows + SUBLANES), lanes]
                        z = term if z is None else z + term
                y = y + z[r:r + rows, :]
            y_ref[:, lanes] = y + b_ref[:, lanes]
        half = rows // 2
        for p in range(2):
            y = y_ref[p * half:(p + 1) * half, :]
            mu = jnp.mean(y, axis=-1, keepdims=True)
            yc = y - mu
            var = jnp.mean(yc * yc, axis=-1, keepdims=True)
            yn = yc * lax.rsqrt(var + EPS) * lg_ref[...] + lb_ref[...]
            o_ref[pl.ds(base + p * half, half), :] = _silu(yn).astype(BF16)
        return carry

    lax.fori_loop(0, L // rows, chunk, 0)


def _conv(proj, w32, b, lg, lb, L, nseq, row0):
    vec = pl.BlockSpec((1, CONV_CH), lambda s: (0, 0))
    return pl.pallas_call(
        functools.partial(_conv_body, L),
        grid=(nseq,),
        in_specs=[pl.BlockSpec((L, CONV_CH), lambda s: (row0 + s, COL_CA // CONV_CH)),
                  pl.BlockSpec((L, CONV_CH), lambda s: (row0 + s, COL_CG // CONV_CH)),
                  pl.BlockSpec((32, CONV_CH), lambda s: (0, 0)), vec, vec, vec],
        out_specs=pl.BlockSpec((L, CONV_CH), lambda s: (s, 0)),
        out_shape=jax.ShapeDtypeStruct((nseq * L, CONV_CH), BF16),
        scratch_shapes=[pltpu.VMEM((L + 2 * PAD, CONV_CH), F32), pltpu.VMEM((CONV_ROWS, CONV_CH), F32)],
        compiler_params=_cparams(("parallel",), 48),
        name="conv_L%d" % L,
    )(proj, proj, w32, b, lg, lb)


def _pool_body(L, G, u_ref, w_ref, sc_ref, o_ref, pad_ref):
    rows = 256
    zeros = jnp.zeros((PAD, 512), F32)
    pad_ref[0:PAD, :] = zeros
    pad_ref[PAD + L:2 * PAD + L, :] = zeros
    for s in range(G):
        pad_ref[PAD:PAD + L, :] = u_ref[s * L:(s + 1) * L, :]
        for c in range(L // rows):
            r0 = c * rows
            t = lax.broadcasted_iota(jnp.int32, (rows, 1), 0) + r0
            for g, w in enumerate(POOL_WINDOWS):
                cols = slice(g * POOL_GROUP_CH, (g + 1) * POOL_GROUP_CH)
                acc = jnp.zeros((rows, POOL_GROUP_CH), F32)
                for i in range(-(w // 2), w - w // 2):
                    acc = acc + pad_ref[PAD + r0 + i:PAD + r0 + i + rows, cols]
                lo = jnp.maximum(t - w // 2, 0)
                hi = jnp.minimum(t + (w - w // 2), L)
                cnt = (hi - lo).astype(F32)
                pooled = acc / cnt - pad_ref[PAD + r0:PAD + r0 + rows, cols]
                mixed = jnp.dot(pooled.astype(BF16), w_ref[g], preferred_element_type=F32)
                o_ref[s * L + r0:s * L + r0 + rows, cols] = (mixed * sc_ref[:, cols]).astype(BF16)


def _pool(proj, pw_bf, pscale, L, nseq, row0, G):
    return pl.pallas_call(
        functools.partial(_pool_body, L, G),
        grid=(nseq // G,),
        in_specs=[pl.BlockSpec((G * L, 512), lambda s: (row0 + s, COL_POOL // 512)),
                  pl.BlockSpec((4, POOL_GROUP_CH, POOL_GROUP_CH), lambda s: (0, 0, 0)),
                  pl.BlockSpec((1, 512), lambda s: (0, 0))],
        out_specs=pl.BlockSpec((G * L, 512), lambda s: (s, 0)),
        out_shape=jax.ShapeDtypeStruct((nseq * L, 512), BF16),
        scratch_shapes=[pltpu.VMEM((L + 2 * PAD, 512), F32)],
        compiler_params=_cparams(("parallel",), 48),
        name="pool_L%d" % L,
    )(proj, pw_bf, pscale)


def _channel_dft(u, wc_ref, z_ref, r0, L):
    rows = u.shape[0]
    for h in range(FOURIER_HEADS):
        cols = slice(h * FOURIER_HEAD_CH, (h + 1) * FOURIER_HEAD_CH)
        a = jnp.dot(u[:, cols].astype(BF16), wc_ref[...], preferred_element_type=F32)
        z_ref[r0:r0 + rows, cols] = a[:, :FOURIER_HEAD_CH].astype(BF16)
        z_ref[L + r0:L + r0 + rows, cols] = a[:, FOURIER_HEAD_CH:].astype(BF16)


def _four_body(L, G, u_ref, wc_ref, wl_ref, o_ref, z_ref):
    scale = (L * FOURIER_HEAD_CH) ** -0.5
    if wl_ref.shape[0] == L:
        for s in range(G):
            rows = slice(s * L, (s + 1) * L)
            _channel_dft(u_ref[rows, :], wc_ref, z_ref, 0, L)
            o_ref[rows, :] = (jnp.dot(wl_ref[...], z_ref[...], preferred_element_type=F32) * scale).astype(BF16)
    else:
        @pl.when(pl.program_id(1) == 0)
        def _():
            rows = 512
            for c in range(L // rows):
                _channel_dft(u_ref[c * rows:(c + 1) * rows, :], wc_ref, z_ref, c * rows, L)

        o_ref[...] = (jnp.dot(wl_ref[...], z_ref[...], preferred_element_type=F32) * scale).astype(BF16)


def _four(pf, wc_bf, wl_bf, L, nseq, row0, G):
    tr = min(L, 512)
    return pl.pallas_call(
        functools.partial(_four_body, L, G),
        grid=(nseq // G, L // tr),
        in_specs=[pl.BlockSpec((G * L, 512), lambda s, r: (row0 + s, 0)),
                  pl.BlockSpec((FOURIER_HEAD_CH, 2 * FOURIER_HEAD_CH), lambda s, r: (0, 0)),
                  pl.BlockSpec((tr, 2 * L), lambda s, r: (r, 0))],
        out_specs=pl.BlockSpec((G * tr, 512), lambda s, r: (s * (L // tr) + r, 0)),
        out_shape=jax.ShapeDtypeStruct((nseq * L, 512), BF16),
        scratch_shapes=[pltpu.VMEM((2 * L, 512), BF16)],
        compiler_params=_cparams(("parallel", "arbitrary"), 48),
        name="four_L%d" % L,
    )(pf, wc_bf, wl_bf)


def _route(logits):
    lane = lax.broadcasted_iota(jnp.int32, logits.shape, 1).astype(F32)
    neg = jnp.float32(-jnp.inf)
    big = jnp.float32(1 << 20)
    is_g = (lane >= N_EXPERTS) & (lane < N_EXPERTS + N_EXPERT_GROUPS)
    gl = jnp.where(is_g, logits, neg)
    gmax = jnp.max(gl, axis=-1, keepdims=True)
    g_idx = jnp.min(jnp.where(gl == gmax, lane, big), axis=-1, keepdims=True) - N_EXPERTS
    denom = jnp.sum(jnp.where(is_g, jnp.exp(gl - gmax), 0.0), axis=-1, keepdims=True)
    g_w = 1.0 / denom
    in_grp = (lane >= g_idx * EXPERTS_PER_GROUP) & (lane < (g_idx + 1) * EXPERTS_PER_GROUP)
    el = jnp.where(in_grp, logits, neg)
    v1 = jnp.max(el, axis=-1, keepdims=True)
    i1 = jnp.min(jnp.where(el == v1, lane, big), axis=-1, keepdims=True)
    el2 = jnp.where(lane == i1, neg, el)
    v2 = jnp.max(el2, axis=-1, keepdims=True)
    i2 = jnp.min(jnp.where(el2 == v2, lane, big), axis=-1, keepdims=True)
    e2 = jnp.exp(v2 - v1)
    p1 = 1.0 / (1.0 + e2)
    p2 = e2 / (1.0 + e2)
    gates = g_w * jnp.where(lane == i1, p1, jnp.where(lane == i2, p2, 0.0))
    return gates + jnp.where(lane == g_idx + N_EXPERTS, 1.0, 0.0)


def _pair_specs(width):
    n = T_CTX // TM
    return [pl.BlockSpec((TM, width), lambda i: (jnp.minimum(i, n - 1), 0)),
            pl.BlockSpec((TM, width), lambda i: (jnp.maximum(i - n, 0), 0))]


def _outproj_body(nx, *refs):
    mixers = refs[nx:nx + 8]
    (w_ref, g1_ref, gain_ref, sc_ref, sh_ref, wr_ref, br_ref, xo_ref, hg_ref, route_ref, wb_ref) = refs[nx + 8:]

    @pl.when(pl.program_id(0) == 0)
    def _():
        wb_ref[...] = w_ref[...].astype(BF16)

    def residual(c):
        rows = slice(c * OUT_ROWS, (c + 1) * OUT_ROWS)
        mix = None
        for k in range(4):
            part = jnp.dot(_load_x(mixers[2 * k:2 * k + 2], rows), wb_ref[512 * k:512 * (k + 1), :],
                           preferred_element_type=F32)
            mix = part if mix is None else mix + part
        return _load_x(refs[:nx], rows) + g1_ref[...] * mix

    nchunk = TM // OUT_ROWS
    x_next = residual(0)
    for c in range(nchunk):
        rows = slice(c * OUT_ROWS, (c + 1) * OUT_ROWS)
        x = x_next
        x_next = residual(c + 1) if c + 1 < nchunk else None
        xo_ref[rows, :] = x
        ms = jnp.mean(x * x, axis=-1, keepdims=True)
        h = x * lax.rsqrt(ms + EPS) * gain_ref[...]
        h = h * (1.0 + sc_ref[...]) + sh_ref[...]
        h_hi = h.astype(BF16)
        h_lo = (h - h_hi.astype(F32)).astype(BF16)
        t = (jnp.dot(h_hi, wr_ref[...], preferred_element_type=F32)
             + jnp.dot(h_lo, wr_ref[...], preferred_element_type=F32))
        logits = t + pltpu.roll(t, LANES - ROUTER_LO_LANE, axis=1) + br_ref[...]
        route = _route(logits)
        route_ref[rows, :] = route
        rec0 = c * OUT_ROWS * REC
        for s in range(H_ROWS):
            hg_ref[pl.ds(rec0 + s, OUT_ROWS, stride=REC), :] = h[:, s * LANES:(s + 1) * LANES]


def _outproj(mixers, xs, w_out, mod4, gain, wr, br, layer):
    tile = lambda w: pl.BlockSpec((TM, w), lambda i: (i, 0))
    const = lambda r, c: pl.BlockSpec((r, c), lambda i: (0, 0))
    mix_specs = []
    for _ in range(4):
        mix_specs += _pair_specs(512)
    return pl.pallas_call(
        functools.partial(_outproj_body, len(xs)),
        grid=(T_ALL // TM,),
        in_specs=_x_specs(len(xs)) + mix_specs + [_resident((2048, D), layer),
                                                  _mod_spec(2, TM), const(1, D), _mod_spec(4, TM), _mod_spec(3, TM),
                                                  const(D, LANES), const(1, LANES)],
        out_specs=[tile(D), pl.BlockSpec((TM * REC, LANES), lambda i: (i, 0)), tile(LANES)],
        out_shape=[jax.ShapeDtypeStruct((T_ALL, D), F32),
                   jax.ShapeDtypeStruct((T_ALL * REC, LANES), F32),
                   jax.ShapeDtypeStruct((T_ALL, LANES), F32)],
        scratch_shapes=[pltpu.VMEM((2048, D), BF16)],
        compiler_params=_cparams(("arbitrary",), 56),
        name="out_proj",
    )(*xs, *[a for pair in mixers for a in pair], w_out, mod4, gain, mod4, mod4, wr, br)


def _rank_body(r_ref, o_ref, cnt_ref, carry_ref):
    @pl.when(pl.program_id(0) == 0)
    def _():
        carry_ref[...] = jnp.zeros_like(carry_ref)

    r = r_ref[...]
    lane = lax.broadcasted_iota(jnp.int32, r.shape, 1)
    is_g = (lane >= N_EXPERTS) & (lane < N_EXPERTS + N_EXPERT_GROUPS)
    onehot = jnp.where(is_g, r, 0.0)
    row = lax.broadcasted_iota(jnp.int32, (RT, RT), 0)
    col = lax.broadcasted_iota(jnp.int32, (RT, RT), 1)
    earlier = jnp.where(col < row, 1.0, 0.0).astype(BF16)
    before = jnp.dot(earlier, onehot.astype(BF16), preferred_element_type=F32) + carry_ref[...]
    rank = jnp.sum(onehot * before, axis=-1, keepdims=True)
    grp = jnp.sum(onehot * (lane - N_EXPERTS).astype(F32), axis=-1, keepdims=True)
    packed = jnp.where(lane == 0, rank, jnp.where(lane == 1, grp, 0.0))
    o_ref[...] = packed.T[0:SUBLANES, :].astype(jnp.int32)
    carry_ref[...] += jnp.sum(onehot, axis=0, keepdims=True)
    cnt_ref[...] = carry_ref[...]


def _rank(route):
    return pl.pallas_call(
        _rank_body,
        grid=(T_ALL // RT,),
        in_specs=[pl.BlockSpec((RT, LANES), lambda i: (i, 0))],
        out_specs=[pl.BlockSpec((SUBLANES, RT), lambda i: (i, 0)),
                   pl.BlockSpec((1, LANES), lambda i: (0, 0))],
        out_shape=[jax.ShapeDtypeStruct((T_ALL // RT * SUBLANES, RT), jnp.int32),
                   jax.ShapeDtypeStruct((1, LANES), F32)],
        scratch_shapes=[pltpu.VMEM((1, LANES), F32)],
        compiler_params=_cparams(("arbitrary",), 32),
        name="moe_rank",
    )(route)


def _wait_rows(copy, n):
    def body(t, c):
        copy.wait()
        return c
    lax.fori_loop(0, n, body, 0, unroll=8)


def _record(ref, idx, rows):
    start = idx * rows if isinstance(idx, int) else pl.multiple_of(idx * rows, rows)
    return ref.at[pl.ds(start, rows)]


def _dispatch_body(slot_sm, fstart_sm, flen_sm, hg_ref, route_ref, xs_hbm, gs_ref, zrec_ref, sem):
    i = pl.program_id(0)
    base = i * DT

    @pl.when(i == 0)
    def _():
        gs_ref[...] = jnp.zeros_like(gs_ref)
        zrec_ref[...] = jnp.zeros_like(zrec_ref)
        for g in range(N_EXPERT_GROUPS):
            def fill(r, c):
                pltpu.make_async_copy(zrec_ref, _record(xs_hbm, fstart_sm[g] + r, REC), sem.at[1]).start()
                return c
            lax.fori_loop(0, flen_sm[g], fill, 0)
        _wait_rows(pltpu.make_async_copy(zrec_ref, _record(xs_hbm, 0, REC), sem.at[1]), NSLOT - T_ALL)

    def issue(t, c):
        slot = slot_sm[base + t]
        pltpu.make_async_copy(_record(hg_ref, t, REC), _record(xs_hbm, slot, REC), sem.at[0]).start()
        gs_ref[pl.ds(slot, 1), :] = route_ref[pl.ds(t, 1), :]
        return c
    lax.fori_loop(0, DT, issue, 0, unroll=8)
    _wait_rows(pltpu.make_async_copy(_record(hg_ref, 0, REC), _record(xs_hbm, 0, REC), sem.at[0]), DT)


def _dispatch(slot, fstart, flen, hg, route):
    return pl.pallas_call(
        _dispatch_body,
        grid_spec=pltpu.PrefetchScalarGridSpec(
            num_scalar_prefetch=3,
            grid=(T_ALL // DT,),
            in_specs=[pl.BlockSpec((DT * REC, LANES), lambda i, s, f0, f1: (i, 0)),
                      pl.BlockSpec((DT, LANES), lambda i, s, f0, f1: (i, 0))],
            out_specs=[pl.BlockSpec(memory_space=pl.ANY),
                       pl.BlockSpec((NSLOT, LANES), lambda i, s, f0, f1: (0, 0), pipeline_mode=pl.Buffered(1))],
            scratch_shapes=[pltpu.VMEM((REC, LANES), F32), pltpu.SemaphoreType.DMA((2,))]),
        out_shape=[jax.ShapeDtypeStruct((NSLOT * REC, LANES), F32),
                   jax.ShapeDtypeStruct((NSLOT, LANES), F32)],
        compiler_params=_cparams(("arbitrary",), 48),
        name="moe_dispatch",
    )(slot, fstart, flen, hg, route)


def _ffn_body(tgrp_sm, xs_ref, gs_ref, wg32_ref, wu32_ref, wd32_ref, ys_ref, h_ref, wg_ref, wu_ref, wd_ref):
    j = pl.program_id(0)
    grp = tgrp_sm[j]

    @pl.when(jnp.logical_or(j == 0, grp != tgrp_sm[jnp.maximum(j - 1, 0)]))
    def _():
        for e in range(EXPERTS_PER_GROUP):
            cols = slice(e * EXPERT_FF, (e + 1) * EXPERT_FF)
            wg_ref[:, cols] = wg32_ref[e].astype(BF16)
            wu_ref[:, cols] = wu32_ref[e].astype(BF16)
            wd_ref[cols, :] = wd32_ref[e].astype(BF16)

    for s in range(H_ROWS):
        h_ref[:, s * LANES:(s + 1) * LANES] = xs_ref[pl.ds(s, TS, stride=REC), :].astype(BF16)
    gates = gs_ref[...]
    h = h_ref[...]
    a = jnp.dot(h, wg_ref[...], preferred_element_type=F32)
    b = jnp.dot(h, wu_ref[...], preferred_element_type=F32)
    hid = _silu(a) * b
    lane = lax.broadcasted_iota(jnp.int32, gates.shape, 1)
    parts = []
    for e in range(EXPERTS_PER_GROUP):
        ge = jnp.sum(jnp.where(lane == grp * EXPERTS_PER_GROUP + e, gates, 0.0), axis=-1, keepdims=True)
        parts.append((hid[:, e * EXPERT_FF:(e + 1) * EXPERT_FF] * ge).astype(BF16))
    y = jnp.dot(jnp.concatenate(parts, axis=1), wd_ref[...], preferred_element_type=F32)
    for s in range(H_ROWS):
        ys_ref[pl.ds(s, TS, stride=H_ROWS), :] = y[:, s * LANES:(s + 1) * LANES]


def _ffn(tile_grp, xs, gs, w_gate, w_up, w_down, layer):
    ffw = EXPERTS_PER_GROUP * EXPERT_FF
    experts = lambda r, c: pl.BlockSpec((None, EXPERTS_PER_GROUP, r, c), lambda j, tg: (layer, tg[j], 0, 0))
    return pl.pallas_call(
        _ffn_body,
        grid_spec=pltpu.PrefetchScalarGridSpec(
            num_scalar_prefetch=1,
            grid=(NT,),
            in_specs=[pl.BlockSpec((TS * REC, LANES), lambda j, tg: (j, 0)),
                      pl.BlockSpec((TS, LANES), lambda j, tg: (j, 0)),
                      experts(D, EXPERT_FF), experts(D, EXPERT_FF), experts(EXPERT_FF, D)],
            out_specs=pl.BlockSpec((TS * H_ROWS, LANES), lambda j, tg: (j, 0)),
            scratch_shapes=[pltpu.VMEM((TS, D), BF16), pltpu.VMEM((D, ffw), BF16),
                            pltpu.VMEM((D, ffw), BF16), pltpu.VMEM((ffw, D), BF16)]),
        out_shape=jax.ShapeDtypeStruct((NSLOT * H_ROWS, LANES), F32),
        compiler_params=_cparams(("arbitrary",), 56),
        name="moe_ffn",
    )(tile_grp, xs, gs, w_gate, w_up, w_down)


def _combine_body(final, slot_sm, ys_hbm, x_ref, g2_ref, fg_ref, *rest):
    outs, buf, sem = rest[:-2], rest[-2], rest[-1]
    i = pl.program_id(0)
    n = pl.num_programs(0)

    def issue(step, b):
        def body(t, c):
            pltpu.make_async_copy(_record(ys_hbm, slot_sm[step * TM + t], H_ROWS),
                                  _record(buf.at[b], t, H_ROWS), sem.at[b]).start()
            return c
        lax.fori_loop(0, TM, body, 0, unroll=8)

    @pl.when(i == 0)
    def _():
        issue(0, 0)

    @pl.when(i + 1 < n)
    def _():
        issue(i + 1, (i + 1) % 2)

    b = i % 2
    _wait_rows(pltpu.make_async_copy(_record(ys_hbm, 0, H_ROWS), _record(buf.at[b], 0, H_ROWS), sem.at[b]), TM)
    y = jnp.concatenate([buf[b, pl.ds(s, TM, stride=H_ROWS), :] for s in range(H_ROWS)], axis=1)
    x = x_ref[...] + g2_ref[...] * y
    if not final:
        outs[0][...] = x
    else:
        ms = jnp.mean(x * x, axis=-1, keepdims=True)
        yn = x * lax.rsqrt(ms + EPS) * fg_ref[...]
        is_ctx = i < T_CTX // TM

        @pl.when(is_ctx)
        def _():
            outs[0][...] = yn

        @pl.when(jnp.logical_not(is_ctx))
        def _():
            outs[1][...] = yn


def _combine(slot, ys, x, mod4, final_gain, final):
    if final:
        nc = T_CTX // TM
        out_specs = [pl.BlockSpec((TM, D), lambda i, s: (jnp.minimum(i, nc - 1), 0)),
                     pl.BlockSpec((TM, D), lambda i, s: (jnp.maximum(i - nc, 0), 0))]
        out_shape = [jax.ShapeDtypeStruct((T_CTX, D), F32), jax.ShapeDtypeStruct((T_LAT, D), F32)]
    else:
        out_specs = pl.BlockSpec((TM, D), lambda i, s: (i, 0))
        out_shape = jax.ShapeDtypeStruct((T_ALL, D), F32)
    return pl.pallas_call(
        functools.partial(_combine_body, final),
        grid_spec=pltpu.PrefetchScalarGridSpec(
            num_scalar_prefetch=1,
            grid=(T_ALL // TM,),
            in_specs=[pl.BlockSpec(memory_space=pl.ANY),
                      pl.BlockSpec((TM, D), lambda i, s: (i, 0)),
                      pl.BlockSpec((None, None, 1, D), lambda i, s: (_mod_row(i, TM), 5, 0, 0)),
                      pl.BlockSpec((1, D), lambda i, s: (0, 0))],
            out_specs=out_specs,
            scratch_shapes=[pltpu.VMEM((2, TM * H_ROWS, LANES), F32), pltpu.SemaphoreType.DMA((2,))]),
        out_shape=out_shape,
        compiler_params=_cparams(("arbitrary",), 48),
        name="moe_combine",
    )(slot, ys, x, mod4, final_gain)


def _moe_plan(rk, cnt):
    rk = rk.reshape(T_ALL // RT, SUBLANES, RT)
    rank = rk[:, 0, :].reshape(T_ALL)
    grp = rk[:, 1, :].reshape(T_ALL)
    cnt = cnt[0, N_EXPERTS:N_EXPERTS + N_EXPERT_GROUPS].astype(jnp.int32)
    padded = (cnt + TS - 1) // TS * TS
    off = jnp.cumsum(padded) - padded
    padded = padded.at[N_EXPERT_GROUPS - 1].set(NSLOT - off[N_EXPERT_GROUPS - 1])
    slot = rank
    for g in range(1, N_EXPERT_GROUPS):
        slot = slot + jnp.where(grp == g, off[g], 0)
    ends = (off + padded)[:N_EXPERT_GROUPS - 1]
    tile_grp = jnp.sum((jnp.arange(NT, dtype=jnp.int32)[:, None] * TS >= ends[None, :]).astype(jnp.int32), axis=1)
    return slot, tile_grp, off + cnt, padded - cnt


def _rope_tables():
    t = np.arange(DEC_SEQ)
    pos = np.stack([t // GRID_W, t % GRID_W], axis=1).astype(np.float64)
    inv = ROPE_THETA ** (-np.arange(16, dtype=np.float64) / 16.0)
    d = np.arange(LANES) % HEAD_DIM
    pair = d // 2
    ang = pos[:, (pair >= 16).astype(np.int64)] * inv[pair % 16][None, :]
    sign = np.where(d % 2 == 0, -1.0, 1.0)
    return np.cos(ang).astype(np.float32), (np.sin(ang) * sign[None, :]).astype(np.float32)


def _dft_pos(L):
    j = np.arange(L)
    ang = 2.0 * np.pi * ((j[:, None] * j[None, :]) % L) / L
    return np.concatenate([np.cos(ang), -np.sin(ang)], axis=1).astype(np.float32)


def _dft_ch():
    c = np.arange(FOURIER_HEAD_CH)
    ang = 2.0 * np.pi * ((c[:, None] * c[None, :]) % FOURIER_HEAD_CH) / FOURIER_HEAD_CH
    return np.concatenate([np.cos(ang), np.sin(ang)], axis=1).astype(np.float32)


def kernel(x_prompt, x_sample, cache_k, cache_v, c, c_ctx, w_ada, b_ada, norm_mix, norm_ffn, w_in, w_out, q_norm, k_norm, conv_w, conv_b, conv_ln_g, conv_ln_b, pool_w, pool_scale, router_group_w, router_group_b, router_expert_w, router_expert_b, moe_w_gate, moe_w_up, moe_w_down, final_norm):
    xs = (x_prompt.reshape(T_CTX, D), x_sample.reshape(T_LAT, D))
    cond8 =jnp.concatenate([c_ctx[None, :], c, jnp.zeros((3, D), F32)], axis=0)
    mod = _ada(cond8, w_ada, b_ada)

    pw_bf = pool_w.astype(BF16)
    n_route = N_EXPERTS + N_EXPERT_GROUPS
    wr = jnp.concatenate([router_expert_w, router_group_w], axis=-1)
    wr_hi = wr.astype(BF16)
    wr_lo = (wr - wr_hi.astype(F32)).astype(BF16)
    w_router = jnp.concatenate([wr_hi, jnp.zeros((DEPTH, D, ROUTER_LO_LANE - n_route), BF16), wr_lo,
                                jnp.zeros((DEPTH, D, LANES - ROUTER_LO_LANE - n_route), BF16)], axis=-1)
    b_router = jnp.concatenate([router_expert_b, router_group_b,
                                jnp.zeros((DEPTH, LANES - N_EXPERTS - N_EXPERT_GROUPS), F32)], axis=-1)
    conv_w32 = jnp.concatenate([conv_w, jnp.zeros((DEPTH, 1, CONV_CH), F32)], axis=1)

    cos_np, sin_np = _rope_tables()
    cos, sin = jnp.asarray(cos_np), jnp.asarray(sin_np)
    wc_bf = jnp.asarray(_dft_ch()).astype(BF16)
    wl_ctx = jnp.asarray(_dft_pos(SEQ)).astype(BF16)
    wl_lat = jnp.asarray(_dft_pos(DEC_SEQ)).astype(BF16)

    cache_k4 = cache_k.reshape(DEC_BATCH, DEPTH, PAST_LEN, KV_CH)
    cache_v4 = cache_v.reshape(DEC_BATCH, DEPTH, PAST_LEN, KV_CH)
    new_k = jnp.zeros((BATCH, DEPTH, SEQ, KV_CH), F32)
    new_v = jnp.zeros((BATCH, DEPTH, SEQ, KV_CH), F32)
    fgain = final_norm.reshape(1, D)

    for l in range(DEPTH):
        mod4 = mod[l].reshape(8, N_MOD, 1, D)
        proj, pf = _inproj(xs, norm_mix[l].reshape(1, D), mod4, w_in, l)
        qg = jnp.tile(q_norm[l], 2).reshape(1, LANES)
        kg = jnp.tile(k_norm[l], 2).reshape(1, LANES)
        attn_c, new_k, new_v = _ctx_attn(proj, qg, kg, new_k, new_v, l)
        attn_l = _lat_attn(proj, cache_k4, cache_v4, qg, kg, cos, sin, l)
        cb, lg, lb = conv_b[l].reshape(1, -1), conv_ln_g[l].reshape(1, -1), conv_ln_b[l].reshape(1, -1)
        ps = pool_scale[l].reshape(1, -1)
        lat0 = T_CTX // DEC_SEQ
        mixers = [(attn_c, attn_l),
                  (_conv(proj, conv_w32[l], cb, lg, lb, SEQ, BATCH, 0),
                   _conv(proj, conv_w32[l], cb, lg, lb, DEC_SEQ, DEC_BATCH, lat0)),
                  (_pool(proj, pw_bf[l], ps, SEQ, BATCH, 0, MIX_G),
                   _pool(proj, pw_bf[l], ps, DEC_SEQ, DEC_BATCH, lat0, 1)),
                  (_four(pf, wc_bf, wl_ctx, SEQ, BATCH, 0, MIX_G),
                   _four(pf, wc_bf, wl_lat, DEC_SEQ, DEC_BATCH, lat0, 1))]
        x, hg, route = _outproj(mixers, xs, w_out, mod4, norm_ffn[l].reshape(1, D),
                                w_router[l], b_router[l].reshape(1, LANES), l)
        slot, tile_grp, fstart, flen = _moe_plan(*_rank(route))
        xslots, gslots = _dispatch(slot, fstart, flen, hg, route)
        yslots = _ffn(tile_grp, xslots, gslots, moe_w_gate, moe_w_up, moe_w_down, l)
        out = _combine(slot, yslots, x, mod4, fgain, l == DEPTH - 1)
        xs = (out,)

    y_prompt = out[0].reshape(BATCH, SEQ, D)
    y_sample = out[1].reshape(DEC_BATCH, DEC_SEQ, D)
    new_k = new_k.reshape(BATCH, DEPTH, SEQ, 2, HEAD_DIM)
    new_v = new_v.reshape(BATCH, DEPTH, SEQ, 2, HEAD_DIM)
    return (y_prompt, y_sample, new_k, new_v)
```

```python
import functools

import numpy as np
import jax
import jax.numpy as jnp
from jax import lax
from jax.experimental import pallas as pl
from jax.experimental.pallas import tpu as pltpu

F32 = jnp.float32
BF16 = jnp.bfloat16

D = 1024
BATCH = 32
SEQ = 256
DEPTH = 2
DEC_BATCH = 4
DEC_SEQ = 2048
PAST_LEN = 256
GRID_W = 64
HEAD_DIM = 64
ATTN_CH = 512
KV_CH = 128
CONV_CH = 512
CONV_WIDTH = 31
POOL_WINDOWS = (2, 4, 8, 16)
POOL_GROUP_CH = 128
FOURIER_HEAD_CH = 128
FOURIER_HEADS = 4
N_EXPERT_GROUPS = 4
EXPERTS_PER_GROUP = 4
N_EXPERTS = 16
EXPERT_FF = 256
ROPE_THETA = 10000.0
EPS = 1e-6
N_MOD = 6

T_CTX = BATCH * SEQ
T_LAT = DEC_BATCH * DEC_SEQ
T_ALL = T_CTX + T_LAT
LANES = 128
SUBLANES = 8
PAD = 16
CONV_ROWS = 128
ROUTER_LO_LANE = 32
IN_COLS = 2816
COL_Q, COL_CA, COL_CG, COL_POOL, COL_K, COL_V, COL_FOUR = 0, 512, 1024, 1536, 2048, 2176, 2304
F32_COLS = COL_FOUR

TM = 512
OUT_ROWS = 256
DT = 2048
CTX_G = 2
MIX_G = 4
TS = 512
NT = T_ALL // TS + N_EXPERT_GROUPS
NSLOT = NT * TS
H_ROWS = D // LANES
REC = H_ROWS
TQ = 256
LK = PAST_LEN + DEC_SEQ


def _cparams(sem, vmem_mb):
    return pltpu.CompilerParams(dimension_semantics=sem, vmem_limit_bytes=vmem_mb * 1024 * 1024)


def _mod_row(i, tm):
    nctx = T_CTX // tm
    per = DEC_SEQ // tm
    return jnp.where(i < nctx, 0, 1 + (i - nctx) // per)


def _mod_spec(k, tm):
    return pl.BlockSpec((None, None, 1, D), lambda i: (_mod_row(i, tm), k, 0, 0))


def _silu(x):
    return x * jax.nn.sigmoid(x)


def _ada_body(c_ref, w_ref, b_ref, o_ref):
    s = _silu(c_ref[...])
    o_ref[...] = jnp.dot(s.astype(BF16), w_ref[...].astype(BF16), preferred_element_type=F32) + b_ref[...]


def _ada(cond8, w_ada, b_ada):
    tn = 1536
    n = N_MOD * D
    return pl.pallas_call(
        _ada_body,
        grid=(DEPTH, n // tn),
        in_specs=[pl.BlockSpec((8, D), lambda l, j: (0, 0)),
                  pl.BlockSpec((None, D, tn), lambda l, j: (l, 0, j)),
                  pl.BlockSpec((None, 1, tn), lambda l, j: (l, 0, j))],
        out_specs=pl.BlockSpec((None, 8, tn), lambda l, j: (l, 0, j)),
        out_shape=jax.ShapeDtypeStruct((DEPTH, 8, n), F32),
        compiler_params=_cparams(("arbitrary", "arbitrary"), 40),
        name="ada_mod",
    )(cond8, w_ada, b_ada.reshape(DEPTH, 1, n))


def _x_specs(nx):
    if nx == 1:
        return [pl.BlockSpec((TM, D), lambda i: (i, 0))]
    n = T_CTX // TM
    return [pl.BlockSpec((TM, D), lambda i: (jnp.minimum(i, n - 1), 0)),
            pl.BlockSpec((TM, D), lambda i: (jnp.maximum(i - n, 0), 0))]


def _load_x(x_refs, rows=slice(None)):
    if len(x_refs) == 1:
        return x_refs[0][rows, :]
    return jnp.where(pl.program_id(0) < T_CTX // TM, x_refs[0][rows, :], x_refs[1][rows, :])


def _resident(shape, layer):
    return pl.BlockSpec((None,) + shape, lambda *_: (layer,) + (0,) * len(shape), pipeline_mode=pl.Buffered(1))


def _inproj_body(nx, *refs):
    g_ref, sc_ref, sh_ref, w_ref, o_ref, of_ref, wb_ref = refs[nx:]

    @pl.when(pl.program_id(0) == 0)
    def _():
        kv0, kv1 = ATTN_CH, ATTN_CH + 2 * KV_CH
        wb_ref[:, 0:kv0] = w_ref[:, 0:kv0].astype(BF16)
        wb_ref[:, kv0:COL_K] = w_ref[:, kv1:kv1 + COL_K - kv0].astype(BF16)
        wb_ref[:, COL_K:COL_FOUR] = w_ref[:, kv0:kv1].astype(BF16)
        wb_ref[:, COL_FOUR:IN_COLS] = w_ref[:, COL_FOUR:IN_COLS].astype(BF16)

    x = _load_x(refs[:nx])
    ms = jnp.mean(x * x, axis=-1, keepdims=True)
    h = x * lax.rsqrt(ms + EPS) * g_ref[...]
    h = h * (1.0 + sc_ref[...]) + sh_ref[...]
    hb = h.astype(BF16)
    o_ref[...] = jnp.dot(hb, wb_ref[:, 0:F32_COLS], preferred_element_type=F32)
    of_ref[...] = jnp.dot(hb, wb_ref[:, F32_COLS:IN_COLS], preferred_element_type=F32).astype(BF16)


def _inproj(xs, gain, mod4, w_in, layer):
    return pl.pallas_call(
        functools.partial(_inproj_body, len(xs)),
        grid=(T_ALL // TM,),
        in_specs=_x_specs(len(xs)) + [pl.BlockSpec((1, D), lambda i: (0, 0)),
                                      _mod_spec(1, TM), _mod_spec(0, TM),
                                      _resident((D, IN_COLS), layer)],
        out_specs=[pl.BlockSpec((TM, F32_COLS), lambda i: (i, 0)),
                   pl.BlockSpec((TM, IN_COLS - F32_COLS), lambda i: (i, 0))],
        out_shape=[jax.ShapeDtypeStruct((T_ALL, F32_COLS), F32),
                   jax.ShapeDtypeStruct((T_ALL, IN_COLS - F32_COLS), BF16)],
        scratch_shapes=[pltpu.VMEM((D, IN_COLS), BF16)],
        compiler_params=_cparams(("arbitrary",), 56),
        name="in_proj",
    )(*xs, gain, mod4, mod4, w_in)


def _lane_lo():
    return lax.broadcasted_iota(jnp.int32, (1, LANES), 1) < HEAD_DIM


def _head_norm(x, gain):
    lo = _lane_lo()
    x2 = x * x
    s_lo = jnp.sum(jnp.where(lo, x2, 0.0), axis=-1, keepdims=True)
    s_hi = jnp.sum(jnp.where(lo, 0.0, x2), axis=-1, keepdims=True)
    r = jnp.where(lo, lax.rsqrt(s_lo * (1.0 / HEAD_DIM) + EPS), lax.rsqrt(s_hi * (1.0 / HEAD_DIM) + EPS))
    return x * r * gain


def _rope(x, cos, sin_signed):
    even = (lax.broadcasted_iota(jnp.int32, (1, LANES), 1) % 2) == 0
    swapped = jnp.where(even, pltpu.roll(x, LANES - 1, axis=1), pltpu.roll(x, 1, axis=1))
    return x * cos + swapped * sin_signed


def _split_heads(x, g):
    lo = _lane_lo()
    own = jnp.where(lo if g == 0 else jnp.logical_not(lo), x, 0.0)
    other = pltpu.roll(own, HEAD_DIM, axis=1)
    return (own, other) if g == 0 else (other, own)


Q_SCALE = HEAD_DIM ** -0.5 * 1.4426950408889634


def _ones_cols(rows, lk):
    r = lax.broadcasted_iota(jnp.int32, (rows, LANES), 0)
    lane = lax.broadcasted_iota(jnp.int32, (rows, LANES), 1)
    return jnp.where(((lane == 0) & (r < lk)) | ((lane == 1) & (r >= lk)), 1.0, 0.0)


def _scores(q_bf, k_cat):
    return lax.dot_general(q_bf, k_cat, (((1,), (1,)), ((), ())), preferred_element_type=F32)


def _attend_tile(q_bf, k_cat, v_cat, lk):
    return _softmax_pv(_scores(q_bf, k_cat), v_cat, lk)


def _softmax_pv(s, v_cat, lk):
    s_lo, s_hi = s[:, :lk], s[:, lk:]
    p_lo = jnp.exp2(s_lo - jnp.max(s_lo, axis=-1, keepdims=True))
    p_hi = jnp.exp2(s_hi - jnp.max(s_hi, axis=-1, keepdims=True))
    p = jnp.concatenate([p_lo, p_hi], axis=1).astype(BF16)
    o = jnp.dot(p, v_cat, preferred_element_type=F32)
    l_lo, l_hi = o[:, LANES:LANES + 1], o[:, LANES + 1:LANES + 2]
    return o[:, :LANES] * jnp.where(_lane_lo(), 1.0 / l_lo, 1.0 / l_hi)


def _ctx_attn_body(q_ref, k_ref, v_ref, qg_ref, kg_ref, kc_in, vc_in, o_ref, kc_ref, vc_ref):
    del kc_in, vc_in
    ones = _ones_cols(2 * SEQ, SEQ)
    kv = []
    for s in range(CTX_G):
        rows = slice(s * SEQ, (s + 1) * SEQ)
        kn = _head_norm(k_ref[rows, :], kg_ref[...])
        v = v_ref[rows, :]
        kc_ref[s] = kn
        vc_ref[s] = v
        for g in range(2):
            k_lo, k_hi = _split_heads(kn, g)
            v_lo, v_hi = _split_heads(v, g)
            kv.append((jnp.concatenate([k_lo, k_hi], axis=0).astype(BF16),
                       jnp.concatenate([jnp.concatenate([v_lo, v_hi], axis=0), ones], axis=1).astype(BF16)))
    for t in range(ATTN_CH // LANES):
        for s in range(CTX_G):
            rows = slice(s * SEQ, (s + 1) * SEQ)
            k_cat, v_cat = kv[2 * s + t // 2]
            q = _head_norm(q_ref[rows, t * LANES:(t + 1) * LANES], qg_ref[...]) * Q_SCALE
            o = _attend_tile(q.astype(BF16), k_cat, v_cat, SEQ)
            o_ref[rows, t * LANES:(t + 1) * LANES] = o.astype(BF16)


def _ctx_attn(proj, qg, kg, kcache, vcache, layer):
    G = CTX_G
    return pl.pallas_call(
        _ctx_attn_body,
        grid=(BATCH // G,),
        in_specs=[pl.BlockSpec((G * SEQ, ATTN_CH), lambda b: (b, COL_Q // ATTN_CH)),
                  pl.BlockSpec((G * SEQ, KV_CH), lambda b: (b, COL_K // KV_CH)),
                  pl.BlockSpec((G * SEQ, KV_CH), lambda b: (b, COL_V // KV_CH)),
                  pl.BlockSpec((1, LANES), lambda b: (0, 0)),
                  pl.BlockSpec((1, LANES), lambda b: (0, 0)),
                  pl.BlockSpec(memory_space=pl.ANY),
                  pl.BlockSpec(memory_space=pl.ANY)],
        out_specs=[pl.BlockSpec((G * SEQ, ATTN_CH), lambda b: (b, 0)),
                   pl.BlockSpec((G, None, SEQ, KV_CH), lambda b: (b, layer, 0, 0)),
                   pl.BlockSpec((G, None, SEQ, KV_CH), lambda b: (b, layer, 0, 0))],
        out_shape=[jax.ShapeDtypeStruct((T_CTX, ATTN_CH), BF16),
                   jax.ShapeDtypeStruct((BATCH, DEPTH, SEQ, KV_CH), F32),
                   jax.ShapeDtypeStruct((BATCH, DEPTH, SEQ, KV_CH), F32)],
        input_output_aliases={5: 1, 6: 2},
        compiler_params=_cparams(("parallel",), 32),
        name="ctx_attn",
    )(proj, proj, proj, qg, kg, kcache, vcache)


def _lat_attn_body(q_ref, k_ref, v_ref, ck_ref, cv_ref, qg_ref, kg_ref, cosk_ref, sink_ref, cosq_ref, sinq_ref,
                   o_ref, kcat_ref, vcat_ref):
    @pl.when(pl.program_id(1) == 0)
    def _():
        rows = 256
        for c in range(LK // rows):
            if c == 0:
                kn = ck_ref[...]
                v = cv_ref[...]
            else:
                r0 = (c - 1) * rows
                kn = _head_norm(k_ref[r0:r0 + rows, :], kg_ref[...])
                kn = _rope(kn, cosk_ref[r0:r0 + rows, :], sink_ref[r0:r0 + rows, :])
                v = v_ref[r0:r0 + rows, :]
            for g in range(2):
                k_lo, k_hi = _split_heads(kn, g)
                v_lo, v_hi = _split_heads(v, g)
                kcat_ref[g, c * rows:(c + 1) * rows, :] = k_lo.astype(BF16)
                kcat_ref[g, LK + c * rows:LK + (c + 1) * rows, :] = k_hi.astype(BF16)
                vcat_ref[g, c * rows:(c + 1) * rows, 0:LANES] = v_lo.astype(BF16)
                vcat_ref[g, LK + c * rows:LK + (c + 1) * rows, 0:LANES] = v_hi.astype(BF16)
                vcat_ref[g, c * rows:(c + 1) * rows, LANES:2 * LANES] = _ones_cols(rows, rows).astype(BF16)
                vcat_ref[g, LK + c * rows:LK + (c + 1) * rows, LANES:2 * LANES] = _ones_cols(rows, 0).astype(BF16)

    def scores(t):
        q = _head_norm(q_ref[:, t * LANES:(t + 1) * LANES], qg_ref[...])
        q = _rope(q, cosq_ref[...], sinq_ref[...]) * Q_SCALE
        return _scores(q.astype(BF16), kcat_ref[t // 2])

    nt = ATTN_CH // LANES
    s = scores(0)
    for t in range(nt):
        s_next = scores(t + 1) if t + 1 < nt else None
        o = _softmax_pv(s, vcat_ref[t // 2], LK)
        o_ref[:, t * LANES:(t + 1) * LANES] = o.astype(BF16)
        s = s_next


def _lat_attn(proj, cache_k4, cache_v4, qg, kg, cos, sin, layer):
    nqb = DEC_SEQ // TQ
    row0 = T_CTX // DEC_SEQ
    return pl.pallas_call(
        _lat_attn_body,
        grid=(DEC_BATCH, nqb),
        in_specs=[pl.BlockSpec((TQ, ATTN_CH), lambda b, i: (T_CTX // TQ + b * nqb + i, COL_Q // ATTN_CH)),
                  pl.BlockSpec((DEC_SEQ, KV_CH), lambda b, i: (row0 + b, COL_K // KV_CH)),
                  pl.BlockSpec((DEC_SEQ, KV_CH), lambda b, i: (row0 + b, COL_V // KV_CH)),
                  pl.BlockSpec((None, None, PAST_LEN, KV_CH), lambda b, i: (b, layer, 0, 0)),
                  pl.BlockSpec((None, None, PAST_LEN, KV_CH), lambda b, i: (b, layer, 0, 0)),
                  pl.BlockSpec((1, LANES), lambda b, i: (0, 0)),
                  pl.BlockSpec((1, LANES), lambda b, i: (0, 0)),
                  pl.BlockSpec((DEC_SEQ, LANES), lambda b, i: (0, 0)),
                  pl.BlockSpec((DEC_SEQ, LANES), lambda b, i: (0, 0)),
                  pl.BlockSpec((TQ, LANES), lambda b, i: (i, 0)),
                  pl.BlockSpec((TQ, LANES), lambda b, i: (i, 0))],
        out_specs=pl.BlockSpec((TQ, ATTN_CH), lambda b, i: (b * nqb + i, 0)),
        out_shape=jax.ShapeDtypeStruct((T_LAT, ATTN_CH), BF16),
        scratch_shapes=[pltpu.VMEM((2, 2 * LK, LANES), BF16), pltpu.VMEM((2, 2 * LK, 2 * LANES), BF16)],
        compiler_params=_cparams(("parallel", "arbitrary"), 56),
        name="lat_attn",
    )(proj, proj, proj, cache_k4, cache_v4, qg, kg, cos, sin, cos, sin)


def _conv_body(L, a_ref, g_ref, w_ref, b_ref, lg_ref, lb_ref, o_ref, pad_ref, y_ref):
    rows = CONV_ROWS
    zeros = jnp.zeros((PAD, CONV_CH), F32)
    pad_ref[0:PAD, :] = zeros
    pad_ref[PAD + L:2 * PAD + L, :] = zeros
    pad_ref[PAD:PAD + L, :] = a_ref[...] * jax.nn.sigmoid(g_ref[...])
    shift0 = PAD - CONV_WIDTH // 2

    def chunk(c, carry):
        base = pl.multiple_of(c * rows, rows)
        for lg in range(CONV_CH // LANES):
            lanes = slice(lg * LANES, (lg + 1) * LANES)
            y = jnp.zeros((rows, LANES), F32)
            for r in range(SUBLANES):
                z = None
                for q in range((CONV_WIDTH + shift0) // SUBLANES + 1):
                    j = SUBLANES * q + r - shift0
                    if 0 <= j < CONV_WIDTH:
                        term = w_ref[j:j + 1, lanes] * pad_ref[pl.ds(base + SUBLANES * q, rows + SUBLANES), lanes]
                        z = term if z is None else z + term
                y = y + z[r:r + rows, :]
            y_ref[:, lanes] = y + b_ref[:, lanes]
        half = rows // 2
        for p in range(2):
            y = y_ref[p * half:(p + 1) * half, :]
            mu = jnp.mean(y, axis=-1, keepdims=True)
            yc = y - mu
            var = jnp.mean(yc * yc, axis=-1, keepdims=True)
            yn = yc * lax.rsqrt(var + EPS) * lg_ref[...] + lb_ref[...]
            o_ref[pl.ds(base + p * half, half), :] = _silu(yn).astype(BF16)
        return carry

    lax.fori_loop(0, L // rows, chunk, 0)


def _conv(proj, w32, b, lg, lb, L, nseq, row0):
    vec = pl.BlockSpec((1, CONV_CH), lambda s: (0, 0))
    return pl.pallas_call(
        functools.partial(_conv_body, L),
        grid=(nseq,),
        in_specs=[pl.BlockSpec((L, CONV_CH), lambda s: (row0 + s, COL_CA // CONV_CH)),
                  pl.BlockSpec((L, CONV_CH), lambda s: (row0 + s, COL_CG // CONV_CH)),
                  pl.BlockSpec((32, CONV_CH), lambda s: (0, 0)), vec, vec, vec],
        out_specs=pl.BlockSpec((L, CONV_CH), lambda s: (s, 0)),
        out_shape=jax.ShapeDtypeStruct((nseq * L, CONV_CH), BF16),
        scratch_shapes=[pltpu.VMEM((L + 2 * PAD, CONV_CH), F32), pltpu.VMEM((CONV_ROWS, CONV_CH), F32)],
        compiler_params=_cparams(("parallel",), 48),
        name="conv_L%d" % L,
    )(proj, proj, w32, b, lg, lb)


def _pool_body(L, G, u_ref, w_ref, sc_ref, o_ref, pad_ref):
    rows = 256
    zeros = jnp.zeros((PAD, 512), F32)
    pad_ref[0:PAD, :] = zeros
    pad_ref[PAD + L:2 * PAD + L, :] = zeros
    for s in range(G):
        pad_ref[PAD:PAD + L, :] = u_ref[s * L:(s + 1) * L, :]
        for c in range(L // rows):
            r0 = c * rows
            t = lax.broadcasted_iota(jnp.int32, (rows, 1), 0) + r0
            for g, w in enumerate(POOL_WINDOWS):
                cols = slice(g * POOL_GROUP_CH, (g + 1) * POOL_GROUP_CH)
                acc = jnp.zeros((rows, POOL_GROUP_CH), F32)
                for i in range(-(w // 2), w - w // 2):
                    acc = acc + pad_ref[PAD + r0 + i:PAD + r0 + i + rows, cols]
                lo = jnp.maximum(t - w // 2, 0)
                hi = jnp.minimum(t + (w - w // 2), L)
                cnt = (hi - lo).astype(F32)
                pooled = acc / cnt - pad_ref[PAD + r0:PAD + r0 + rows, cols]
                mixed = jnp.dot(pooled.astype(BF16), w_ref[g], preferred_element_type=F32)
                o_ref[s * L + r0:s * L + r0 + rows, cols] = (mixed * sc_ref[:, cols]).astype(BF16)


def _pool(proj, pw_bf, pscale, L, nseq, row0, G):
    return pl.pallas_call(
        functools.partial(_pool_body, L, G),
        grid=(nseq // G,),
        in_specs=[pl.BlockSpec((G * L, 512), lambda s: (row0 + s, COL_POOL // 512)),
                  pl.BlockSpec((4, POOL_GROUP_CH, POOL_GROUP_CH), lambda s: (0, 0, 0)),
                  pl.BlockSpec((1, 512), lambda s: (0, 0))],
        out_specs=pl.BlockSpec((G * L, 512), lambda s: (s, 0)),
        out_shape=jax.ShapeDtypeStruct((nseq * L, 512), BF16),
        scratch_shapes=[pltpu.VMEM((L + 2 * PAD, 512), F32)],
        compiler_params=_cparams(("parallel",), 48),
        name="pool_L%d" % L,
    )(proj, pw_bf, pscale)


def _channel_dft(u, wc_ref, z_ref, r0, L):
    rows = u.shape[0]
    for h in range(FOURIER_HEADS):
        cols = slice(h * FOURIER_HEAD_CH, (h + 1) * FOURIER_HEAD_CH)
        a = jnp.dot(u[:, cols].astype(BF16), wc_ref[...], preferred_element_type=F32)
        z_ref[r0:r0 + rows, cols] = a[:, :FOURIER_HEAD_CH].astype(BF16)
        z_ref[L + r0:L + r0 + rows, cols] = a[:, FOURIER_HEAD_CH:].astype(BF16)


def _four_body(L, G, u_ref, wc_ref, wl_ref, o_ref, z_ref):
    scale = (L * FOURIER_HEAD_CH) ** -0.5
    if wl_ref.shape[0] == L:
        for s in range(G):
            rows = slice(s * L, (s + 1) * L)
            _channel_dft(u_ref[rows, :], wc_ref, z_ref, 0, L)
            o_ref[rows, :] = (jnp.dot(wl_ref[...], z_ref[...], preferred_element_type=F32) * scale).astype(BF16)
    else:
        @pl.when(pl.program_id(1) == 0)
        def _():
            rows = 512
            for c in range(L // rows):
                _channel_dft(u_ref[c * rows:(c + 1) * rows, :], wc_ref, z_ref, c * rows, L)

        o_ref[...] = (jnp.dot(wl_ref[...], z_ref[...], preferred_element_type=F32) * scale).astype(BF16)


def _four(pf, wc_bf, wl_bf, L, nseq, row0, G):
    tr = min(L, 512)
    return pl.pallas_call(
        functools.partial(_four_body, L, G),
        grid=(nseq // G, L // tr),
        in_specs=[pl.BlockSpec((G * L, 512), lambda s, r: (row0 + s, 0)),
                  pl.BlockSpec((FOURIER_HEAD_CH, 2 * FOURIER_HEAD_CH), lambda s, r: (0, 0)),
                  pl.BlockSpec((tr, 2 * L), lambda s, r: (r, 0))],
        out_specs=pl.BlockSpec((G * tr, 512), lambda s, r: (s * (L // tr) + r, 0)),
        out_shape=jax.ShapeDtypeStruct((nseq * L, 512), BF16),
        scratch_shapes=[pltpu.VMEM((2 * L, 512), BF16)],
        compiler_params=_cparams(("parallel", "arbitrary"), 48),
        name="four_L%d" % L,
    )(pf, wc_bf, wl_bf)


def _route(logits):
    lane = lax.broadcasted_iota(jnp.int32, logits.shape, 1).astype(F32)
    neg = jnp.float32(-jnp.inf)
    big = jnp.float32(1 << 20)
    is_g = (lane >= N_EXPERTS) & (lane < N_EXPERTS + N_EXPERT_GROUPS)
    gl = jnp.where(is_g, logits, neg)
    gmax = jnp.max(gl, axis=-1, keepdims=True)
    g_idx = jnp.min(jnp.where(gl == gmax, lane, big), axis=-1, keepdims=True) - N_EXPERTS
    denom = jnp.sum(jnp.where(is_g, jnp.exp(gl - gmax), 0.0), axis=-1, keepdims=True)
    g_w = 1.0 / denom
    in_grp = (lane >= g_idx * EXPERTS_PER_GROUP) & (lane < (g_idx + 1) * EXPERTS_PER_GROUP)
    el = jnp.where(in_grp, logits, neg)
    v1 = jnp.max(el, axis=-1, keepdims=True)
    i1 = jnp.min(jnp.where(el == v1, lane, big), axis=-1, keepdims=True)
    el2 = jnp.where(lane == i1, neg, el)
    v2 = jnp.max(el2, axis=-1, keepdims=True)
    i2 = jnp.min(jnp.where(el2 == v2, lane, big), axis=-1, keepdims=True)
    e2 = jnp.exp(v2 - v1)
    p1 = 1.0 / (1.0 + e2)
    p2 = e2 / (1.0 + e2)
    gates = g_w * jnp.where(lane == i1, p1, jnp.where(lane == i2, p2, 0.0))
    return gates + jnp.where(lane == g_idx + N_EXPERTS, 1.0, 0.0)


def _earlier_matrix(rows):
    row = lax.broadcasted_iota(jnp.int32, (rows, rows), 0)
    col = lax.broadcasted_iota(jnp.int32, (rows, rows), 1)
    return jnp.where(col < row, 1.0, 0.0).astype(BF16)


def _bucket_ranks(route, carry_ref, earlier):
    lane = lax.broadcasted_iota(jnp.int32, route.shape, 1)
    is_g = (lane >= N_EXPERTS) & (lane < N_EXPERTS + N_EXPERT_GROUPS)
    onehot = jnp.where(is_g, route, 0.0)
    before =jnp.dot(earlier, onehot.astype(BF16), preferred_element_type=F32) + carry_ref[...]
    rank = jnp.sum(onehot * before, axis=-1, keepdims=True)
    grp = jnp.sum(onehot * (lane - N_EXPERTS).astype(F32), axis=-1, keepdims=True)
    packed = jnp.where(lane == 0, rank, jnp.where(lane == 1, grp, 0.0))
    carry_ref[...] += jnp.sum(onehot, axis=0, keepdims=True)
    return packed.T[0:SUBLANES, :].astype(jnp.int32)


def _pair_specs(width):
    n = T_CTX // TM
    return [pl.BlockSpec((TM, width), lambda i: (jnp.minimum(i, n - 1), 0)),
            pl.BlockSpec((TM, width), lambda i: (jnp.maximum(i - n, 0), 0))]


def _outproj_body(nx, *refs):
    mixers = refs[nx:nx + 8]
    (w_ref, g1_ref, gain_ref, sc_ref, sh_ref, wr_ref, br_ref,
     xo_ref, hg_ref, route_ref, rk_ref, cnt_ref, wb_ref, carry_ref) = refs[nx + 8:]

    @pl.when(pl.program_id(0) == 0)
    def _():
        wb_ref[...] = w_ref[...].astype(BF16)
        carry_ref[...] = jnp.zeros_like(carry_ref)

    def residual(c):
        rows = slice(c * OUT_ROWS, (c + 1) * OUT_ROWS)
        mix = None
        for k in range(4):
            part = jnp.dot(_load_x(mixers[2 * k:2 * k + 2], rows), wb_ref[512 * k:512 * (k + 1), :],
                           preferred_element_type=F32)
            mix = part if mix is None else mix + part
        return _load_x(refs[:nx], rows) + g1_ref[...] * mix

    nchunk = TM // OUT_ROWS
    earlier = _earlier_matrix(OUT_ROWS)
    x_next = residual(0)
    for c in range(nchunk):
        rows = slice(c * OUT_ROWS, (c + 1) * OUT_ROWS)
        x = x_next
        x_next = residual(c + 1) if c + 1 < nchunk else None
        xo_ref[rows, :] = x
        ms = jnp.mean(x * x, axis=-1, keepdims=True)
        h = x * lax.rsqrt(ms + EPS) * gain_ref[...]
        h = h * (1.0 + sc_ref[...]) + sh_ref[...]
        h_hi = h.astype(BF16)
        h_lo = (h - h_hi.astype(F32)).astype(BF16)
        t = (jnp.dot(h_hi, wr_ref[...], preferred_element_type=F32)
             + jnp.dot(h_lo, wr_ref[...], preferred_element_type=F32))
        logits = t + pltpu.roll(t, LANES - ROUTER_LO_LANE, axis=1) + br_ref[...]
        route = _route(logits)
        route_ref[rows, :] = route
        rk_ref[:, rows] = _bucket_ranks(route, carry_ref, earlier)
        rec0 = c * OUT_ROWS * REC
        for s in range(H_ROWS):
            hg_ref[pl.ds(rec0 + s, OUT_ROWS, stride=REC), :] = h[:, s * LANES:(s + 1) * LANES]
    cnt_ref[...] = carry_ref[...]


def _outproj(mixers, xs, w_out, mod4, gain, wr, br, layer):
    tile = lambda w: pl.BlockSpec((TM, w), lambda i: (i, 0))
    const = lambda r, c: pl.BlockSpec((r, c), lambda i: (0, 0))
    mix_specs = []
    for _ in range(4):
        mix_specs += _pair_specs(512)
    return pl.pallas_call(
        functools.partial(_outproj_body, len(xs)),
        grid=(T_ALL // TM,),
        in_specs=_x_specs(len(xs)) + mix_specs + [_resident((2048, D), layer),
                                                  _mod_spec(2, TM), const(1, D), _mod_spec(4, TM), _mod_spec(3, TM),
                                                  const(D, LANES), const(1, LANES)],
        out_specs=[tile(D), pl.BlockSpec((TM * REC, LANES), lambda i: (i, 0)), tile(LANES),
                   pl.BlockSpec((SUBLANES, TM), lambda i: (i, 0)), const(1, LANES)],
        out_shape=[jax.ShapeDtypeStruct((T_ALL, D), F32),
                   jax.ShapeDtypeStruct((T_ALL * REC, LANES), F32),
                   jax.ShapeDtypeStruct((T_ALL, LANES), F32),
                   jax.ShapeDtypeStruct((T_ALL // TM * SUBLANES, TM), jnp.int32),
                   jax.ShapeDtypeStruct((1, LANES), F32)],
        scratch_shapes=[pltpu.VMEM((2048, D), BF16), pltpu.VMEM((1, LANES), F32)],
        compiler_params=_cparams(("arbitrary",), 56),
        name="out_proj",
    )(*xs, *[a for pair in mixers for a in pair], w_out, mod4, gain, mod4, mod4, wr, br)


def _wait_rows(copy, n):
    def body(t, c):
        copy.wait()
        return c
    lax.fori_loop(0, n, body, 0, unroll=8)


def _record(ref, idx, rows):
    start = idx * rows if isinstance(idx, int) else pl.multiple_of(idx * rows, rows)
    return ref.at[pl.ds(start, rows)]


def _dispatch_body(slot_sm, fstart_sm, flen_sm, hg_ref, route_ref, xs_hbm, gs_ref, zrec_ref, sem):
    i = pl.program_id(0)
    base = i * DT

    @pl.when(i == 0)
    def _():
        gs_ref[...] = jnp.zeros_like(gs_ref)
        zrec_ref[...] = jnp.zeros_like(zrec_ref)
        for g in range(N_EXPERT_GROUPS):
            def fill(r, c):
                pltpu.make_async_copy(zrec_ref, _record(xs_hbm, fstart_sm[g] + r, REC), sem.at[1]).start()
                return c
            lax.fori_loop(0, flen_sm[g], fill, 0)
        _wait_rows(pltpu.make_async_copy(zrec_ref, _record(xs_hbm, 0, REC), sem.at[1]), NSLOT - T_ALL)

    def issue(pair, c):
        for prio in range(2):
            t = 2 * pair + prio
            slot = slot_sm[base + t]
            pltpu.async_copy(_record(hg_ref, t, REC), _record(xs_hbm, slot, REC), sem.at[0], priority=prio)
            gs_ref[pl.ds(slot, 1), :] = route_ref[pl.ds(t, 1), :]
        return c
    lax.fori_loop(0, DT // 2, issue, 0, unroll=4)
    _wait_rows(pltpu.make_async_copy(_record(hg_ref, 0, REC), _record(xs_hbm, 0, REC), sem.at[0]), DT)


def _dispatch(slot, fstart, flen, hg, route):
    return pl.pallas_call(
        _dispatch_body,
        grid_spec=pltpu.PrefetchScalarGridSpec(
            num_scalar_prefetch=3,
            grid=(T_ALL // DT,),
            in_specs=[pl.BlockSpec((DT * REC, LANES), lambda i, s, f0, f1: (i, 0)),
                      pl.BlockSpec((DT, LANES), lambda i, s, f0, f1: (i, 0))],
            out_specs=[pl.BlockSpec(memory_space=pl.ANY),
                       pl.BlockSpec((NSLOT, LANES), lambda i, s, f0, f1: (0, 0), pipeline_mode=pl.Buffered(1))],
            scratch_shapes=[pltpu.VMEM((REC, LANES), F32), pltpu.SemaphoreType.DMA((2,))]),
        out_shape=[jax.ShapeDtypeStruct((NSLOT * REC, LANES), F32),
                   jax.ShapeDtypeStruct((NSLOT, LANES), F32)],
        compiler_params=_cparams(("arbitrary",), 48),
        name="moe_dispatch",
    )(slot, fstart, flen, hg, route)


def _ffn_body(tgrp_sm, xs_ref, gs_ref, wg32_ref, wu32_ref, wd32_ref, ys_ref, h_ref, wg_ref, wu_ref, wd_ref):
    j = pl.program_id(0)
    grp = tgrp_sm[j]

    @pl.when(jnp.logical_or(j == 0, grp != tgrp_sm[jnp.maximum(j - 1, 0)]))
    def _():
        for e in range(EXPERTS_PER_GROUP):
            cols = slice(e * EXPERT_FF, (e + 1) * EXPERT_FF)
            wg_ref[:, cols] = wg32_ref[e].astype(BF16)
            wu_ref[:, cols] = wu32_ref[e].astype(BF16)
            wd_ref[cols, :] = wd32_ref[e].astype(BF16)

    for s in range(H_ROWS):
        h_ref[:, s * LANES:(s + 1) * LANES] = xs_ref[pl.ds(s, TS, stride=REC), :].astype(BF16)
    gates = gs_ref[...]
    h = h_ref[...]
    a = jnp.dot(h, wg_ref[...], preferred_element_type=F32)
    b = jnp.dot(h, wu_ref[...], preferred_element_type=F32)
    hid = _silu(a) * b
    lane = lax.broadcasted_iota(jnp.int32, gates.shape, 1)
    parts = []
    for e in range(EXPERTS_PER_GROUP):
        ge = jnp.sum(jnp.where(lane == grp * EXPERTS_PER_GROUP + e, gates, 0.0), axis=-1, keepdims=True)
        parts.append((hid[:, e * EXPERT_FF:(e + 1) * EXPERT_FF] * ge).astype(BF16))
    y = jnp.dot(jnp.concatenate(parts, axis=1), wd_ref[...], preferred_element_type=F32)
    for s in range(H_ROWS):
        ys_ref[pl.ds(s, TS, stride=H_ROWS), :] = y[:, s * LANES:(s + 1) * LANES]


def _ffn(tile_grp, xs, gs, w_gate, w_up, w_down, layer):
    ffw = EXPERTS_PER_GROUP * EXPERT_FF
    experts = lambda r, c: pl.BlockSpec((None, EXPERTS_PER_GROUP, r, c), lambda j, tg: (layer, tg[j], 0, 0))
    return pl.pallas_call(
        _ffn_body,
        grid_spec=pltpu.PrefetchScalarGridSpec(
            num_scalar_prefetch=1,
            grid=(NT,),
            in_specs=[pl.BlockSpec((TS * REC, LANES), lambda j, tg: (j, 0)),
                      pl.BlockSpec((TS, LANES), lambda j, tg: (j, 0)),
                      experts(D, EXPERT_FF), experts(D, EXPERT_FF), experts(EXPERT_FF, D)],
            out_specs=pl.BlockSpec((TS * H_ROWS, LANES), lambda j, tg: (j, 0)),
            scratch_shapes=[pltpu.VMEM((TS, D), BF16), pltpu.VMEM((D, ffw), BF16),
                            pltpu.VMEM((D, ffw), BF16), pltpu.VMEM((ffw, D), BF16)]),
        out_shape=jax.ShapeDtypeStruct((NSLOT * H_ROWS, LANES), F32),
        compiler_params=_cparams(("arbitrary",), 56),
        name="moe_ffn",
    )(tile_grp, xs, gs, w_gate, w_up, w_down)


def _combine_body(final, slot_sm, ys_hbm, x_ref, g2_ref, fg_ref, *rest):
    outs, buf, sem = rest[:-2], rest[-2], rest[-1]
    i = pl.program_id(0)
    n = pl.num_programs(0)

    def issue(step, b):
        def body(pair, c):
            for prio in range(2):
                t = 2 * pair + prio
                pltpu.async_copy(_record(ys_hbm, slot_sm[step * TM + t], H_ROWS),
                                 _record(buf.at[b], t, H_ROWS), sem.at[b], priority=prio)
            return c
        lax.fori_loop(0, TM // 2, body, 0, unroll=4)

    @pl.when(i == 0)
    def _():
        issue(0, 0)

    @pl.when(i + 1 < n)
    def _():
        issue(i + 1, (i + 1) % 2)

    b = i % 2
    _wait_rows(pltpu.make_async_copy(_record(ys_hbm, 0, H_ROWS), _record(buf.at[b], 0, H_ROWS), sem.at[b]), TM)
    y = jnp.concatenate([buf[b, pl.ds(s, TM, stride=H_ROWS), :] for s in range(H_ROWS)], axis=1)
    x = x_ref[...] + g2_ref[...] * y
    if not final:
        outs[0][...] = x
    else:
        ms = jnp.mean(x * x, axis=-1, keepdims=True)
        yn = x * lax.rsqrt(ms + EPS) * fg_ref[...]
        is_ctx = i < T_CTX // TM

        @pl.when(is_ctx)
        def _():
            outs[0][...] = yn

        @pl.when(jnp.logical_not(is_ctx))
        def _():
            outs[1][...] = yn


def _combine(slot, ys, x, mod4, final_gain, final):
    if final:
        nc = T_CTX // TM
        out_specs = [pl.BlockSpec((TM, D), lambda i, s: (jnp.minimum(i, nc - 1), 0)),
                     pl.BlockSpec((TM, D), lambda i, s: (jnp.maximum(i - nc, 0), 0))]
        out_shape = [jax.ShapeDtypeStruct((T_CTX, D), F32), jax.ShapeDtypeStruct((T_LAT, D), F32)]
    else:
        out_specs = pl.BlockSpec((TM, D), lambda i, s: (i, 0))
        out_shape = jax.ShapeDtypeStruct((T_ALL, D), F32)
    return pl.pallas_call(
        functools.partial(_combine_body, final),
        grid_spec=pltpu.PrefetchScalarGridSpec(
            num_scalar_prefetch=1,
            grid=(T_ALL // TM,),
            in_specs=[pl.BlockSpec(memory_space=pl.ANY),
                      pl.BlockSpec((TM, D), lambda i, s: (i, 0)),
                      pl.BlockSpec((None, None, 1, D), lambda i, s: (_mod_row(i, TM), 5, 0, 0)),
                      pl.BlockSpec((1, D), lambda i, s: (0, 0))],
            out_specs=out_specs,
            scratch_shapes=[pltpu.VMEM((2, TM * H_ROWS, LANES), F32), pltpu.SemaphoreType.DMA((2,))]),
        out_shape=out_shape,
        compiler_params=_cparams(("arbitrary",), 48),
        name="moe_combine",
    )(slot, ys, x, mod4, final_gain)


def _moe_plan(rk, cnt):
    rk = rk.reshape(T_ALL // TM, SUBLANES, TM)
    rank = rk[:, 0, :].reshape(T_ALL)
    grp = rk[:, 1, :].reshape(T_ALL)
    cnt = cnt[0, N_EXPERTS:N_EXPERTS + N_EXPERT_GROUPS].astype(jnp.int32)
    padded = (cnt + TS - 1) // TS * TS
    off = jnp.cumsum(padded) - padded
    padded = padded.at[N_EXPERT_GROUPS - 1].set(NSLOT - off[N_EXPERT_GROUPS - 1])
    slot = rank
    for g in range(1, N_EXPERT_GROUPS):
        slot = slot + jnp.where(grp == g, off[g], 0)
    ends = (off + padded)[:N_EXPERT_GROUPS - 1]
    tile_grp = jnp.sum((jnp.arange(NT, dtype=jnp.int32)[:, None] * TS >= ends[None, :]).astype(jnp.int32), axis=1)
    return slot, tile_grp, off + cnt, padded - cnt


def _rope_tables():
    t = np.arange(DEC_SEQ)
    pos = np.stack([t // GRID_W, t % GRID_W], axis=1).astype(np.float64)
    inv = ROPE_THETA ** (-np.arange(16, dtype=np.float64) / 16.0)
    d = np.arange(LANES) % HEAD_DIM
    pair = d // 2
    ang = pos[:, (pair >= 16).astype(np.int64)] * inv[pair % 16][None, :]
    sign = np.where(d % 2 == 0, -1.0, 1.0)
    return np.cos(ang).astype(np.float32), (np.sin(ang) * sign[None, :]).astype(np.float32)


def _dft_pos(L):
    j = np.arange(L)
    ang = 2.0 * np.pi * ((j[:, None] * j[None, :]) % L) / L
    return np.concatenate([np.cos(ang), -np.sin(ang)], axis=1).astype(np.float32)


def _dft_ch():
    c = np.arange(FOURIER_HEAD_CH)
    ang = 2.0 * np.pi * ((c[:, None] * c[None, :]) % FOURIER_HEAD_CH) / FOURIER_HEAD_CH
    return np.concatenate([np.cos(ang), np.sin(ang)], axis=1).astype(np.float32)


def kernel(x_prompt, x_sample, cache_k, cache_v, c, c_ctx, w_ada, b_ada, norm_mix, norm_ffn, w_in, w_out, q_norm, k_norm, conv_w, conv_b, conv_ln_g, conv_ln_b, pool_w, pool_scale, router_group_w, router_group_b, router_expert_w, router_expert_b, moe_w_gate, moe_w_up, moe_w_down, final_norm):
    xs = (x_prompt.reshape(T_CTX, D), x_sample.reshape(T_LAT, D))
    cond8 =jnp.concatenate([c_ctx[None, :], c, jnp.zeros((3, D), F32)], axis=0)
    mod = _ada(cond8, w_ada, b_ada)

    pw_bf = pool_w.astype(BF16)
    n_route = N_EXPERTS + N_EXPERT_GROUPS
    wr = jnp.concatenate([router_expert_w, router_group_w], axis=-1)
    wr_hi = wr.astype(BF16)
    wr_lo = (wr - wr_hi.astype(F32)).astype(BF16)
    w_router = jnp.concatenate([wr_hi, jnp.zeros((DEPTH, D, ROUTER_LO_LANE - n_route), BF16), wr_lo,
                                jnp.zeros((DEPTH, D, LANES - ROUTER_LO_LANE - n_route), BF16)], axis=-1)
    b_router = jnp.concatenate([router_expert_b, router_group_b,
                                jnp.zeros((DEPTH, LANES - N_EXPERTS - N_EXPERT_GROUPS), F32)], axis=-1)
    conv_w32 = jnp.concatenate([conv_w, jnp.zeros((DEPTH, 1, CONV_CH), F32)], axis=1)

    cos_np, sin_np = _rope_tables()
    cos, sin = jnp.asarray(cos_np), jnp.asarray(sin_np)
    wc_bf = jnp.asarray(_dft_ch()).astype(BF16)
    wl_ctx = jnp.asarray(_dft_pos(SEQ)).astype(BF16)
    wl_lat = jnp.asarray(_dft_pos(DEC_SEQ)).astype(BF16)

    cache_k4 = cache_k.reshape(DEC_BATCH, DEPTH, PAST_LEN, KV_CH)
    cache_v4 = cache_v.reshape(DEC_BATCH, DEPTH, PAST_LEN, KV_CH)
    new_k = jnp.zeros((BATCH, DEPTH, SEQ, KV_CH), F32)
    new_v = jnp.zeros((BATCH, DEPTH, SEQ, KV_CH), F32)
    fgain = final_norm.reshape(1, D)

    for l in range(DEPTH):
        mod4 = mod[l].reshape(8, N_MOD, 1, D)
        proj, pf = _inproj(xs, norm_mix[l].reshape(1, D), mod4, w_in, l)
        qg = jnp.tile(q_norm[l], 2).reshape(1, LANES)
        kg = jnp.tile(k_norm[l], 2).reshape(1, LANES)
        attn_c, new_k, new_v = _ctx_attn(proj, qg, kg, new_k, new_v, l)
        attn_l = _lat_attn(proj, cache_k4, cache_v4, qg, kg, cos, sin, l)
        cb, lg, lb = conv_b[l].reshape(1, -1), conv_ln_g[l].reshape(1, -1), conv_ln_b[l].reshape(1, -1)
        ps = pool_scale[l].reshape(1, -1)
        lat0 = T_CTX // DEC_SEQ
        mixers = [(attn_c, attn_l),
                  (_conv(proj, conv_w32[l], cb, lg, lb, SEQ, BATCH, 0),
                   _conv(proj, conv_w32[l], cb, lg, lb, DEC_SEQ, DEC_BATCH, lat0)),
                  (_pool(proj, pw_bf[l], ps, SEQ, BATCH, 0, MIX_G),
                   _pool(proj, pw_bf[l], ps, DEC_SEQ, DEC_BATCH, lat0, 1)),
                  (_four(pf, wc_bf, wl_ctx, SEQ, BATCH, 0, MIX_G),
                   _four(pf, wc_bf, wl_lat, DEC_SEQ, DEC_BATCH, lat0, 1))]
        x, hg, route, rk, cnt = _outproj(mixers, xs, w_out, mod4, norm_ffn[l].reshape(1, D),
                                         w_router[l], b_router[l].reshape(1, LANES), l)
        slot, tile_grp, fstart, flen = _moe_plan(rk, cnt)
        xslots, gslots = _dispatch(slot, fstart, flen, hg, route)
        yslots = _ffn(tile_grp, xslots, gslots, moe_w_gate, moe_w_up, moe_w_down, l)
        out = _combine(slot, yslots, x, mod4, fgain, l == DEPTH - 1)
        xs = (out,)

    y_prompt = out[0].reshape(BATCH, SEQ, D)
    y_sample = out[1].reshape(DEC_BATCH, DEC_SEQ, D)
    new_k = new_k.reshape(BATCH, DEPTH, SEQ, 2, HEAD_DIM)
    new_v = new_v.reshape(BATCH, DEPTH, SEQ, 2, HEAD_DIM)
    return (y_prompt, y_sample, new_k, new_v)
```

```python
import functools

import numpy as np
import jax
import jax.numpy as jnp
from jax import lax
from jax.experimental import pallas as pl
from jax.experimental.pallas import tpu as pltpu

F32 = jnp.float32
BF16 = jnp.bfloat16

D = 1024
BATCH = 32
SEQ = 256
DEPTH = 2
DEC_BATCH = 4
DEC_SEQ = 2048
PAST_LEN = 256
GRID_W = 64
HEAD_DIM = 64
ATTN_CH = 512
KV_CH = 128
CONV_CH = 512
CONV_WIDTH = 31
POOL_WINDOWS = (2, 4, 8, 16)
POOL_GROUP_CH = 128
FOURIER_HEAD_CH = 128
FOURIER_HEADS = 4
N_EXPERT_GROUPS = 4
EXPERTS_PER_GROUP = 4
N_EXPERTS = 16
EXPERT_FF = 256
ROPE_THETA = 10000.0
EPS = 1e-6
N_MOD = 6

T_CTX = BATCH * SEQ
T_LAT = DEC_BATCH * DEC_SEQ
T_ALL = T_CTX + T_LAT
LANES = 128
SUBLANES = 8
PAD = 16
CONV_ROWS = 128
ROUTER_LO_LANE = 32
IN_COLS = 2816
COL_Q, COL_CA, COL_CG, COL_POOL, COL_K, COL_V, COL_FOUR = 0, 512, 1024, 1536, 2048, 2176, 2304
F32_COLS = COL_FOUR

TM = 512
OUT_ROWS = 256
DT = 2048
CTX_G = 2
MIX_G = 4
TS = 512
NT = T_ALL // TS + N_EXPERT_GROUPS
NSLOT = NT * TS
H_ROWS = D // LANES
REC = H_ROWS
TQ = 256
LK = PAST_LEN + DEC_SEQ


def _cparams(sem, vmem_mb):
    return pltpu.CompilerParams(dimension_semantics=sem, vmem_limit_bytes=vmem_mb * 1024 * 1024)


def _mod_row(i, tm):
    nctx = T_CTX // tm
    per = DEC_SEQ // tm
    return jnp.where(i < nctx, 0, 1 + (i - nctx) // per)


def _mod_spec(k, tm):
    return pl.BlockSpec((None, None, 1, D), lambda i: (_mod_row(i, tm), k, 0, 0))


def _silu(x):
    return x * jax.nn.sigmoid(x)


def _ada_body(c_ref, w_ref, b_ref, o_ref):
    s = _silu(c_ref[...])
    o_ref[...] = jnp.dot(s.astype(BF16), w_ref[...].astype(BF16), preferred_element_type=F32) + b_ref[...]


def _ada(cond8, w_ada, b_ada):
    tn = 1536
    n = N_MOD * D
    return pl.pallas_call(
        _ada_body,
        grid=(DEPTH, n // tn),
        in_specs=[pl.BlockSpec((8, D), lambda l, j: (0, 0)),
                  pl.BlockSpec((None, D, tn), lambda l, j: (l, 0, j)),
                  pl.BlockSpec((None, 1, tn), lambda l, j: (l, 0, j))],
        out_specs=pl.BlockSpec((None, 8, tn), lambda l, j: (l, 0, j)),
        out_shape=jax.ShapeDtypeStruct((DEPTH, 8, n), F32),
        compiler_params=_cparams(("arbitrary", "arbitrary"), 40),
        name="ada_mod",
    )(cond8, w_ada, b_ada.reshape(DEPTH, 1, n))


def _x_specs(nx):
    if nx == 1:
        return [pl.BlockSpec((TM, D), lambda i: (i, 0))]
    n = T_CTX // TM
    return [pl.BlockSpec((TM, D), lambda i: (jnp.minimum(i, n - 1), 0)),
            pl.BlockSpec((TM, D), lambda i: (jnp.maximum(i - n, 0), 0))]


def _load_x(x_refs, rows=slice(None)):
    if len(x_refs) == 1:
        return x_refs[0][rows, :]
    return jnp.where(pl.program_id(0) < T_CTX // TM, x_refs[0][rows, :], x_refs[1][rows, :])


def _resident(shape, layer):
    return pl.BlockSpec((None,) + shape, lambda *_: (layer,) + (0,) * len(shape), pipeline_mode=pl.Buffered(1))


def _inproj_body(nx, *refs):
    g_ref, sc_ref, sh_ref, w_ref, o_ref, of_ref, wb_ref = refs[nx:]

    @pl.when(pl.program_id(0) == 0)
    def _():
        kv0, kv1 = ATTN_CH, ATTN_CH + 2 * KV_CH
        wb_ref[:, 0:kv0] = w_ref[:, 0:kv0].astype(BF16)
        wb_ref[:, kv0:COL_K] = w_ref[:, kv1:kv1 + COL_K - kv0].astype(BF16)
        wb_ref[:, COL_K:COL_FOUR] = w_ref[:, kv0:kv1].astype(BF16)
        wb_ref[:, COL_FOUR:IN_COLS] = w_ref[:, COL_FOUR:IN_COLS].astype(BF16)

    x = _load_x(refs[:nx])
    ms = jnp.mean(x * x, axis=-1, keepdims=True)
    h = x * lax.rsqrt(ms + EPS) * g_ref[...]
    h = h * (1.0 + sc_ref[...]) + sh_ref[...]
    hb = h.astype(BF16)
    o_ref[...] = jnp.dot(hb, wb_ref[:, 0:F32_COLS], preferred_element_type=F32)
    of_ref[...] = jnp.dot(hb, wb_ref[:, F32_COLS:IN_COLS], preferred_element_type=F32).astype(BF16)


def _inproj(xs, gain, mod4, w_in, layer):
    return pl.pallas_call(
        functools.partial(_inproj_body, len(xs)),
        grid=(T_ALL // TM,),
        in_specs=_x_specs(len(xs)) + [pl.BlockSpec((1, D), lambda i: (0, 0)),
                                      _mod_spec(1, TM), _mod_spec(0, TM),
                                      _resident((D, IN_COLS), layer)],
        out_specs=[pl.BlockSpec((TM, F32_COLS), lambda i: (i, 0)),
                   pl.BlockSpec((TM, IN_COLS - F32_COLS), lambda i: (i, 0))],
        out_shape=[jax.ShapeDtypeStruct((T_ALL, F32_COLS), F32),
                   jax.ShapeDtypeStruct((T_ALL, IN_COLS - F32_COLS), BF16)],
        scratch_shapes=[pltpu.VMEM((D, IN_COLS), BF16)],
        compiler_params=_cparams(("arbitrary",), 56),
        name="in_proj",
    )(*xs, gain, mod4, mod4, w_in)


def _lane_lo():
    return lax.broadcasted_iota(jnp.int32, (1, LANES), 1) < HEAD_DIM


def _head_norm(x, gain):
    lo = _lane_lo()
    x2 = x * x
    s_lo = jnp.sum(jnp.where(lo, x2, 0.0), axis=-1, keepdims=True)
    s_hi = jnp.sum(jnp.where(lo, 0.0, x2), axis=-1, keepdims=True)
    r = jnp.where(lo, lax.rsqrt(s_lo * (1.0 / HEAD_DIM) + EPS), lax.rsqrt(s_hi * (1.0 / HEAD_DIM) + EPS))
    return x * r * gain


def _rope(x, cos, sin_signed):
    even = (lax.broadcasted_iota(jnp.int32, (1, LANES), 1) % 2) == 0
    swapped = jnp.where(even, pltpu.roll(x, LANES - 1, axis=1), pltpu.roll(x, 1, axis=1))
    return x * cos + swapped * sin_signed


def _split_heads(x, g):
    lo = _lane_lo()
    own = jnp.where(lo if g == 0 else jnp.logical_not(lo), x, 0.0)
    other = pltpu.roll(own, HEAD_DIM, axis=1)
    return (own, other) if g == 0 else (other, own)


Q_SCALE = HEAD_DIM ** -0.5 * 1.4426950408889634


def _ones_cols(rows, lk):
    r = lax.broadcasted_iota(jnp.int32, (rows, LANES), 0)
    lane = lax.broadcasted_iota(jnp.int32, (rows, LANES), 1)
    return jnp.where(((lane == 0) & (r < lk)) | ((lane == 1) & (r >= lk)), 1.0, 0.0)


def _scores(q_bf, k_cat):
    return lax.dot_general(q_bf, k_cat, (((1,), (1,)), ((), ())), preferred_element_type=F32)


def _attend_tile(q_bf, k_cat, v_cat, lk):
    return _softmax_pv(_scores(q_bf, k_cat), v_cat, lk)


def _softmax_pv(s, v_cat, lk):
    s_lo, s_hi = s[:, :lk], s[:, lk:]
    p_lo = jnp.exp2(s_lo - jnp.max(s_lo, axis=-1, keepdims=True))
    p_hi = jnp.exp2(s_hi - jnp.max(s_hi, axis=-1, keepdims=True))
    p = jnp.concatenate([p_lo, p_hi], axis=1).astype(BF16)
    o = jnp.dot(p, v_cat, preferred_element_type=F32)
    l_lo, l_hi = o[:, LANES:LANES + 1], o[:, LANES + 1:LANES + 2]
    return o[:, :LANES] * jnp.where(_lane_lo(), 1.0 / l_lo, 1.0 / l_hi)


def _ctx_attn_body(q_ref, k_ref, v_ref, qg_ref, kg_ref, kc_in, vc_in, o_ref, kc_ref, vc_ref):
    del kc_in, vc_in
    ones = _ones_cols(2 * SEQ, SEQ)
    kv = []
    for s in range(CTX_G):
        rows = slice(s * SEQ, (s + 1) * SEQ)
        kn = _head_norm(k_ref[rows, :], kg_ref[...])
        v = v_ref[rows, :]
        kc_ref[s] = kn
        vc_ref[s] = v
        for g in range(2):
            k_lo, k_hi = _split_heads(kn, g)
            v_lo, v_hi = _split_heads(v, g)
            kv.append((jnp.concatenate([k_lo, k_hi], axis=0).astype(BF16),
                       jnp.concatenate([jnp.concatenate([v_lo, v_hi], axis=0), ones], axis=1).astype(BF16)))
    for t in range(ATTN_CH // LANES):
        for s in range(CTX_G):
            rows = slice(s * SEQ, (s + 1) * SEQ)
            k_cat, v_cat = kv[2 * s + t // 2]
            q = _head_norm(q_ref[rows, t * LANES:(t + 1) * LANES], qg_ref[...]) * Q_SCALE
            o = _attend_tile(q.astype(BF16), k_cat, v_cat, SEQ)
            o_ref[rows, t * LANES:(t + 1) * LANES] = o.astype(BF16)


def _ctx_attn(proj, qg, kg, kcache, vcache, layer):
    G = CTX_G
    return pl.pallas_call(
        _ctx_attn_body,
        grid=(BATCH // G,),
        in_specs=[pl.BlockSpec((G * SEQ, ATTN_CH), lambda b: (b, COL_Q // ATTN_CH)),
                  pl.BlockSpec((G * SEQ, KV_CH), lambda b: (b, COL_K // KV_CH)),
                  pl.BlockSpec((G * SEQ, KV_CH), lambda b: (b, COL_V // KV_CH)),
                  pl.BlockSpec((1, LANES), lambda b: (0, 0)),
                  pl.BlockSpec((1, LANES), lambda b: (0, 0)),
                  pl.BlockSpec(memory_space=pl.ANY),
                  pl.BlockSpec(memory_space=pl.ANY)],
        out_specs=[pl.BlockSpec((G * SEQ, ATTN_CH), lambda b: (b, 0)),
                   pl.BlockSpec((G, None, SEQ, KV_CH), lambda b: (b, layer, 0, 0)),
                   pl.BlockSpec((G, None, SEQ, KV_CH), lambda b: (b, layer, 0, 0))],
        out_shape=[jax.ShapeDtypeStruct((T_CTX, ATTN_CH), BF16),
                   jax.ShapeDtypeStruct((BATCH, DEPTH, SEQ, KV_CH), F32),
                   jax.ShapeDtypeStruct((BATCH, DEPTH, SEQ, KV_CH), F32)],
        input_output_aliases={5: 1, 6: 2},
        compiler_params=_cparams(("parallel",), 32),
        name="ctx_attn",
    )(proj, proj, proj, qg, kg, kcache, vcache)


def _lat_attn_body(q_ref, k_ref, v_ref, ck_ref, cv_ref, qg_ref, kg_ref, cosk_ref, sink_ref, cosq_ref, sinq_ref,
                   o_ref, kcat_ref, vcat_ref):
    @pl.when(pl.program_id(1) == 0)
    def _():
        rows = 256
        for c in range(LK // rows):
            if c == 0:
                kn = ck_ref[...]
                v = cv_ref[...]
            else:
                r0 = (c - 1) * rows
                kn = _head_norm(k_ref[r0:r0 + rows, :], kg_ref[...])
                kn = _rope(kn, cosk_ref[r0:r0 + rows, :], sink_ref[r0:r0 + rows, :])
                v = v_ref[r0:r0 + rows, :]
            for g in range(2):
                k_lo, k_hi = _split_heads(kn, g)
                v_lo, v_hi = _split_heads(v, g)
                kcat_ref[g, c * rows:(c + 1) * rows, :] = k_lo.astype(BF16)
                kcat_ref[g, LK + c * rows:LK + (c + 1) * rows, :] = k_hi.astype(BF16)
                vcat_ref[g, c * rows:(c + 1) * rows, 0:LANES] = v_lo.astype(BF16)
                vcat_ref[g, LK + c * rows:LK + (c + 1) * rows, 0:LANES] = v_hi.astype(BF16)
                vcat_ref[g, c * rows:(c + 1) * rows, LANES:2 * LANES] = _ones_cols(rows, rows).astype(BF16)
                vcat_ref[g, LK + c * rows:LK + (c + 1) * rows, LANES:2 * LANES] = _ones_cols(rows, 0).astype(BF16)

    def scores(t):
        q = _head_norm(q_ref[:, t * LANES:(t + 1) * LANES], qg_ref[...])
        q = _rope(q, cosq_ref[...], sinq_ref[...]) * Q_SCALE
        return _scores(q.astype(BF16), kcat_ref[t // 2])

    nt = ATTN_CH // LANES
    s = scores(0)
    for t in range(nt):
        s_next = scores(t + 1) if t + 1 < nt else None
        o = _softmax_pv(s, vcat_ref[t // 2], LK)
        o_ref[:, t * LANES:(t + 1) * LANES] = o.astype(BF16)
        s = s_next


def _lat_attn(proj, cache_k4, cache_v4, qg, kg, cos, sin, layer):
    nqb = DEC_SEQ // TQ
    row0 = T_CTX // DEC_SEQ
    return pl.pallas_call(
        _lat_attn_body,
        grid=(DEC_BATCH, nqb),
        in_specs=[pl.BlockSpec((TQ, ATTN_CH), lambda b, i: (T_CTX // TQ + b * nqb + i, COL_Q // ATTN_CH)),
                  pl.BlockSpec((DEC_SEQ, KV_CH), lambda b, i: (row0 + b, COL_K // KV_CH)),
                  pl.BlockSpec((DEC_SEQ, KV_CH), lambda b, i: (row0 + b, COL_V // KV_CH)),
                  pl.BlockSpec((None, None, PAST_LEN, KV_CH), lambda b, i: (b, layer, 0, 0)),
                  pl.BlockSpec((None, None, PAST_LEN, KV_CH), lambda b, i: (b, layer, 0, 0)),
                  pl.BlockSpec((1, LANES), lambda b, i: (0, 0)),
                  pl.BlockSpec((1, LANES), lambda b, i: (0, 0)),
                  pl.BlockSpec((DEC_SEQ, LANES), lambda b, i: (0, 0)),
                  pl.BlockSpec((DEC_SEQ, LANES), lambda b, i: (0, 0)),
                  pl.BlockSpec((TQ, LANES), lambda b, i: (i, 0)),
                  pl.BlockSpec((TQ, LANES), lambda b, i: (i, 0))],
        out_specs=pl.BlockSpec((TQ, ATTN_CH), lambda b, i: (b * nqb + i, 0)),
        out_shape=jax.ShapeDtypeStruct((T_LAT, ATTN_CH), BF16),
        scratch_shapes=[pltpu.VMEM((2, 2 * LK, LANES), BF16), pltpu.VMEM((2, 2 * LK, 2 * LANES), BF16)],
        compiler_params=_cparams(("parallel", "arbitrary"), 56),
        name="lat_attn",
    )(proj, proj, proj, cache_k4, cache_v4, qg, kg, cos, sin, cos, sin)


def _conv_body(L, a_ref, g_ref, w_ref, b_ref, lg_ref, lb_ref, o_ref, pad_ref, y_ref):
    rows = CONV_ROWS
    zeros = jnp.zeros((PAD, CONV_CH), F32)
    pad_ref[0:PAD, :] = zeros
    pad_ref[PAD + L:2 * PAD + L, :] = zeros
    pad_ref[PAD:PAD + L, :] = a_ref[...] * jax.nn.sigmoid(g_ref[...])
    shift0 = PAD - CONV_WIDTH // 2

    def chunk(c, carry):
        base = pl.multiple_of(c * rows, rows)
        for lg in range(CONV_CH // LANES):
            lanes = slice(lg * LANES, (lg + 1) * LANES)
            y = jnp.zeros((rows, LANES), F32)
            for r in range(SUBLANES):
                z = None
                for q in range((CONV_WIDTH + shift0) // SUBLANES + 1):
                    j = SUBLANES * q + r - shift0
                    if 0 <= j < CONV_WIDTH:
                        term = w_ref[j:j + 1, lanes] * pad_ref[pl.ds(base + SUBLANES * q, rows + SUBLANES), lanes]
                        z = term if z is None else z + term
                y = y + z[r:r + rows, :]
            y_ref[:, lanes] = y + b_ref[:, lanes]
        half = rows // 2
        for p in range(2):
            y = y_ref[p * half:(p + 1) * half, :]
            mu = jnp.mean(y, axis=-1, keepdims=True)
            yc = y - mu
            var = jnp.mean(yc * yc, axis=-1, keepdims=True)
            yn = yc * lax.rsqrt(var + EPS) * lg_ref[...] + lb_ref[...]
            o_ref[pl.ds(base + p * half, half), :] = _silu(yn).astype(BF16)
        return carry

    lax.fori_loop(0, L // rows, chunk, 0)


def _conv(proj, w32, b, lg, lb, L, nseq, row0):
    vec = pl.BlockSpec((1, CONV_CH), lambda s: (0, 0))
    return pl.pallas_call(
        functools.partial(_conv_body, L),
        grid=(nseq,),
        in_specs=[pl.BlockSpec((L, CONV_CH), lambda s: (row0 + s, COL_CA // CONV_CH)),
                  pl.BlockSpec((L, CONV_CH), lambda s: (row0 + s, COL_CG // CONV_CH)),
                  pl.BlockSpec((32, CONV_CH), lambda s: (0, 0)), vec, vec, vec],
        out_specs=pl.BlockSpec((L, CONV_CH), lambda s: (s, 0)),
        out_shape=jax.ShapeDtypeStruct((nseq * L, CONV_CH), BF16),
        scratch_shapes=[pltpu.VMEM((L + 2 * PAD, CONV_CH), F32), pltpu.VMEM((CONV_ROWS, CONV_CH), F32)],
        compiler_params=_cparams(("parallel",), 48),
        name="conv_L%d" % L,
    )(proj, proj, w32, b, lg, lb)


def _pool_body(L, G, u_ref, w_ref, sc_ref, o_ref, pad_ref):
    rows = 256
    zeros = jnp.zeros((PAD, 512), F32)
    pad_ref[0:PAD, :] = zeros
    pad_ref[PAD + L:2 * PAD + L, :] = zeros
    for s in range(G):
        pad_ref[PAD:PAD + L, :] = u_ref[s * L:(s + 1) * L, :]
        for c in range(L // rows):
            r0 = c * rows
            t = lax.broadcasted_iota(jnp.int32, (rows, 1), 0) + r0
            for g, w in enumerate(POOL_WINDOWS):
                cols = slice(g * POOL_GROUP_CH, (g + 1) * POOL_GROUP_CH)
                acc = jnp.zeros((rows, POOL_GROUP_CH), F32)
                for i in range(-(w // 2), w - w // 2):
                    acc = acc + pad_ref[PAD + r0 + i:PAD + r0 + i + rows, cols]
                lo = jnp.maximum(t - w // 2, 0)
                hi = jnp.minimum(t + (w - w // 2), L)
                cnt = (hi - lo).astype(F32)
                pooled = acc / cnt - pad_ref[PAD + r0:PAD + r0 + rows, cols]
                mixed = jnp.dot(pooled.astype(BF16), w_ref[g], preferred_element_type=F32)
                o_ref[s * L + r0:s * L + r0 + rows, cols] = (mixed * sc_ref[:, cols]).astype(BF16)


def _pool(proj, pw_bf, pscale, L, nseq, row0, G):
    return pl.pallas_call(
        functools.partial(_pool_body, L, G),
        grid=(nseq // G,),
        in_specs=[pl.BlockSpec((G * L, 512), lambda s: (row0 + s, COL_POOL // 512)),
                  pl.BlockSpec((4, POOL_GROUP_CH, POOL_GROUP_CH), lambda s: (0, 0, 0)),
                  pl.BlockSpec((1, 512), lambda s: (0, 0))],
        out_specs=pl.BlockSpec((G * L, 512), lambda s: (s, 0)),
        out_shape=jax.ShapeDtypeStruct((nseq * L, 512), BF16),
        scratch_shapes=[pltpu.VMEM((L + 2 * PAD, 512), F32)],
        compiler_params=_cparams(("parallel",), 48),
        name="pool_L%d" % L,
    )(proj, pw_bf, pscale)


def _channel_dft(u, wc_ref, z_ref, r0, L):
    rows = u.shape[0]
    for h in range(FOURIER_HEADS):
        cols = slice(h * FOURIER_HEAD_CH, (h + 1) * FOURIER_HEAD_CH)
        a = jnp.dot(u[:, cols].astype(BF16), wc_ref[...], preferred_element_type=F32)
        z_ref[r0:r0 + rows, cols] = a[:, :FOURIER_HEAD_CH].astype(BF16)
        z_ref[L + r0:L + r0 + rows, cols] = a[:, FOURIER_HEAD_CH:].astype(BF16)


def _four_body(L, G, u_ref, wc_ref, wl_ref, o_ref, z_ref):
    scale = (L * FOURIER_HEAD_CH) ** -0.5
    if wl_ref.shape[0] == L:
        for s in range(G):
            rows = slice(s * L, (s + 1) * L)
            _channel_dft(u_ref[rows, :], wc_ref, z_ref, 0, L)
            o_ref[rows, :] = (jnp.dot(wl_ref[...], z_ref[...], preferred_element_type=F32) * scale).astype(BF16)
    else:
        s = pl.program_id(1)

        @pl.when(pl.program_id(0) == 0)
        def _():
            rows = 512
            for c in range(L // rows):
                _channel_dft(u_ref[c * rows:(c + 1) * rows, :], wc_ref, z_ref.at[s], c * rows, L)

        o_ref[...] = (jnp.dot(wl_ref[...], z_ref[s], preferred_element_type=F32) * scale).astype(BF16)


def _four(pf, wc_bf, wl_bf, L, nseq, row0, G):
    tr = min(L, 512)
    wc_spec = pl.BlockSpec((FOURIER_HEAD_CH, 2 * FOURIER_HEAD_CH), lambda a, b: (0, 0))
    if tr == L:
        grid = (nseq // G, 1)
        in_specs = [pl.BlockSpec((G * L, 512), lambda s, r: (row0 + s, 0)), wc_spec,
                    pl.BlockSpec((L, 2 * L), lambda s, r: (0, 0))]
        out_spec = pl.BlockSpec((G * L, 512), lambda s, r: (s, 0))
        z_shape = (2 * L, 512)
    else:
        nblk = L // tr
        grid = (nblk, nseq)
        in_specs = [pl.BlockSpec((L, 512), lambda r, s: (row0 + jnp.where(r == 0, s, nseq - 1), 0)), wc_spec,
                    pl.BlockSpec((tr, 2 * L), lambda r, s: (r, 0))]
        out_spec = pl.BlockSpec((tr, 512), lambda r, s: (s * nblk + r, 0))
        z_shape = (nseq, 2 * L, 512)
    return pl.pallas_call(
        functools.partial(_four_body, L, G),
        grid=grid,
        in_specs=in_specs,
        out_specs=out_spec,
        out_shape=jax.ShapeDtypeStruct((nseq * L, 512), BF16),
        scratch_shapes=[pltpu.VMEM(z_shape, BF16)],
        compiler_params=_cparams(("arbitrary", "arbitrary"), 48),
        name="four_L%d" % L,
    )(pf, wc_bf, wl_bf)


def _route(logits):
    lane = lax.broadcasted_iota(jnp.int32, logits.shape, 1).astype(F32)
    neg = jnp.float32(-jnp.inf)
    big = jnp.float32(1 << 20)
    is_g = (lane >= N_EXPERTS) & (lane < N_EXPERTS + N_EXPERT_GROUPS)
    gl = jnp.where(is_g, logits, neg)
    gmax = jnp.max(gl, axis=-1, keepdims=True)
    g_idx = jnp.min(jnp.where(gl == gmax, lane, big), axis=-1, keepdims=True) - N_EXPERTS
    denom = jnp.sum(jnp.where(is_g, jnp.exp(gl - gmax), 0.0), axis=-1, keepdims=True)
    g_w = 1.0 / denom
    in_grp = (lane >= g_idx * EXPERTS_PER_GROUP) & (lane < (g_idx + 1) * EXPERTS_PER_GROUP)
    el = jnp.where(in_grp, logits, neg)
    v1 = jnp.max(el, axis=-1, keepdims=True)
    i1 = jnp.min(jnp.where(el == v1, lane, big), axis=-1, keepdims=True)
    el2 = jnp.where(lane == i1, neg, el)
    v2 = jnp.max(el2, axis=-1, keepdims=True)
    i2 = jnp.min(jnp.where(el2 == v2, lane, big), axis=-1, keepdims=True)
    e2 = jnp.exp(v2 - v1)
    p1 = 1.0 / (1.0 + e2)
    p2 = e2 / (1.0 + e2)
    gates = g_w * jnp.where(lane == i1, p1, jnp.where(lane == i2, p2, 0.0))
    return gates + jnp.where(lane == g_idx + N_EXPERTS, 1.0, 0.0)


def _earlier_matrix(rows):
    row = lax.broadcasted_iota(jnp.int32, (rows, rows), 0)
    col = lax.broadcasted_iota(jnp.int32, (rows, rows), 1)
    return jnp.where(col < row, 1.0, 0.0).astype(BF16)


def _bucket_ranks(route, carry_ref, earlier):
    lane = lax.broadcasted_iota(jnp.int32, route.shape, 1)
    is_g = (lane >= N_EXPERTS) & (lane < N_EXPERTS + N_EXPERT_GROUPS)
    onehot = jnp.where(is_g, route, 0.0)
    before =jnp.dot(earlier, onehot.astype(BF16), preferred_element_type=F32) + carry_ref[...]
    rank = jnp.sum(onehot * before, axis=-1, keepdims=True)
    grp = jnp.sum(onehot * (lane - N_EXPERTS).astype(F32), axis=-1, keepdims=True)
    packed = jnp.where(lane == 0, rank, jnp.where(lane == 1, grp, 0.0))
    carry_ref[...] += jnp.sum(onehot, axis=0, keepdims=True)
    return packed.T[0:SUBLANES, :].astype(jnp.int32)


def _pair_specs(width):
    n = T_CTX // TM
    return [pl.BlockSpec((TM, width), lambda i: (jnp.minimum(i, n - 1), 0)),
            pl.BlockSpec((TM, width), lambda i: (jnp.maximum(i - n, 0), 0))]


def _outproj_body(nx, *refs):
    mixers = refs[nx:nx + 8]
    (w_ref, g1_ref, gain_ref, sc_ref, sh_ref, wr_ref, br_ref,
     xo_ref, hg_ref, route_ref, rk_ref, cnt_ref, wb_ref, carry_ref) = refs[nx + 8:]

    @pl.when(pl.program_id(0) == 0)
    def _():
        wb_ref[...] = w_ref[...].astype(BF16)
        carry_ref[...] = jnp.zeros_like(carry_ref)

    def residual(c):
        rows = slice(c * OUT_ROWS, (c + 1) * OUT_ROWS)
        mix = None
        for k in range(4):
            part = jnp.dot(_load_x(mixers[2 * k:2 * k + 2], rows), wb_ref[512 * k:512 * (k + 1), :],
                           preferred_element_type=F32)
            mix = part if mix is None else mix + part
        return _load_x(refs[:nx], rows) + g1_ref[...] * mix

    nchunk = TM // OUT_ROWS
    earlier = _earlier_matrix(OUT_ROWS)
    x_next = residual(0)
    for c in range(nchunk):
        rows = slice(c * OUT_ROWS, (c + 1) * OUT_ROWS)
        x = x_next
        x_next = residual(c + 1) if c + 1 < nchunk else None
        xo_ref[rows, :] = x
        ms = jnp.mean(x * x, axis=-1, keepdims=True)
        h = x * lax.rsqrt(ms + EPS) * gain_ref[...]
        h = h * (1.0 + sc_ref[...]) + sh_ref[...]
        h_hi = h.astype(BF16)
        h_lo = (h - h_hi.astype(F32)).astype(BF16)
        t = (jnp.dot(h_hi, wr_ref[...], preferred_element_type=F32)
             + jnp.dot(h_lo, wr_ref[...], preferred_element_type=F32))
        logits = t + pltpu.roll(t, LANES - ROUTER_LO_LANE, axis=1) + br_ref[...]
        route = _route(logits)
        route_ref[rows, :] = route
        rk_ref[:, rows] = _bucket_ranks(route, carry_ref, earlier)
        rec0 = c * OUT_ROWS * REC
        for s in range(H_ROWS):
            hg_ref[pl.ds(rec0 + s, OUT_ROWS, stride=REC), :] = h[:, s * LANES:(s + 1) * LANES]
    cnt_ref[...] = carry_ref[...]


def _outproj(mixers, xs, w_out, mod4, gain, wr, br, layer):
    tile = lambda w: pl.BlockSpec((TM, w), lambda i: (i, 0))
    const = lambda r, c: pl.BlockSpec((r, c), lambda i: (0, 0))
    mix_specs = []
    for _ in range(4):
        mix_specs += _pair_specs(512)
    return pl.pallas_call(
        functools.partial(_outproj_body, len(xs)),
        grid=(T_ALL // TM,),
        in_specs=_x_specs(len(xs)) + mix_specs + [_resident((2048, D), layer),
                                                  _mod_spec(2, TM), const(1, D), _mod_spec(4, TM), _mod_spec(3, TM),
                                                  const(D, LANES), const(1, LANES)],
        out_specs=[tile(D), pl.BlockSpec((TM * REC, LANES), lambda i: (i, 0)), tile(LANES),
                   pl.BlockSpec((SUBLANES, TM), lambda i: (i, 0)), const(1, LANES)],
        out_shape=[jax.ShapeDtypeStruct((T_ALL, D), F32),
                   jax.ShapeDtypeStruct((T_ALL * REC, LANES), F32),
                   jax.ShapeDtypeStruct((T_ALL, LANES), F32),
                   jax.ShapeDtypeStruct((T_ALL // TM * SUBLANES, TM), jnp.int32),
                   jax.ShapeDtypeStruct((1, LANES), F32)],
        scratch_shapes=[pltpu.VMEM((2048, D), BF16), pltpu.VMEM((1, LANES), F32)],
        compiler_params=_cparams(("arbitrary",), 56),
        name="out_proj",
    )(*xs, *[a for pair in mixers for a in pair], w_out, mod4, gain, mod4, mod4, wr, br)


def _wait_rows(copy, n):
    def body(t, c):
        copy.wait()
        return c
    lax.fori_loop(0, n, body, 0, unroll=8)


def _record(ref, idx, rows):
    start = idx * rows if isinstance(idx, int) else pl.multiple_of(idx * rows, rows)
    return ref.at[pl.ds(start, rows)]


def _dispatch_body(slot_sm, fstart_sm, flen_sm, hg_ref, route_ref, xs_hbm, gs_ref, zrec_ref, sem):
    i = pl.program_id(0)
    base = i * DT

    @pl.when(i == 0)
    def _():
        gs_ref[...] = jnp.zeros_like(gs_ref)
        zrec_ref[...] = jnp.zeros_like(zrec_ref)
        for g in range(N_EXPERT_GROUPS):
            def fill(r, c):
                pltpu.make_async_copy(zrec_ref, _record(xs_hbm, fstart_sm[g] + r, REC), sem.at[1]).start()
                return c
            lax.fori_loop(0, flen_sm[g], fill, 0)
        _wait_rows(pltpu.make_async_copy(zrec_ref, _record(xs_hbm, 0, REC), sem.at[1]), NSLOT - T_ALL)

    def issue(pair, c):
        for prio in range(2):
            t = 2 * pair + prio
            slot = slot_sm[base + t]
            pltpu.async_copy(_record(hg_ref, t, REC), _record(xs_hbm, slot, REC), sem.at[0], priority=prio)
            gs_ref[pl.ds(slot, 1), :] = route_ref[pl.ds(t, 1), :]
        return c
    lax.fori_loop(0, DT // 2, issue, 0, unroll=4)
    _wait_rows(pltpu.make_async_copy(_record(hg_ref, 0, REC), _record(xs_hbm, 0, REC), sem.at[0]), DT)


def _dispatch(slot, fstart, flen, hg, route):
    return pl.pallas_call(
        _dispatch_body,
        grid_spec=pltpu.PrefetchScalarGridSpec(
            num_scalar_prefetch=3,
            grid=(T_ALL // DT,),
            in_specs=[pl.BlockSpec((DT * REC, LANES), lambda i, s, f0, f1: (i, 0)),
                      pl.BlockSpec((DT, LANES), lambda i, s, f0, f1: (i, 0))],
            out_specs=[pl.BlockSpec(memory_space=pl.ANY),
                       pl.BlockSpec((NSLOT, LANES), lambda i, s, f0, f1: (0, 0), pipeline_mode=pl.Buffered(1))],
            scratch_shapes=[pltpu.VMEM((REC, LANES), F32), pltpu.SemaphoreType.DMA((2,))]),
        out_shape=[jax.ShapeDtypeStruct((NSLOT * REC, LANES), F32),
                   jax.ShapeDtypeStruct((NSLOT, LANES), F32)],
        compiler_params=_cparams(("arbitrary",), 48),
        name="moe_dispatch",
    )(slot, fstart, flen, hg, route)


def _ffn_body(tgrp_sm, xs_ref, gs_ref, wg32_ref, wu32_ref, wd32_ref, ys_ref, h_ref, wg_ref, wu_ref, wd_ref):
    j = pl.program_id(0)
    grp = tgrp_sm[j]

    @pl.when(jnp.logical_or(j == 0, grp != tgrp_sm[jnp.maximum(j - 1, 0)]))
    def _():
        for e in range(EXPERTS_PER_GROUP):
            cols = slice(e * EXPERT_FF, (e + 1) * EXPERT_FF)
            wg_ref[:, cols] = wg32_ref[e].astype(BF16)
            wu_ref[:, cols] = wu32_ref[e].astype(BF16)
            wd_ref[cols, :] = wd32_ref[e].astype(BF16)

    for s in range(H_ROWS):
        h_ref[:, s * LANES:(s + 1) * LANES] = xs_ref[pl.ds(s, TS, stride=REC), :].astype(BF16)
    gates = gs_ref[...]
    h = h_ref[...]
    a = jnp.dot(h, wg_ref[...], preferred_element_type=F32)
    b = jnp.dot(h, wu_ref[...], preferred_element_type=F32)
    hid = _silu(a) * b
    lane = lax.broadcasted_iota(jnp.int32, gates.shape, 1)
    parts = []
    for e in range(EXPERTS_PER_GROUP):
        ge = jnp.sum(jnp.where(lane == grp * EXPERTS_PER_GROUP + e, gates, 0.0), axis=-1, keepdims=True)
        parts.append((hid[:, e * EXPERT_FF:(e + 1) * EXPERT_FF] * ge).astype(BF16))
    y = jnp.dot(jnp.concatenate(parts, axis=1), wd_ref[...], preferred_element_type=F32)
    for s in range(H_ROWS):
        ys_ref[pl.ds(s, TS, stride=H_ROWS), :] = y[:, s * LANES:(s + 1) * LANES]


def _ffn(tile_grp, xs, gs, w_gate, w_up, w_down, layer):
    ffw = EXPERTS_PER_GROUP * EXPERT_FF
    experts = lambda r, c: pl.BlockSpec((None, EXPERTS_PER_GROUP, r, c), lambda j, tg: (layer, tg[j], 0, 0))
    return pl.pallas_call(
        _ffn_body,
        grid_spec=pltpu.PrefetchScalarGridSpec(
            num_scalar_prefetch=1,
            grid=(NT,),
            in_specs=[pl.BlockSpec((TS * REC, LANES), lambda j, tg: (j, 0)),
                      pl.BlockSpec((TS, LANES), lambda j, tg: (j, 0)),
                      experts(D, EXPERT_FF), experts(D, EXPERT_FF), experts(EXPERT_FF, D)],
            out_specs=pl.BlockSpec((TS * H_ROWS, LANES), lambda j, tg: (j, 0)),
            scratch_shapes=[pltpu.VMEM((TS, D), BF16), pltpu.VMEM((D, ffw), BF16),
                            pltpu.VMEM((D, ffw), BF16), pltpu.VMEM((ffw, D), BF16)]),
        out_shape=jax.ShapeDtypeStruct((NSLOT * H_ROWS, LANES), F32),
        compiler_params=_cparams(("arbitrary",), 56),
        name="moe_ffn",
    )(tile_grp, xs, gs, w_gate, w_up, w_down)


def _combine_body(final, slot_sm, ys_hbm, x_ref, g2_ref, fg_ref, *rest):
    outs, buf, sem = rest[:-2], rest[-2], rest[-1]
    i = pl.program_id(0)
    n = pl.num_programs(0)

    def issue(step, b):
        def body(pair, c):
            for prio in range(2):
                t = 2 * pair + prio
                pltpu.async_copy(_record(ys_hbm, slot_sm[step * TM + t], H_ROWS),
                                 _record(buf.at[b], t, H_ROWS), sem.at[b], priority=prio)
            return c
        lax.fori_loop(0, TM // 2, body, 0, unroll=4)

    @pl.when(i == 0)
    def _():
        issue(0, 0)

    @pl.when(i + 1 < n)
    def _():
        issue(i + 1, (i + 1) % 2)

    b = i % 2
    _wait_rows(pltpu.make_async_copy(_record(ys_hbm, 0, H_ROWS), _record(buf.at[b], 0, H_ROWS), sem.at[b]), TM)
    y = jnp.concatenate([buf[b, pl.ds(s, TM, stride=H_ROWS), :] for s in range(H_ROWS)], axis=1)
    x = x_ref[...] + g2_ref[...] * y
    if not final:
        outs[0][...] = x
    else:
        ms = jnp.mean(x * x, axis=-1, keepdims=True)
        yn = x * lax.rsqrt(ms + EPS) * fg_ref[...]
        is_ctx = i < T_CTX // TM

        @pl.when(is_ctx)
        def _():
            outs[0][...] = yn

        @pl.when(jnp.logical_not(is_ctx))
        def _():
            outs[1][...] = yn


def _combine(slot, ys, x, mod4, final_gain, final):
    if final:
        nc = T_CTX // TM
        out_specs = [pl.BlockSpec((TM, D), lambda i, s: (jnp.minimum(i, nc - 1), 0)),
                     pl.BlockSpec((TM, D), lambda i, s: (jnp.maximum(i - nc, 0), 0))]
        out_shape = [jax.ShapeDtypeStruct((T_CTX, D), F32), jax.ShapeDtypeStruct((T_LAT, D), F32)]
    else:
        out_specs = pl.BlockSpec((TM, D), lambda i, s: (i, 0))
        out_shape = jax.ShapeDtypeStruct((T_ALL, D), F32)
    return pl.pallas_call(
        functools.partial(_combine_body, final),
        grid_spec=pltpu.PrefetchScalarGridSpec(
            num_scalar_prefetch=1,
            grid=(T_ALL // TM,),
            in_specs=[pl.BlockSpec(memory_space=pl.ANY),
                      pl.BlockSpec((TM, D), lambda i, s: (i, 0)),
                      pl.BlockSpec((None, None, 1, D), lambda i, s: (_mod_row(i, TM), 5, 0, 0)),
                      pl.BlockSpec((1, D), lambda i, s: (0, 0))],
            out_specs=out_specs,
            scratch_shapes=[pltpu.VMEM((2, TM * H_ROWS, LANES), F32), pltpu.SemaphoreType.DMA((2,))]),
        out_shape=out_shape,
        compiler_params=_cparams(("arbitrary",), 48),
        name="moe_combine",
    )(slot, ys, x, mod4, final_gain)


def _moe_plan(rk, cnt):
    rk = rk.reshape(T_ALL // TM, SUBLANES, TM)
    rank = rk[:, 0, :].reshape(T_ALL)
    grp = rk[:, 1, :].reshape(T_ALL)
    cnt = cnt[0, N_EXPERTS:N_EXPERTS + N_EXPERT_GROUPS].astype(jnp.int32)
    padded = (cnt + TS - 1) // TS * TS
    off = jnp.cumsum(padded) - padded
    padded = padded.at[N_EXPERT_GROUPS - 1].set(NSLOT - off[N_EXPERT_GROUPS - 1])
    slot = rank
    for g in range(1, N_EXPERT_GROUPS):
        slot = slot + jnp.where(grp == g, off[g], 0)
    ends = (off + padded)[:N_EXPERT_GROUPS - 1]
    tile_grp = jnp.sum((jnp.arange(NT, dtype=jnp.int32)[:, None] * TS >= ends[None, :]).astype(jnp.int32), axis=1)
    return slot, tile_grp, off + cnt, padded - cnt


def _rope_tables():
    t = np.arange(DEC_SEQ)
    pos = np.stack([t // GRID_W, t % GRID_W], axis=1).astype(np.float64)
    inv = ROPE_THETA ** (-np.arange(16, dtype=np.float64) / 16.0)
    d = np.arange(LANES) % HEAD_DIM
    pair = d // 2
    ang = pos[:, (pair >= 16).astype(np.int64)] * inv[pair % 16][None, :]
    sign = np.where(d % 2 == 0, -1.0, 1.0)
    return np.cos(ang).astype(np.float32), (np.sin(ang) * sign[None, :]).astype(np.float32)


def _dft_pos(L):
    j = np.arange(L)
    ang = 2.0 * np.pi * ((j[:, None] * j[None, :]) % L) / L
    return np.concatenate([np.cos(ang), -np.sin(ang)], axis=1).astype(np.float32)


def _dft_ch():
    c = np.arange(FOURIER_HEAD_CH)
    ang = 2.0 * np.pi * ((c[:, None] * c[None, :]) % FOURIER_HEAD_CH) / FOURIER_HEAD_CH
    return np.concatenate([np.cos(ang), np.sin(ang)], axis=1).astype(np.float32)


def kernel(x_prompt, x_sample, cache_k, cache_v, c, c_ctx, w_ada, b_ada, norm_mix, norm_ffn, w_in, w_out, q_norm, k_norm, conv_w, conv_b, conv_ln_g, conv_ln_b, pool_w, pool_scale, router_group_w, router_group_b, router_expert_w, router_expert_b, moe_w_gate, moe_w_up, moe_w_down, final_norm):
    xs = (x_prompt.reshape(T_CTX, D), x_sample.reshape(T_LAT, D))
    cond8 =jnp.concatenate([c_ctx[None, :], c, jnp.zeros((3, D), F32)], axis=0)
    mod = _ada(cond8, w_ada, b_ada)

    pw_bf = pool_w.astype(BF16)
    n_route = N_EXPERTS + N_EXPERT_GROUPS
    wr = jnp.concatenate([router_expert_w, router_group_w], axis=-1)
    wr_hi = wr.astype(BF16)
    wr_lo = (wr - wr_hi.astype(F32)).astype(BF16)
    w_router = jnp.concatenate([wr_hi, jnp.zeros((DEPTH, D, ROUTER_LO_LANE - n_route), BF16), wr_lo,
                                jnp.zeros((DEPTH, D, LANES - ROUTER_LO_LANE - n_route), BF16)], axis=-1)
    b_router = jnp.concatenate([router_expert_b, router_group_b,
                                jnp.zeros((DEPTH, LANES - N_EXPERTS - N_EXPERT_GROUPS), F32)], axis=-1)
    conv_w32 = jnp.concatenate([conv_w, jnp.zeros((DEPTH, 1, CONV_CH), F32)], axis=1)

    cos_np, sin_np = _rope_tables()
    cos, sin = jnp.asarray(cos_np), jnp.asarray(sin_np)
    wc_bf = jnp.asarray(_dft_ch()).astype(BF16)
    wl_ctx = jnp.asarray(_dft_pos(SEQ)).astype(BF16)
    wl_lat = jnp.asarray(_dft_pos(DEC_SEQ)).astype(BF16)

    cache_k4 = cache_k.reshape(DEC_BATCH, DEPTH, PAST_LEN, KV_CH)
    cache_v4 = cache_v.reshape(DEC_BATCH, DEPTH, PAST_LEN, KV_CH)
    new_k = jnp.zeros((BATCH, DEPTH, SEQ, KV_CH), F32)
    new_v = jnp.zeros((BATCH, DEPTH, SEQ, KV_CH), F32)
    fgain = final_norm.reshape(1, D)

    for l in range(DEPTH):
        mod4 = mod[l].reshape(8, N_MOD, 1, D)
        proj, pf = _inproj(xs, norm_mix[l].reshape(1, D), mod4, w_in, l)
        qg = jnp.tile(q_norm[l], 2).reshape(1, LANES)
        kg = jnp.tile(k_norm[l], 2).reshape(1, LANES)
        attn_c, new_k, new_v = _ctx_attn(proj, qg, kg, new_k, new_v, l)
        attn_l = _lat_attn(proj, cache_k4, cache_v4, qg, kg, cos, sin, l)
        cb, lg, lb = conv_b[l].reshape(1, -1), conv_ln_g[l].reshape(1, -1), conv_ln_b[l].reshape(1, -1)
        ps = pool_scale[l].reshape(1, -1)
        lat0 = T_CTX // DEC_SEQ
        mixers = [(attn_c, attn_l),
                  (_conv(proj, conv_w32[l], cb, lg, lb, SEQ, BATCH, 0),
                   _conv(proj, conv_w32[l], cb, lg, lb, DEC_SEQ, DEC_BATCH, lat0)),
                  (_pool(proj, pw_bf[l], ps, SEQ, BATCH, 0, MIX_G),
                   _pool(proj, pw_bf[l], ps, DEC_SEQ, DEC_BATCH, lat0, 1)),
                  (_four(pf, wc_bf, wl_ctx, SEQ, BATCH, 0, MIX_G),
                   _four(pf, wc_bf, wl_lat, DEC_SEQ, DEC_BATCH, lat0, 1))]
        x, hg, route, rk, cnt = _outproj(mixers, xs, w_out, mod4, norm_ffn[l].reshape(1, D),
                                         w_router[l], b_router[l].reshape(1, LANES), l)
        slot, tile_grp, fstart, flen = _moe_plan(rk, cnt)
        xslots, gslots = _dispatch(slot, fstart, flen, hg, route)
        yslots = _ffn(tile_grp, xslots, gslots, moe_w_gate, moe_w_up, moe_w_down, l)
        out = _combine(slot, yslots, x, mod4, fgain, l == DEPTH - 1)
        xs = (out,)

    y_prompt = out[0].reshape(BATCH, SEQ, D)
    y_sample = out[1].reshape(DEC_BATCH, DEC_SEQ, D)
    new_k = new_k.reshape(BATCH, DEPTH, SEQ, 2, HEAD_DIM)
    new_v = new_v.reshape(BATCH, DEPTH, SEQ, 2, HEAD_DIM)
    return (y_prompt, y_sample, new_k, new_v)
```

```python
import functools

import numpy as np
import jax
import jax.numpy as jnp
from jax import lax
from jax.experimental import pallas as pl
from jax.experimental.pallas import tpu as pltpu

F32 = jnp.float32
BF16 = jnp.bfloat16

D = 1024
BATCH = 32
SEQ = 256
DEPTH = 2
DEC_BATCH = 4
DEC_SEQ = 2048
PAST_LEN = 256
GRID_W = 64
HEAD_DIM = 64
ATTN_CH = 512
KV_CH = 128
CONV_CH = 512
CONV_WIDTH = 31
POOL_WINDOWS = (2, 4, 8, 16)
POOL_GROUP_CH = 128
FOURIER_HEAD_CH = 128
FOURIER_HEADS = 4
N_EXPERT_GROUPS = 4
EXPERTS_PER_GROUP = 4
N_EXPERTS = 16
EXPERT_FF = 256
ROPE_THETA = 10000.0
EPS = 1e-6
N_MOD = 6

T_CTX = BATCH * SEQ
T_LAT = DEC_BATCH * DEC_SEQ
T_ALL = T_CTX + T_LAT
LANES = 128
SUBLANES = 8
PAD = 16
CONV_ROWS = 256
ROUTER_LO_LANE = 32
IN_COLS = 2816
COL_Q, COL_CA, COL_CG, COL_POOL, COL_K, COL_V, COL_FOUR = 0, 512, 1024, 1536, 2048, 2176, 2304
F32_COLS = COL_FOUR

TM = 512
OUT_ROWS = 256
DT = 2048
CTX_G = 2
MIX_G = 4
TS = 512
NT = T_ALL // TS + N_EXPERT_GROUPS
NSLOT = NT * TS
H_ROWS = D // LANES
REC = H_ROWS
TQ = 256
LK = PAST_LEN + DEC_SEQ


def _cparams(sem, vmem_mb):
    return pltpu.CompilerParams(dimension_semantics=sem, vmem_limit_bytes=vmem_mb * 1024 * 1024)


def _mod_row(i, tm):
    nctx = T_CTX // tm
    per = DEC_SEQ // tm
    return jnp.where(i < nctx, 0, 1 + (i - nctx) // per)


def _mod_spec(k, tm):
    return pl.BlockSpec((None, None, 1, D), lambda i: (_mod_row(i, tm), k, 0, 0))


def _silu(x):
    return x * jax.nn.sigmoid(x)


def _ada_body(c_ref, w_ref, b_ref, o_ref):
    s = _silu(c_ref[...])
    o_ref[...] = jnp.dot(s.astype(BF16), w_ref[...].astype(BF16), preferred_element_type=F32) + b_ref[...]


def _ada(cond8, w_ada, b_ada):
    tn = 1536
    n = N_MOD * D
    return pl.pallas_call(
        _ada_body,
        grid=(DEPTH, n // tn),
        in_specs=[pl.BlockSpec((8, D), lambda l, j: (0, 0)),
                  pl.BlockSpec((None, D, tn), lambda l, j: (l, 0, j)),
                  pl.BlockSpec((None, 1, tn), lambda l, j: (l, 0, j))],
        out_specs=pl.BlockSpec((None, 8, tn), lambda l, j: (l, 0, j)),
        out_shape=jax.ShapeDtypeStruct((DEPTH, 8, n), F32),
        compiler_params=_cparams(("arbitrary", "arbitrary"), 40),
        name="ada_mod",
    )(cond8, w_ada, b_ada.reshape(DEPTH, 1, n))


def _x_specs(nx):
    if nx == 1:
        return [pl.BlockSpec((TM, D), lambda i: (i, 0))]
    n = T_CTX // TM
    return [pl.BlockSpec((TM, D), lambda i: (jnp.minimum(i, n - 1), 0)),
            pl.BlockSpec((TM, D), lambda i: (jnp.maximum(i - n, 0), 0))]


def _load_x(x_refs, rows=slice(None)):
    if len(x_refs) == 1:
        return x_refs[0][rows, :]
    return jnp.where(pl.program_id(0) < T_CTX // TM, x_refs[0][rows, :], x_refs[1][rows, :])


def _resident(shape, layer):
    return pl.BlockSpec((None,) + shape, lambda *_: (layer,) + (0,) * len(shape), pipeline_mode=pl.Buffered(1))


def _inproj_body(nx, *refs):
    g_ref, sc_ref, sh_ref, w_ref, o_ref, of_ref, wb_ref = refs[nx:]

    @pl.when(pl.program_id(0) == 0)
    def _():
        kv0, kv1 = ATTN_CH, ATTN_CH + 2 * KV_CH
        wb_ref[:, 0:kv0] = w_ref[:, 0:kv0].astype(BF16)
        wb_ref[:, kv0:COL_K] = w_ref[:, kv1:kv1 + COL_K - kv0].astype(BF16)
        wb_ref[:, COL_K:COL_FOUR] = w_ref[:, kv0:kv1].astype(BF16)
        wb_ref[:, COL_FOUR:IN_COLS] = w_ref[:, COL_FOUR:IN_COLS].astype(BF16)

    x = _load_x(refs[:nx])
    ms = jnp.mean(x * x, axis=-1, keepdims=True)
    h = x * lax.rsqrt(ms + EPS) * g_ref[...]
    h = h * (1.0 + sc_ref[...]) + sh_ref[...]
    hb = h.astype(BF16)
    o_ref[...] = jnp.dot(hb, wb_ref[:, 0:F32_COLS], preferred_element_type=F32)
    of_ref[...] = jnp.dot(hb, wb_ref[:, F32_COLS:IN_COLS], preferred_element_type=F32).astype(BF16)


def _inproj(xs, gain, mod4, w_in, layer):
    return pl.pallas_call(
        functools.partial(_inproj_body, len(xs)),
        grid=(T_ALL // TM,),
        in_specs=_x_specs(len(xs)) + [pl.BlockSpec((1, D), lambda i: (0, 0)),
                                      _mod_spec(1, TM), _mod_spec(0, TM),
                                      _resident((D, IN_COLS), layer)],
        out_specs=[pl.BlockSpec((TM, F32_COLS), lambda i: (i, 0)),
                   pl.BlockSpec((TM, IN_COLS - F32_COLS), lambda i: (i, 0))],
        out_shape=[jax.ShapeDtypeStruct((T_ALL, F32_COLS), F32),
                   jax.ShapeDtypeStruct((T_ALL, IN_COLS - F32_COLS), BF16)],
        scratch_shapes=[pltpu.VMEM((D, IN_COLS), BF16)],
        compiler_params=_cparams(("arbitrary",), 56),
        name="in_proj",
    )(*xs, gain, mod4, mod4, w_in)


def _lane_lo():
    return lax.broadcasted_iota(jnp.int32, (1, LANES), 1) < HEAD_DIM


def _head_norm(x, gain):
    lo = _lane_lo()
    x2 = x * x
    s_lo = jnp.sum(jnp.where(lo, x2, 0.0), axis=-1, keepdims=True)
    s_hi = jnp.sum(jnp.where(lo, 0.0, x2), axis=-1, keepdims=True)
    r = jnp.where(lo, lax.rsqrt(s_lo * (1.0 / HEAD_DIM) + EPS), lax.rsqrt(s_hi * (1.0 / HEAD_DIM) + EPS))
    return x * r * gain


def _rope(x, cos, sin_signed):
    even = (lax.broadcasted_iota(jnp.int32, (1, LANES), 1) % 2) == 0
    swapped = jnp.where(even, pltpu.roll(x, LANES - 1, axis=1), pltpu.roll(x, 1, axis=1))
    return x * cos + swapped * sin_signed


def _split_heads(x, g):
    lo = _lane_lo()
    own = jnp.where(lo if g == 0 else jnp.logical_not(lo), x, 0.0)
    other = pltpu.roll(own, HEAD_DIM, axis=1)
    return (own, other) if g == 0 else (other, own)


Q_SCALE = HEAD_DIM ** -0.5 * 1.4426950408889634


def _ones_cols(rows, lk):
    r = lax.broadcasted_iota(jnp.int32, (rows, LANES), 0)
    lane = lax.broadcasted_iota(jnp.int32, (rows, LANES), 1)
    return jnp.where(((lane == 0) & (r < lk)) | ((lane == 1) & (r >= lk)), 1.0, 0.0)


def _scores(q_bf, k_cat):
    return lax.dot_general(q_bf, k_cat, (((1,), (1,)), ((), ())), preferred_element_type=F32)


def _attend_tile(q_bf, k_cat, v_cat, lk):
    return _softmax_pv(_scores(q_bf, k_cat), v_cat, lk)


def _softmax_pv(s, v_cat, lk):
    s_lo, s_hi = s[:, :lk], s[:, lk:]
    p_lo = jnp.exp2(s_lo - jnp.max(s_lo, axis=-1, keepdims=True))
    p_hi = jnp.exp2(s_hi - jnp.max(s_hi, axis=-1, keepdims=True))
    p = jnp.concatenate([p_lo, p_hi], axis=1).astype(BF16)
    o = jnp.dot(p, v_cat, preferred_element_type=F32)
    l_lo, l_hi = o[:, LANES:LANES + 1], o[:, LANES + 1:LANES + 2]
    return o[:, :LANES] * jnp.where(_lane_lo(), 1.0 / l_lo, 1.0 / l_hi)


def _ctx_attn_body(q_ref, k_ref, v_ref, qg_ref, kg_ref, kc_in, vc_in, o_ref, kc_ref, vc_ref):
    del kc_in, vc_in
    ones = _ones_cols(2 * SEQ, SEQ)
    kv = []
    for s in range(CTX_G):
        rows = slice(s * SEQ, (s + 1) * SEQ)
        kn = _head_norm(k_ref[rows, :], kg_ref[...])
        v = v_ref[rows, :]
        kc_ref[s] = kn
        vc_ref[s] = v
        for g in range(2):
            k_lo, k_hi = _split_heads(kn, g)
            v_lo, v_hi = _split_heads(v, g)
            kv.append((jnp.concatenate([k_lo, k_hi], axis=0).astype(BF16),
                       jnp.concatenate([jnp.concatenate([v_lo, v_hi], axis=0), ones], axis=1).astype(BF16)))
    for t in range(ATTN_CH // LANES):
        for s in range(CTX_G):
            rows = slice(s * SEQ, (s + 1) * SEQ)
            k_cat, v_cat = kv[2 * s + t // 2]
            q = _head_norm(q_ref[rows, t * LANES:(t + 1) * LANES], qg_ref[...]) * Q_SCALE
            o = _attend_tile(q.astype(BF16), k_cat, v_cat, SEQ)
            o_ref[rows, t * LANES:(t + 1) * LANES] = o.astype(BF16)


def _ctx_attn(proj, qg, kg, kcache, vcache, layer):
    G = CTX_G
    return pl.pallas_call(
        _ctx_attn_body,
        grid=(BATCH // G,),
        in_specs=[pl.BlockSpec((G * SEQ, ATTN_CH), lambda b: (b, COL_Q // ATTN_CH)),
                  pl.BlockSpec((G * SEQ, KV_CH), lambda b: (b, COL_K // KV_CH)),
                  pl.BlockSpec((G * SEQ, KV_CH), lambda b: (b, COL_V // KV_CH)),
                  pl.BlockSpec((1, LANES), lambda b: (0, 0)),
                  pl.BlockSpec((1, LANES), lambda b: (0, 0)),
                  pl.BlockSpec(memory_space=pl.ANY),
                  pl.BlockSpec(memory_space=pl.ANY)],
        out_specs=[pl.BlockSpec((G * SEQ, ATTN_CH), lambda b: (b, 0)),
                   pl.BlockSpec((G, None, SEQ, KV_CH), lambda b: (b, layer, 0, 0)),
                   pl.BlockSpec((G, None, SEQ, KV_CH), lambda b: (b, layer, 0, 0))],
        out_shape=[jax.ShapeDtypeStruct((T_CTX, ATTN_CH), BF16),
                   jax.ShapeDtypeStruct((BATCH, DEPTH, SEQ, KV_CH), F32),
                   jax.ShapeDtypeStruct((BATCH, DEPTH, SEQ, KV_CH), F32)],
        input_output_aliases={5: 1, 6: 2},
        compiler_params=_cparams(("parallel",), 32),
        name="ctx_attn",
    )(proj, proj, proj, qg, kg, kcache, vcache)


def _lat_attn_body(q_ref, k_ref, v_ref, ck_ref, cv_ref, qg_ref, kg_ref, cosk_ref, sink_ref, cosq_ref, sinq_ref,
                   o_ref, kcat_ref, vcat_ref):
    @pl.when(pl.program_id(1) == 0)
    def _():
        rows = 256
        for c in range(LK // rows):
            if c == 0:
                kn = ck_ref[...]
                v = cv_ref[...]
            else:
                r0 = (c - 1) * rows
                kn = _head_norm(k_ref[r0:r0 + rows, :], kg_ref[...])
                kn = _rope(kn, cosk_ref[r0:r0 + rows, :], sink_ref[r0:r0 + rows, :])
                v = v_ref[r0:r0 + rows, :]
            for g in range(2):
                k_lo, k_hi = _split_heads(kn, g)
                v_lo, v_hi = _split_heads(v, g)
                kcat_ref[g, c * rows:(c + 1) * rows, :] = k_lo.astype(BF16)
                kcat_ref[g, LK + c * rows:LK + (c + 1) * rows, :] = k_hi.astype(BF16)
                vcat_ref[g, c * rows:(c + 1) * rows, 0:LANES] = v_lo.astype(BF16)
                vcat_ref[g, LK + c * rows:LK + (c + 1) * rows, 0:LANES] = v_hi.astype(BF16)
                vcat_ref[g, c * rows:(c + 1) * rows, LANES:2 * LANES] = _ones_cols(rows, rows).astype(BF16)
                vcat_ref[g, LK + c * rows:LK + (c + 1) * rows, LANES:2 * LANES] = _ones_cols(rows, 0).astype(BF16)

    def scores(t):
        q = _head_norm(q_ref[:, t * LANES:(t + 1) * LANES], qg_ref[...])
        q = _rope(q, cosq_ref[...], sinq_ref[...]) * Q_SCALE
        return _scores(q.astype(BF16), kcat_ref[t // 2])

    nt = ATTN_CH // LANES
    s = scores(0)
    for t in range(nt):
        s_next = scores(t + 1) if t + 1 < nt else None
        o = _softmax_pv(s, vcat_ref[t // 2], LK)
        o_ref[:, t * LANES:(t + 1) * LANES] = o.astype(BF16)
        s = s_next


def _lat_attn(proj, cache_k4, cache_v4, qg, kg, cos, sin, layer):
    nqb = DEC_SEQ // TQ
    row0 = T_CTX // DEC_SEQ
    return pl.pallas_call(
        _lat_attn_body,
        grid=(DEC_BATCH, nqb),
        in_specs=[pl.BlockSpec((TQ, ATTN_CH), lambda b, i: (T_CTX // TQ + b * nqb + i, COL_Q // ATTN_CH)),
                  pl.BlockSpec((DEC_SEQ, KV_CH), lambda b, i: (row0 + b, COL_K // KV_CH)),
                  pl.BlockSpec((DEC_SEQ, KV_CH), lambda b, i: (row0 + b, COL_V // KV_CH)),
                  pl.BlockSpec((None, None, PAST_LEN, KV_CH), lambda b, i: (b, layer, 0, 0)),
                  pl.BlockSpec((None, None, PAST_LEN, KV_CH), lambda b, i: (b, layer, 0, 0)),
                  pl.BlockSpec((1, LANES), lambda b, i: (0, 0)),
                  pl.BlockSpec((1, LANES), lambda b, i: (0, 0)),
                  pl.BlockSpec((DEC_SEQ, LANES), lambda b, i: (0, 0)),
                  pl.BlockSpec((DEC_SEQ, LANES), lambda b, i: (0, 0)),
                  pl.BlockSpec((TQ, LANES), lambda b, i: (i, 0)),
                  pl.BlockSpec((TQ, LANES), lambda b, i: (i, 0))],
        out_specs=pl.BlockSpec((TQ, ATTN_CH), lambda b, i: (b * nqb + i, 0)),
        out_shape=jax.ShapeDtypeStruct((T_LAT, ATTN_CH), BF16),
        scratch_shapes=[pltpu.VMEM((2, 2 * LK, LANES), BF16), pltpu.VMEM((2, 2 * LK, 2 * LANES), BF16)],
        compiler_params=_cparams(("parallel", "arbitrary"), 56),
        name="lat_attn",
    )(proj, proj, proj, cache_k4, cache_v4, qg, kg, cos, sin, cos, sin)


def _conv_body(L, a_ref, g_ref, w_ref, b_ref, lg_ref, lb_ref, o_ref, pad_ref, y_ref):
    rows = CONV_ROWS
    zeros = jnp.zeros((PAD, CONV_CH), F32)
    pad_ref[0:PAD, :] = zeros
    pad_ref[PAD + L:2 * PAD + L, :] = zeros
    pad_ref[PAD:PAD + L, :] = a_ref[...] * jax.nn.sigmoid(g_ref[...])
    shift0 = PAD - CONV_WIDTH // 2

    def chunk(c, carry):
        base = pl.multiple_of(c * rows, rows)
        for lg in range(CONV_CH // LANES):
            lanes = slice(lg * LANES, (lg + 1) * LANES)
            y = jnp.zeros((rows, LANES), F32)
            for r in range(SUBLANES):
                z = None
                for q in range((CONV_WIDTH + shift0) // SUBLANES + 1):
                    j = SUBLANES * q + r - shift0
                    if 0 <= j < CONV_WIDTH:
                        term = w_ref[j:j + 1, lanes] * pad_ref[pl.ds(base + SUBLANES * q, rows + SUBLANES), lanes]
                        z = term if z is None else z + term
                y = y + z[r:r + rows, :]
            y_ref[:, lanes] = y + b_ref[:, lanes]
        half = rows // 2
        for p in range(2):
            y = y_ref[p * half:(p + 1) * half, :]
            mu = jnp.mean(y, axis=-1, keepdims=True)
            yc = y - mu
            var = jnp.mean(yc * yc, axis=-1, keepdims=True)
            yn = yc * lax.rsqrt(var + EPS) * lg_ref[...] + lb_ref[...]
            o_ref[pl.ds(base + p * half, half), :] = _silu(yn).astype(BF16)
        return carry

    lax.fori_loop(0, L // rows, chunk, 0)


def _conv(proj, w32, b, lg, lb, L, nseq, row0):
    vec = pl.BlockSpec((1, CONV_CH), lambda s: (0, 0))
    return pl.pallas_call(
        functools.partial(_conv_body, L),
        grid=(nseq,),
        in_specs=[pl.BlockSpec((L, CONV_CH), lambda s: (row0 + s, COL_CA // CONV_CH)),
                  pl.BlockSpec((L, CONV_CH), lambda s: (row0 + s, COL_CG // CONV_CH)),
                  pl.BlockSpec((32, CONV_CH), lambda s: (0, 0)), vec, vec, vec],
        out_specs=pl.BlockSpec((L, CONV_CH), lambda s: (s, 0)),
        out_shape=jax.ShapeDtypeStruct((nseq * L, CONV_CH), BF16),
        scratch_shapes=[pltpu.VMEM((L + 2 * PAD, CONV_CH), F32), pltpu.VMEM((CONV_ROWS, CONV_CH), F32)],
        compiler_params=_cparams(("parallel",), 48),
        name="conv_L%d" % L,
    )(proj, proj, w32, b, lg, lb)


def _pool_body(L, G, u_ref, w_ref, sc_ref, o_ref, pad_ref):
    rows = 256
    zeros = jnp.zeros((PAD, 512), F32)
    pad_ref[0:PAD, :] = zeros
    pad_ref[PAD + L:2 * PAD + L, :] = zeros
    for s in range(G):
        pad_ref[PAD:PAD + L, :] = u_ref[s * L:(s + 1) * L, :]
        for c in range(L // rows):
            r0 = c * rows
            t = lax.broadcasted_iota(jnp.int32, (rows, 1), 0) + r0
            for g, w in enumerate(POOL_WINDOWS):
                cols = slice(g * POOL_GROUP_CH, (g + 1) * POOL_GROUP_CH)
                acc = jnp.zeros((rows, POOL_GROUP_CH), F32)
                for i in range(-(w // 2), w - w // 2):
                    acc = acc + pad_ref[PAD + r0 + i:PAD + r0 + i + rows, cols]
                lo = jnp.maximum(t - w // 2, 0)
                hi = jnp.minimum(t + (w - w // 2), L)
                cnt = (hi - lo).astype(F32)
                pooled = acc / cnt - pad_ref[PAD + r0:PAD + r0 + rows, cols]
                mixed = jnp.dot(pooled.astype(BF16), w_ref[g], preferred_element_type=F32)
                o_ref[s * L + r0:s * L + r0 + rows, cols] = (mixed * sc_ref[:, cols]).astype(BF16)


def _pool(proj, pw_bf, pscale, L, nseq, row0, G):
    return pl.pallas_call(
        functools.partial(_pool_body, L, G),
        grid=(nseq // G,),
        in_specs=[pl.BlockSpec((G * L, 512), lambda s: (row0 + s, COL_POOL // 512)),
                  pl.BlockSpec((4, POOL_GROUP_CH, POOL_GROUP_CH), lambda s: (0, 0, 0)),
                  pl.BlockSpec((1, 512), lambda s: (0, 0))],
        out_specs=pl.BlockSpec((G * L, 512), lambda s: (s, 0)),
        out_shape=jax.ShapeDtypeStruct((nseq * L, 512), BF16),
        scratch_shapes=[pltpu.VMEM((L + 2 * PAD, 512), F32)],
        compiler_params=_cparams(("parallel",), 48),
        name="pool_L%d" % L,
    )(proj, pw_bf, pscale)


def _channel_dft(u, wc_ref, z_ref, r0, L):
    rows = u.shape[0]
    for h in range(FOURIER_HEADS):
        cols = slice(h * FOURIER_HEAD_CH, (h + 1) * FOURIER_HEAD_CH)
        a = jnp.dot(u[:, cols].astype(BF16), wc_ref[...], preferred_element_type=F32)
        z_ref[r0:r0 + rows, cols] = a[:, :FOURIER_HEAD_CH].astype(BF16)
        z_ref[L + r0:L + r0 + rows, cols] = a[:, FOURIER_HEAD_CH:].astype(BF16)


def _four_body(L, G, u_ref, wc_ref, wl_ref, o_ref, z_ref):
    scale = (L * FOURIER_HEAD_CH) ** -0.5
    if wl_ref.shape[0] == L:
        for s in range(G):
            rows = slice(s * L, (s + 1) * L)
            _channel_dft(u_ref[rows, :], wc_ref, z_ref, 0, L)
            o_ref[rows, :] = (jnp.dot(wl_ref[...], z_ref[...], preferred_element_type=F32) * scale).astype(BF16)
    else:
        s = pl.program_id(1)

        @pl.when(pl.program_id(0) == 0)
        def _():
            rows = 512
            for c in range(L // rows):
                _channel_dft(u_ref[c * rows:(c + 1) * rows, :], wc_ref, z_ref.at[s], c * rows, L)

        o_ref[...] = (jnp.dot(wl_ref[...], z_ref[s], preferred_element_type=F32) * scale).astype(BF16)


def _four(pf, wc_bf, wl_bf, L, nseq, row0, G):
    tr = min(L, 512)
    wc_spec = pl.BlockSpec((FOURIER_HEAD_CH, 2 * FOURIER_HEAD_CH), lambda a, b: (0, 0))
    if tr == L:
        grid = (nseq // G, 1)
        in_specs = [pl.BlockSpec((G * L, 512), lambda s, r: (row0 + s, 0)), wc_spec,
                    pl.BlockSpec((L, 2 * L), lambda s, r: (0, 0))]
        out_spec = pl.BlockSpec((G * L, 512), lambda s, r: (s, 0))
        z_shape = (2 * L, 512)
    else:
        nblk = L // tr
        grid = (nblk, nseq)
        in_specs = [pl.BlockSpec((L, 512), lambda r, s: (row0 + jnp.where(r == 0, s, nseq - 1), 0)), wc_spec,
                    pl.BlockSpec((tr, 2 * L), lambda r, s: (r, 0))]
        out_spec = pl.BlockSpec((tr, 512), lambda r, s: (s * nblk + r, 0))
        z_shape = (nseq, 2 * L, 512)
    return pl.pallas_call(
        functools.partial(_four_body, L, G),
        grid=grid,
        in_specs=in_specs,
        out_specs=out_spec,
        out_shape=jax.ShapeDtypeStruct((nseq * L, 512), BF16),
        scratch_shapes=[pltpu.VMEM(z_shape, BF16)],
        compiler_params=_cparams(("arbitrary", "arbitrary"), 48),
        name="four_L%d" % L,
    )(pf, wc_bf, wl_bf)


def _route(logits):
    lane = lax.broadcasted_iota(jnp.int32, logits.shape, 1).astype(F32)
    neg = jnp.float32(-jnp.inf)
    big = jnp.float32(1 << 20)
    is_g = (lane >= N_EXPERTS) & (lane < N_EXPERTS + N_EXPERT_GROUPS)
    gl = jnp.where(is_g, logits, neg)
    gmax = jnp.max(gl, axis=-1, keepdims=True)
    g_idx = jnp.min(jnp.where(gl == gmax, lane, big), axis=-1, keepdims=True) - N_EXPERTS
    denom = jnp.sum(jnp.where(is_g, jnp.exp(gl - gmax), 0.0), axis=-1, keepdims=True)
    g_w = 1.0 / denom
    in_grp = (lane >= g_idx * EXPERTS_PER_GROUP) & (lane < (g_idx + 1) * EXPERTS_PER_GROUP)
    el = jnp.where(in_grp, logits, neg)
    v1 = jnp.max(el, axis=-1, keepdims=True)
    i1 = jnp.min(jnp.where(el == v1, lane, big), axis=-1, keepdims=True)
    el2 = jnp.where(lane == i1, neg, el)
    v2 = jnp.max(el2, axis=-1, keepdims=True)
    i2 = jnp.min(jnp.where(el2 == v2, lane, big), axis=-1, keepdims=True)
    e2 = jnp.exp(v2 - v1)
    p1 = 1.0 / (1.0 + e2)
    p2 = e2 / (1.0 + e2)
    gates = g_w * jnp.where(lane == i1, p1, jnp.where(lane == i2, p2, 0.0))
    return gates + jnp.where(lane == g_idx + N_EXPERTS, 1.0, 0.0)


def _earlier_matrix(rows):
    row = lax.broadcasted_iota(jnp.int32, (rows, rows), 0)
    col = lax.broadcasted_iota(jnp.int32, (rows, rows), 1)
    return jnp.where(col < row, 1.0, 0.0).astype(BF16)


def _bucket_ranks(route, carry_ref, earlier):
    lane = lax.broadcasted_iota(jnp.int32, route.shape, 1)
    is_g = (lane >= N_EXPERTS) & (lane < N_EXPERTS + N_EXPERT_GROUPS)
    onehot = jnp.where(is_g, route, 0.0)
    before =jnp.dot(earlier, onehot.astype(BF16), preferred_element_type=F32) + carry_ref[...]
    rank = jnp.sum(onehot * before, axis=-1, keepdims=True)
    grp = jnp.sum(onehot * (lane - N_EXPERTS).astype(F32), axis=-1, keepdims=True)
    packed = jnp.where(lane == 0, rank, jnp.where(lane == 1, grp, 0.0))
    carry_ref[...] += jnp.sum(onehot, axis=0, keepdims=True)
    return packed.T[0:SUBLANES, :].astype(jnp.int32)


def _pair_specs(width):
    n = T_CTX // TM
    return [pl.BlockSpec((TM, width), lambda i: (jnp.minimum(i, n - 1), 0)),
            pl.BlockSpec((TM, width), lambda i: (jnp.maximum(i - n, 0), 0))]


def _outproj_body(nx, *refs):
    mixers = refs[nx:nx + 8]
    (w_ref, g1_ref, gain_ref, sc_ref, sh_ref, wr_ref, br_ref,
     xo_ref, hg_ref, route_ref, rk_ref, cnt_ref, wb_ref, carry_ref) = refs[nx + 8:]

    @pl.when(pl.program_id(0) == 0)
    def _():
        wb_ref[...] = w_ref[...].astype(BF16)
        carry_ref[...] = jnp.zeros_like(carry_ref)

    def residual(c):
        rows = slice(c * OUT_ROWS, (c + 1) * OUT_ROWS)
        mix = None
        for k in range(4):
            part = jnp.dot(_load_x(mixers[2 * k:2 * k + 2], rows), wb_ref[512 * k:512 * (k + 1), :],
                           preferred_element_type=F32)
            mix = part if mix is None else mix + part
        return _load_x(refs[:nx], rows) + g1_ref[...] * mix

    nchunk = TM // OUT_ROWS
    earlier = _earlier_matrix(OUT_ROWS)
    x_next = residual(0)
    for c in range(nchunk):
        rows = slice(c * OUT_ROWS, (c + 1) * OUT_ROWS)
        x = x_next
        x_next = residual(c + 1) if c + 1 < nchunk else None
        xo_ref[rows, :] = x
        ms = jnp.mean(x * x, axis=-1, keepdims=True)
        h = x * lax.rsqrt(ms + EPS) * gain_ref[...]
        h = h * (1.0 + sc_ref[...]) + sh_ref[...]
        h_hi = h.astype(BF16)
        h_lo = (h - h_hi.astype(F32)).astype(BF16)
        t = (jnp.dot(h_hi, wr_ref[...], preferred_element_type=F32)
             + jnp.dot(h_lo, wr_ref[...], preferred_element_type=F32))
        logits = t + pltpu.roll(t, LANES - ROUTER_LO_LANE, axis=1) + br_ref[...]
        route = _route(logits)
        route_ref[rows, :] = route
        rk_ref[:, rows] = _bucket_ranks(route, carry_ref, earlier)
        rec0 = c * OUT_ROWS * REC
        for s in range(H_ROWS):
            hg_ref[pl.ds(rec0 + s, OUT_ROWS, stride=REC), :] = h[:, s * LANES:(s + 1) * LANES]
    cnt_ref[...] = carry_ref[...]


def _outproj(mixers, xs, w_out, mod4, gain, wr, br, layer):
    tile = lambda w: pl.BlockSpec((TM, w), lambda i: (i, 0))
    const = lambda r, c: pl.BlockSpec((r, c), lambda i: (0, 0))
    mix_specs = []
    for _ in range(4):
        mix_specs += _pair_specs(512)
    return pl.pallas_call(
        functools.partial(_outproj_body, len(xs)),
        grid=(T_ALL // TM,),
        in_specs=_x_specs(len(xs)) + mix_specs + [_resident((2048, D), layer),
                                                  _mod_spec(2, TM), const(1, D), _mod_spec(4, TM), _mod_spec(3, TM),
                                                  const(D, LANES), const(1, LANES)],
        out_specs=[tile(D), pl.BlockSpec((TM * REC, LANES), lambda i: (i, 0)), tile(LANES),
                   pl.BlockSpec((SUBLANES, TM), lambda i: (i, 0)), const(1, LANES)],
        out_shape=[jax.ShapeDtypeStruct((T_ALL, D), F32),
                   jax.ShapeDtypeStruct((T_ALL * REC, LANES), F32),
                   jax.ShapeDtypeStruct((T_ALL, LANES), F32),
                   jax.ShapeDtypeStruct((T_ALL // TM * SUBLANES, TM), jnp.int32),
                   jax.ShapeDtypeStruct((1, LANES), F32)],
        scratch_shapes=[pltpu.VMEM((2048, D), BF16), pltpu.VMEM((1, LANES), F32)],
        compiler_params=_cparams(("arbitrary",), 56),
        name="out_proj",
    )(*xs, *[a for pair in mixers for a in pair], w_out, mod4, gain, mod4, mod4, wr, br)


def _wait_rows(copy, n):
    def body(t, c):
        copy.wait()
        return c
    lax.fori_loop(0, n, body, 0, unroll=8)


def _record(ref, idx, rows):
    start = idx * rows if isinstance(idx, int) else pl.multiple_of(idx * rows, rows)
    return ref.at[pl.ds(start, rows)]


def _dispatch_body(slot_sm, fstart_sm, flen_sm, hg_ref, route_ref, xs_hbm, gs_ref, zrec_ref, sem):
    i = pl.program_id(0)
    base = i * DT

    @pl.when(i == 0)
    def _():
        gs_ref[...] = jnp.zeros_like(gs_ref)
        zrec_ref[...] = jnp.zeros_like(zrec_ref)
        for g in range(N_EXPERT_GROUPS):
            def fill(r, c):
                pltpu.make_async_copy(zrec_ref, _record(xs_hbm, fstart_sm[g] + r, REC), sem.at[1]).start()
                return c
            lax.fori_loop(0, flen_sm[g], fill, 0)
        _wait_rows(pltpu.make_async_copy(zrec_ref, _record(xs_hbm, 0, REC), sem.at[1]), NSLOT - T_ALL)

    def issue(pair, c):
        for prio in range(2):
            t = 2 * pair + prio
            slot = slot_sm[base + t]
            pltpu.async_copy(_record(hg_ref, t, REC), _record(xs_hbm, slot, REC), sem.at[0], priority=prio)
            gs_ref[pl.ds(slot, 1), :] = route_ref[pl.ds(t, 1), :]
        return c
    lax.fori_loop(0, DT // 2, issue, 0, unroll=4)
    _wait_rows(pltpu.make_async_copy(_record(hg_ref, 0, REC), _record(xs_hbm, 0, REC), sem.at[0]), DT)


def _dispatch(slot, fstart, flen, hg, route):
    return pl.pallas_call(
        _dispatch_body,
        grid_spec=pltpu.PrefetchScalarGridSpec(
            num_scalar_prefetch=3,
            grid=(T_ALL // DT,),
            in_specs=[pl.BlockSpec((DT * REC, LANES), lambda i, s, f0, f1: (i, 0)),
                      pl.BlockSpec((DT, LANES), lambda i, s, f0, f1: (i, 0))],
            out_specs=[pl.BlockSpec(memory_space=pl.ANY),
                       pl.BlockSpec((NSLOT, LANES), lambda i, s, f0, f1: (0, 0), pipeline_mode=pl.Buffered(1))],
            scratch_shapes=[pltpu.VMEM((REC, LANES), F32), pltpu.SemaphoreType.DMA((2,))]),
        out_shape=[jax.ShapeDtypeStruct((NSLOT * REC, LANES), F32),
                   jax.ShapeDtypeStruct((NSLOT, LANES), F32)],
        compiler_params=_cparams(("arbitrary",), 48),
        name="moe_dispatch",
    )(slot, fstart, flen, hg, route)


def _ffn_body(tgrp_sm, xs_ref, gs_ref, wg32_ref, wu32_ref, wd32_ref, ys_ref, h_ref, wg_ref, wu_ref, wd_ref):
    j = pl.program_id(0)
    grp = tgrp_sm[j]

    @pl.when(jnp.logical_or(j == 0, grp != tgrp_sm[jnp.maximum(j - 1, 0)]))
    def _():
        for e in range(EXPERTS_PER_GROUP):
            cols = slice(e * EXPERT_FF, (e + 1) * EXPERT_FF)
            wg_ref[:, cols] = wg32_ref[e].astype(BF16)
            wu_ref[:, cols] = wu32_ref[e].astype(BF16)
            wd_ref[cols, :] = wd32_ref[e].astype(BF16)

    for s in range(H_ROWS):
        h_ref[:, s * LANES:(s + 1) * LANES] = xs_ref[pl.ds(s, TS, stride=REC), :].astype(BF16)
    gates = gs_ref[...]
    h = h_ref[...]
    a = jnp.dot(h, wg_ref[...], preferred_element_type=F32)
    b = jnp.dot(h, wu_ref[...], preferred_element_type=F32)
    hid = _silu(a) * b
    lane = lax.broadcasted_iota(jnp.int32, gates.shape, 1)
    parts = []
    for e in range(EXPERTS_PER_GROUP):
        ge = jnp.sum(jnp.where(lane == grp * EXPERTS_PER_GROUP + e, gates, 0.0), axis=-1, keepdims=True)
        parts.append((hid[:, e * EXPERT_FF:(e + 1) * EXPERT_FF] * ge).astype(BF16))
    y = jnp.dot(jnp.concatenate(parts, axis=1), wd_ref[...], preferred_element_type=F32)
    for s in range(H_ROWS):
        ys_ref[pl.ds(s, TS, stride=H_ROWS), :] = y[:, s * LANES:(s + 1) * LANES]


def _ffn(tile_grp, xs, gs, w_gate, w_up, w_down, layer):
    ffw = EXPERTS_PER_GROUP * EXPERT_FF
    experts = lambda r, c: pl.BlockSpec((None, EXPERTS_PER_GROUP, r, c), lambda j, tg: (layer, tg[j], 0, 0))
    return pl.pallas_call(
        _ffn_body,
        grid_spec=pltpu.PrefetchScalarGridSpec(
            num_scalar_prefetch=1,
            grid=(NT,),
            in_specs=[pl.BlockSpec((TS * REC, LANES), lambda j, tg: (j, 0)),
                      pl.BlockSpec((TS, LANES), lambda j, tg: (j, 0)),
                      experts(D, EXPERT_FF), experts(D, EXPERT_FF), experts(EXPERT_FF, D)],
            out_specs=pl.BlockSpec((TS * H_ROWS, LANES), lambda j, tg: (j, 0)),
            scratch_shapes=[pltpu.VMEM((TS, D), BF16), pltpu.VMEM((D, ffw), BF16),
                            pltpu.VMEM((D, ffw), BF16), pltpu.VMEM((ffw, D), BF16)]),
        out_shape=jax.ShapeDtypeStruct((NSLOT * H_ROWS, LANES), F32),
        compiler_params=_cparams(("arbitrary",), 56),
        name="moe_ffn",
    )(tile_grp, xs, gs, w_gate, w_up, w_down)


def _combine_body(final, slot_sm, ys_hbm, x_ref, g2_ref, fg_ref, *rest):
    outs, buf, sem = rest[:-2], rest[-2], rest[-1]
    i = pl.program_id(0)
    n = pl.num_programs(0)

    def issue(step, b):
        def body(pair, c):
            for prio in range(2):
                t = 2 * pair + prio
                pltpu.async_copy(_record(ys_hbm, slot_sm[step * TM + t], H_ROWS),
                                 _record(buf.at[b], t, H_ROWS), sem.at[b], priority=prio)
            return c
        lax.fori_loop(0, TM // 2, body, 0, unroll=4)

    @pl.when(i == 0)
    def _():
        issue(0, 0)

    @pl.when(i + 1 < n)
    def _():
        issue(i + 1, (i + 1) % 2)

    b = i % 2
    _wait_rows(pltpu.make_async_copy(_record(ys_hbm, 0, H_ROWS), _record(buf.at[b], 0, H_ROWS), sem.at[b]), TM)
    y = jnp.concatenate([buf[b, pl.ds(s, TM, stride=H_ROWS), :] for s in range(H_ROWS)], axis=1)
    x = x_ref[...] + g2_ref[...] * y
    if not final:
        outs[0][...] = x
    else:
        ms = jnp.mean(x * x, axis=-1, keepdims=True)
        yn = x * lax.rsqrt(ms + EPS) * fg_ref[...]
        is_ctx = i < T_CTX // TM

        @pl.when(is_ctx)
        def _():
            outs[0][...] = yn

        @pl.when(jnp.logical_not(is_ctx))
        def _():
            outs[1][...] = yn


def _combine(slot, ys, x, mod4, final_gain, final):
    if final:
        nc = T_CTX // TM
        out_specs = [pl.BlockSpec((TM, D), lambda i, s: (jnp.minimum(i, nc - 1), 0)),
                     pl.BlockSpec((TM, D), lambda i, s: (jnp.maximum(i - nc, 0), 0))]
        out_shape = [jax.ShapeDtypeStruct((T_CTX, D), F32), jax.ShapeDtypeStruct((T_LAT, D), F32)]
    else:
        out_specs = pl.BlockSpec((TM, D), lambda i, s: (i, 0))
        out_shape = jax.ShapeDtypeStruct((T_ALL, D), F32)
    return pl.pallas_call(
        functools.partial(_combine_body, final),
        grid_spec=pltpu.PrefetchScalarGridSpec(
            num_scalar_prefetch=1,
            grid=(T_ALL // TM,),
            in_specs=[pl.BlockSpec(memory_space=pl.ANY),
                      pl.BlockSpec((TM, D), lambda i, s: (i, 0)),
                      pl.BlockSpec((None, None, 1, D), lambda i, s: (_mod_row(i, TM), 5, 0, 0)),
                      pl.BlockSpec((1, D), lambda i, s: (0, 0))],
            out_specs=out_specs,
            scratch_shapes=[pltpu.VMEM((2, TM * H_ROWS, LANES), F32), pltpu.SemaphoreType.DMA((2,))]),
        out_shape=out_shape,
        compiler_params=_cparams(("arbitrary",), 48),
        name="moe_combine",
    )(slot, ys, x, mod4, final_gain)


def _moe_plan(rk, cnt):
    rk = rk.reshape(T_ALL // TM, SUBLANES, TM)
    rank = rk[:, 0, :].reshape(T_ALL)
    grp = rk[:, 1, :].reshape(T_ALL)
    cnt = cnt[0, N_EXPERTS:N_EXPERTS + N_EXPERT_GROUPS].astype(jnp.int32)
    padded = (cnt + TS - 1) // TS * TS
    off = jnp.cumsum(padded) - padded
    padded = padded.at[N_EXPERT_GROUPS - 1].set(NSLOT - off[N_EXPERT_GROUPS - 1])
    slot = rank
    for g in range(1, N_EXPERT_GROUPS):
        slot = slot + jnp.where(grp == g, off[g], 0)
    ends = (off + padded)[:N_EXPERT_GROUPS - 1]
    tile_grp = jnp.sum((jnp.arange(NT, dtype=jnp.int32)[:, None] * TS >= ends[None, :]).astype(jnp.int32), axis=1)
    return slot, tile_grp, off + cnt, padded - cnt


def _rope_tables():
    t = np.arange(DEC_SEQ)
    pos = np.stack([t // GRID_W, t % GRID_W], axis=1).astype(np.float64)
    inv = ROPE_THETA ** (-np.arange(16, dtype=np.float64) / 16.0)
    d = np.arange(LANES) % HEAD_DIM
    pair = d // 2
    ang = pos[:, (pair >= 16).astype(np.int64)] * inv[pair % 16][None, :]
    sign = np.where(d % 2 == 0, -1.0, 1.0)
    return np.cos(ang).astype(np.float32), (np.sin(ang) * sign[None, :]).astype(np.float32)


def _dft_pos(L):
    j = np.arange(L)
    ang = 2.0 * np.pi * ((j[:, None] * j[None, :]) % L) / L
    return np.concatenate([np.cos(ang), -np.sin(ang)], axis=1).astype(np.float32)


def _dft_ch():
    c = np.arange(FOURIER_HEAD_CH)
    ang = 2.0 * np.pi * ((c[:, None] * c[None, :]) % FOURIER_HEAD_CH) / FOURIER_HEAD_CH
    return np.concatenate([np.cos(ang), np.sin(ang)], axis=1).astype(np.float32)


def kernel(x_prompt, x_sample, cache_k, cache_v, c, c_ctx, w_ada, b_ada, norm_mix, norm_ffn, w_in, w_out, q_norm, k_norm, conv_w, conv_b, conv_ln_g, conv_ln_b, pool_w, pool_scale, router_group_w, router_group_b, router_expert_w, router_expert_b, moe_w_gate, moe_w_up, moe_w_down, final_norm):
    xs = (x_prompt.reshape(T_CTX, D), x_sample.reshape(T_LAT, D))
    cond8 =jnp.concatenate([c_ctx[None, :], c, jnp.zeros((3, D), F32)], axis=0)
    mod = _ada(cond8, w_ada, b_ada)

    pw_bf = pool_w.astype(BF16)
    n_route = N_EXPERTS + N_EXPERT_GROUPS
    wr = jnp.concatenate([router_expert_w, router_group_w], axis=-1)
    wr_hi = wr.astype(BF16)
    wr_lo = (wr - wr_hi.astype(F32)).astype(BF16)
    w_router = jnp.concatenate([wr_hi, jnp.zeros((DEPTH, D, ROUTER_LO_LANE - n_route), BF16), wr_lo,
                                jnp.zeros((DEPTH, D, LANES - ROUTER_LO_LANE - n_route), BF16)], axis=-1)
    b_router = jnp.concatenate([router_expert_b, router_group_b,
                                jnp.zeros((DEPTH, LANES - N_EXPERTS - N_EXPERT_GROUPS), F32)], axis=-1)
    conv_w32 = jnp.concatenate([conv_w, jnp.zeros((DEPTH, 1, CONV_CH), F32)], axis=1)

    cos_np, sin_np = _rope_tables()
    cos, sin = jnp.asarray(cos_np), jnp.asarray(sin_np)
    wc_bf = jnp.asarray(_dft_ch()).astype(BF16)
    wl_ctx = jnp.asarray(_dft_pos(SEQ)).astype(BF16)
    wl_lat = jnp.asarray(_dft_pos(DEC_SEQ)).astype(BF16)

    cache_k4 = cache_k.reshape(DEC_BATCH, DEPTH, PAST_LEN, KV_CH)
    cache_v4 = cache_v.reshape(DEC_BATCH, DEPTH, PAST_LEN, KV_CH)
    new_k = jnp.zeros((BATCH, DEPTH, SEQ, KV_CH), F32)
    new_v = jnp.zeros((BATCH, DEPTH, SEQ, KV_CH), F32)
    fgain = final_norm.reshape(1, D)

    for l in range(DEPTH):
        mod4 = mod[l].reshape(8, N_MOD, 1, D)
        proj, pf = _inproj(xs, norm_mix[l].reshape(1, D), mod4, w_in, l)
        qg = jnp.tile(q_norm[l], 2).reshape(1, LANES)
        kg = jnp.tile(k_norm[l], 2).reshape(1, LANES)
        attn_c, new_k, new_v = _ctx_attn(proj, qg, kg, new_k, new_v, l)
        attn_l = _lat_attn(proj, cache_k4, cache_v4, qg, kg, cos, sin, l)
        cb, lg, lb = conv_b[l].reshape(1, -1), conv_ln_g[l].reshape(1, -1), conv_ln_b[l].reshape(1, -1)
        ps = pool_scale[l].reshape(1, -1)
        lat0 = T_CTX // DEC_SEQ
        mixers = [(attn_c, attn_l),
                  (_conv(proj, conv_w32[l], cb, lg, lb, SEQ, BATCH, 0),
                   _conv(proj, conv_w32[l], cb, lg, lb, DEC_SEQ, DEC_BATCH, lat0)),
                  (_pool(proj, pw_bf[l], ps, SEQ, BATCH, 0, MIX_G),
                   _pool(proj, pw_bf[l], ps, DEC_SEQ, DEC_BATCH, lat0, 1)),
                  (_four(pf, wc_bf, wl_ctx, SEQ, BATCH, 0, MIX_G),
                   _four(pf, wc_bf, wl_lat, DEC_SEQ, DEC_BATCH, lat0, 1))]
        x, hg, route, rk, cnt = _outproj(mixers, xs, w_out, mod4, norm_ffn[l].reshape(1, D),
                                         w_router[l], b_router[l].reshape(1, LANES), l)
        slot, tile_grp, fstart, flen = _moe_plan(rk, cnt)
        xslots, gslots = _dispatch(slot, fstart, flen, hg, route)
        yslots = _ffn(tile_grp, xslots, gslots, moe_w_gate, moe_w_up, moe_w_down, l)
        out = _combine(slot, yslots, x, mod4, fgain, l == DEPTH - 1)
        xs = (out,)

    y_prompt = out[0].reshape(BATCH, SEQ, D)
    y_sample = out[1].reshape(DEC_BATCH, DEC_SEQ, D)
    new_k = new_k.reshape(BATCH, DEPTH, SEQ, 2, HEAD_DIM)
    new_v = new_v.reshape(BATCH, DEPTH, SEQ, 2, HEAD_DIM)
    return (y_prompt, y_sample, new_k, new_v)
```
